```python
import jax, jax.numpy as jnp
from jax import lax
import numpy as np

D_MODEL = 1024
BATCH = 8
SEQ = 2048
DEPTH = 2
DEC_BATCH = 32
DEC_SEQ = 32
PAST_LEN = 1024

CHUNK = 64
PLE_DIM = 256
EPS = 1e-6
N_BRANCH = 3
A_HEAD_DIM = 64
A_HEADS = D_MODEL // 128
A_KV_HEADS = 2
A_WIDTH = A_HEADS * A_HEAD_DIM
A_KV_WIDTH = A_KV_HEADS * A_HEAD_DIM
IDX_HEADS = 4
IDX_DIM = 64
IDX_W_SCALE = (IDX_HEADS * IDX_DIM) ** -0.5
TOPK_MAX = 256
Q_BLOCK = 128
B_WIDTH = D_MODEL // 2
B_GROUPS = 4
B_CHUNK = 128
C_WIDTH = D_MODEL // 2
C_CONV = 31
SPLIT_WIDTHS = (A_WIDTH, A_KV_WIDTH, A_KV_WIDTH, IDX_HEADS * IDX_DIM, IDX_DIM, IDX_HEADS, A_WIDTH,
                2 * B_WIDTH, B_WIDTH, 2 * C_WIDTH, C_WIDTH, N_BRANCH * D_MODEL)
IN_COLS = sum(SPLIT_WIDTHS)

kernel_name = "hybrid_dsa_gmlp_conformer_stream_step"


def rms_norm(x, g):
    xf = x.astype(jnp.float32)
    y = xf * lax.rsqrt(jnp.mean(jnp.square(xf), axis=-1, keepdims=True) + EPS)
    return (y * g.astype(jnp.float32)).astype(x.dtype)


def layer_norm(x, g, b):
    xf = x.astype(jnp.float32)
    xc = xf - jnp.mean(xf, axis=-1, keepdims=True)
    var = jnp.mean(jnp.square(xc), axis=-1, keepdims=True)
    return (xc * lax.rsqrt(var + EPS) * g.astype(jnp.float32) + b.astype(jnp.float32)).astype(x.dtype)


def split_columns(proj):
    cuts, acc = [], 0
    for w in SPLIT_WIDTHS[:-1]:
        acc += w
        cuts.append(acc)
    return jnp.split(proj, cuts, axis=-1)


def alibi_slopes():
    return jnp.asarray(2.0 ** (-8.0 * np.arange(1, A_HEADS + 1) / A_HEADS), dtype=jnp.float32)


def dsa_attend(q, iq, iw, k, v, ik, q_pos, topk):
    bsz, t = q.shape[:2]
    k_pos = jnp.arange(k.shape[1], dtype=jnp.int32)
    visible = (k_pos[None, :] // CHUNK) <= (q_pos[:, None] // CHUNK)
    rel = jax.nn.relu(jnp.einsum('bthd,bsd->bths', iq, ik).astype(jnp.float32))
    score = jnp.einsum('bths,bth->bts', rel, iw.astype(jnp.float32))
    score = jnp.where(visible[None], score, -jnp.inf)
    _, sel = lax.top_k(score, topk)
    valid = (sel // CHUNK) <= (q_pos[None, :, None] // CHUNK)
    kg = jax.vmap(lambda kk, ss: kk[ss])(k, sel)
    vg = jax.vmap(lambda vv, ss: vv[ss])(v, sel)
    qg = q.reshape(bsz, t, A_KV_HEADS, A_HEADS // A_KV_HEADS, A_HEAD_DIM)
    logits = jnp.einsum('btgrd,btkgd->btgrk', qg, kg).astype(jnp.float32) * (A_HEAD_DIM ** -0.5)
    dist = jnp.abs(q_pos[None, :, None] - sel).astype(jnp.float32)
    slopes = alibi_slopes().reshape(A_KV_HEADS, A_HEADS // A_KV_HEADS)
    logits = logits - slopes[None, None, :, :, None] * dist[:, :, None, None, :]
    logits = jnp.where(valid[:, :, None, None, :], logits, -jnp.inf)
    prob = jax.nn.softmax(logits, axis=-1).astype(v.dtype)
    out = jnp.einsum('btgrk,btkgd->btgrd', prob, vg)
    return out.reshape(bsz, t, A_WIDTH)


def dsa_prompt(q, iq, iw, k, v, ik):
    bsz, s = q.shape[:2]
    nb = s // Q_BLOCK
    topk = min(TOPK_MAX, s // 4)

    def blocks(a):
        return jnp.moveaxis(a.reshape((bsz, nb, Q_BLOCK) + a.shape[2:]), 1, 0)

    pos = jnp.arange(s, dtype=jnp.int32).reshape(nb, Q_BLOCK)
    out = lax.map(lambda xs: dsa_attend(xs[0], xs[1], xs[2], k, v, ik, xs[3], topk),
                  (blocks(q), blocks(iq), blocks(iw), pos))
    return jnp.moveaxis(out, 0, 1).reshape(bsz, s, A_WIDTH)


def gmlp_unit(uv, ln_g, ln_b, ws, bs):
    u, v = jnp.split(jax.nn.gelu(uv, approximate=False), 2, axis=-1)
    v = layer_norm(v, ln_g, ln_b)
    bsz, t = v.shape[:2]
    n = min(t, B_CHUNK)
    i = jnp.arange(B_CHUNK)
    mask = (i[None, :] // CHUNK) <= (i[:, None] // CHUNK)
    w = (ws * mask[None].astype(ws.dtype))[:, :n, :n]
    vc = v.reshape(bsz, t // n, n, B_GROUPS, B_WIDTH // B_GROUPS)
    mixed = jnp.einsum('gij,bnjgc->bnigc', w, vc) + bs[:, :n].T[None, None, :, :, None]
    return u * mixed.reshape(bsz, t, B_WIDTH), v


def conv_module(glu_in, left, w, b, ln_g, ln_b):
    a, g = jnp.split(glu_in, 2, axis=-1)
    c = a * jax.nn.sigmoid(g)
    ext = jnp.concatenate([left.astype(c.dtype), c], axis=1)
    y = lax.conv_general_dilated(ext, w[:, None, :], window_strides=(1,), padding='VALID',
                                 dimension_numbers=('NWC', 'WIO', 'NWC'),
                                 feature_group_count=ext.shape[-1]) + b
    y = jax.nn.silu(layer_norm(y, ln_g, ln_b))
    return y, ext[:, -(C_CONV - 1):]


def layer_forward(x, p, past, lw):
    (norm_g, w_in, g_ln_g, g_ln_b, g_ws, g_bs, c_w, c_b, c_ln_g, c_ln_b,
     w_a, w_b, w_c, w_out, ple_g, w_pg, w_ple) = lw
    bsz, t = x.shape[:2]
    (aq, ak, av, iq, ik, iw, za, uv, zb, glu, zc, gates) = split_columns(rms_norm(x, norm_g) @ w_in)
    q = aq.reshape(bsz, t, A_HEADS, A_HEAD_DIM)
    k = ak.reshape(bsz, t, A_KV_HEADS, A_HEAD_DIM)
    v = av.reshape(bsz, t, A_KV_HEADS, A_HEAD_DIM)
    iq = iq.reshape(bsz, t, IDX_HEADS, IDX_DIM)
    iw = iw * IDX_W_SCALE
    if past is None:
        o_a = dsa_prompt(q, iq, iw, k, v, ik)
        left = jnp.zeros((bsz, C_CONV - 1, C_WIDTH), x.dtype)
    else:
        ck, cv, cik, left = past
        past_len = ck.shape[1]
        q_pos = past_len + jnp.arange(t, dtype=jnp.int32)
        o_a = dsa_attend(q, iq, iw,
                         jnp.concatenate([ck.astype(k.dtype), k], axis=1),
                         jnp.concatenate([cv.astype(v.dtype), v], axis=1),
                         jnp.concatenate([cik.astype(ik.dtype), ik], axis=1),
                         q_pos, min(TOPK_MAX, (past_len + t) // 4))
    o_b, v_rows = gmlp_unit(uv, g_ln_g, g_ln_b, g_ws, g_bs)
    o_c, conv_tail = conv_module(glu, left, c_w, c_b, c_ln_g, c_ln_b)
    ga, gb, gc = jnp.split(jax.nn.sigmoid(gates), N_BRANCH, axis=-1)
    merged = (ga * ((o_a * jax.nn.silu(za)) @ w_a)
              + gb * ((o_b * jax.nn.silu(zb)) @ w_b)
              + gc * ((o_c * jax.nn.silu(zc)) @ w_c))
    x = x + merged @ w_out
    x = x + jax.nn.sigmoid(rms_norm(x, ple_g) @ w_pg) * (p @ w_ple)
    return x, k, v, ik, conv_tail, v_rows


def setup_inputs(seed: int = 0) -> dict:
    key = jax.random.key(seed)
    ks = jax.random.split(key, 32)
    f32 = jnp.float32

    def nrm(k, shape, scale):
        return jax.random.normal(k, shape, f32) * scale

    return {
        'x_prompt': nrm(ks[0], (BATCH, SEQ, D_MODEL), 1.0),
        'x_sample': nrm(ks[1], (DEC_BATCH, DEC_SEQ, D_MODEL), 1.0),
        'cache_k': nrm(ks[2], (DEPTH, DEC_BATCH, PAST_LEN, A_KV_HEADS, A_HEAD_DIM), 1.0),
        'cache_v': nrm(ks[3], (DEPTH, DEC_BATCH, PAST_LEN, A_KV_HEADS, A_HEAD_DIM), 1.0),
        'cache_idx_k': nrm(ks[4], (DEPTH, DEC_BATCH, PAST_LEN, IDX_DIM), 1.0),
        'state_conv': nrm(ks[5], (DEPTH, DEC_BATCH, C_CONV - 1, C_WIDTH), 0.5),
        'p_prompt': nrm(ks[6], (DEPTH, BATCH, SEQ, PLE_DIM), 1.0),
        'p_sample': nrm(ks[7], (DEPTH, DEC_BATCH, DEC_SEQ, PLE_DIM), 1.0),
        'norm_g': 1.0 + nrm(ks[8], (DEPTH, D_MODEL), 0.05),
        'w_in': nrm(ks[9], (DEPTH, D_MODEL, IN_COLS), D_MODEL ** -0.5),
        'gmlp_ln_g': 1.0 + nrm(ks[10], (DEPTH, B_WIDTH), 0.05),
        'gmlp_ln_b': nrm(ks[11], (DEPTH, B_WIDTH), 0.02),
        'gmlp_ws': nrm(ks[12], (DEPTH, B_GROUPS, B_CHUNK, B_CHUNK), B_CHUNK ** -0.5),
        'gmlp_bs': 1.0 + nrm(ks[13], (DEPTH, B_GROUPS, B_CHUNK), 0.05),
        'conv_w': nrm(ks[14], (DEPTH, C_CONV, C_WIDTH), C_CONV ** -0.5),
        'conv_b': nrm(ks[15], (DEPTH, C_WIDTH), 0.02),
        'conv_ln_g': 1.0 + nrm(ks[16], (DEPTH, C_WIDTH), 0.05),
        'conv_ln_b': nrm(ks[17], (DEPTH, C_WIDTH), 0.02),
        'w_branch_a': nrm(ks[18], (DEPTH, A_WIDTH, D_MODEL), A_WIDTH ** -0.5),
        'w_branch_b': nrm(ks[19], (DEPTH, B_WIDTH, D_MODEL), B_WIDTH ** -0.5),
        'w_branch_c': nrm(ks[20], (DEPTH, C_WIDTH, D_MODEL), C_WIDTH ** -0.5),
        'w_out': nrm(ks[21], (DEPTH, D_MODEL, D_MODEL), D_MODEL ** -0.5),
        'ple_norm_g': 1.0 + nrm(ks[22], (DEPTH, D_MODEL), 0.05),
        'w_ple_gate': nrm(ks[23], (DEPTH, D_MODEL, D_MODEL), D_MODEL ** -0.5),
        'w_ple': nrm(ks[24], (DEPTH, PLE_DIM, D_MODEL), PLE_DIM ** -0.5),
        'final_norm_g': 1.0 + nrm(ks[25], (D_MODEL,), 0.05),
    }


def reference(x_prompt, x_sample, cache_k, cache_v, cache_idx_k, state_conv, p_prompt, p_sample,
              norm_g, w_in, gmlp_ln_g, gmlp_ln_b, gmlp_ws, gmlp_bs, conv_w, conv_b, conv_ln_g, conv_ln_b,
              w_branch_a, w_branch_b, w_branch_c, w_out, ple_norm_g, w_ple_gate, w_ple, final_norm_g):
    xp, xs = x_prompt, x_sample
    kp, vp, ikp, cvp = [], [], [], []
    ks_, vs_, iks, cvs, gvs = [], [], [], [], []
    for l in range(DEPTH):
        lw = (norm_g[l], w_in[l], gmlp_ln_g[l], gmlp_ln_b[l], gmlp_ws[l], gmlp_bs[l],
              conv_w[l], conv_b[l], conv_ln_g[l], conv_ln_b[l],
              w_branch_a[l], w_branch_b[l], w_branch_c[l], w_out[l],
              ple_norm_g[l], w_ple_gate[l], w_ple[l])
        xp, k1, v1, ik1, c1, _ = layer_forward(xp, p_prompt[l], None, lw)
        kp.append(k1); vp.append(v1); ikp.append(ik1); cvp.append(c1)
        past = (cache_k[l], cache_v[l], cache_idx_k[l], state_conv[l])
        xs, k2, v2, ik2, c2, g2 = layer_forward(xs, p_sample[l], past, lw)
        ks_.append(k2); vs_.append(v2); iks.append(ik2); cvs.append(c2); gvs.append(g2)
    y_prompt = rms_norm(xp, final_norm_g)
    y_sample = rms_norm(xs, final_norm_g)
    return (y_prompt, y_sample,
            jnp.stack(kp), jnp.stack(vp), jnp.stack(ikp), jnp.stack(cvp),
            jnp.stack(ks_), jnp.stack(vs_), jnp.stack(iks), jnp.stack(cvs), jnp.stack(gvs))
```

```python
import functools

import jax
import jax.numpy as jnp
import numpy as np
from jax import lax
from jax.experimental import pallas as pl
from jax.experimental.pallas import tpu as pltpu

F32 = jnp.float32
BF16 = jnp.bfloat16

CHUNK = 64
EPS = 1e-6
A_HEAD_DIM = 64
A_HEADS = 8
A_KV_HEADS = 2
A_REP = A_HEADS // A_KV_HEADS
A_WIDTH = A_HEADS * A_HEAD_DIM
A_KV_WIDTH = A_KV_HEADS * A_HEAD_DIM
IDX_HEADS = 4
IDX_DIM = 64
IDX_W_SCALE = (IDX_HEADS * IDX_DIM) ** -0.5
TOPK_MAX = 256
B_GROUPS = 4
B_CHUNK = 128
C_CONV = 31
N_BRANCH = 3

QW = 896
KVW = 384
Q_IQ = A_WIDTH
Q_IW = A_WIDTH + IDX_HEADS * IDX_DIM
KV_V = A_KV_WIDTH
KV_IK = 2 * A_KV_WIDTH

VMEM_LIMIT = 56 * 1024 * 1024
ROW_TILE = 256
KEY_TILE = 256
SEARCH_STEPS = 8
SEARCH_ROUNDS = 48
NEG_INF = float("-inf")


def _cparams(sem):
    return pltpu.CompilerParams(dimension_semantics=sem, vmem_limit_bytes=VMEM_LIMIT)


def _rms(xf, g):
    return xf * lax.rsqrt(jnp.mean(jnp.square(xf), axis=-1, keepdims=True) + EPS) * g


def _ln(xf, g, b):
    xc = xf - jnp.mean(xf, axis=-1, keepdims=True)
    var = jnp.mean(jnp.square(xc), axis=-1, keepdims=True)
    return xc * lax.rsqrt(var + EPS) * g + b


def _silu(x):
    return x * jax.nn.sigmoid(x)


def _dot(a, b):
    return jnp.dot(a, b, preferred_element_type=F32)


def _dot_nt(a, b):
    return lax.dot_general(a, b, (((1,), (1,)), ((), ())), preferred_element_type=F32)


def _proj_kernel(x_ref, g_ref, w_ref, *out_refs, widths):
    h = _rms(x_ref[...], g_ref[...]).astype(BF16)
    off = 0
    for o_ref, wd in zip(out_refs, widths):
        for c in range(0, wd, 512):
            cw = min(512, wd - c)
            o_ref[:, c:c + cw] = _dot(h, w_ref[:, off + c:off + c + cw])
        off += wd


def _proj(x, g, w, widths):
    m, d = x.shape
    n = w.shape[1]
    return pl.pallas_call(
        functools.partial(_proj_kernel, widths=widths),
        grid=(m // ROW_TILE,),
        in_specs=[
            pl.BlockSpec((ROW_TILE, d), lambda i: (i, 0)),
            pl.BlockSpec((1, d), lambda i: (0, 0)),
            pl.BlockSpec((d, n), lambda i: (0, 0), pipeline_mode=pl.Buffered(1)),
        ],
        out_specs=[pl.BlockSpec((ROW_TILE, wd), lambda i: (i, 0)) for wd in widths],
        out_shape=[jax.ShapeDtypeStruct((m, wd), F32) for wd in widths],
        compiler_params=_cparams(("parallel",)),
        name="proj",
    )(x, g, w)


LANES = 128


def _stack_heads(x, n, dst):
    t = x.shape[0]
    keep = (lax.broadcasted_iota(jnp.int32, (t, LANES), 1) // A_HEAD_DIM) == dst
    parts = []
    for h in range(n):
        slab = x[:, (h // 2) * LANES:(h // 2 + 1) * LANES]
        if h % 2 != dst:
            slab = pltpu.roll(slab, A_HEAD_DIM, 1)
        parts.append(jnp.where(keep, slab, 0.0))
    return jnp.concatenate(parts, axis=0)


def _attend(qrow, za, load_k, load_v, load_ik, score_ref, *, tq, nt, q_pos0, n_keys, topk):
    tk = KEY_TILE
    iq_s = _stack_heads(qrow[:, Q_IQ:Q_IW], IDX_HEADS, 0).astype(BF16)
    iw = qrow[:, Q_IW:Q_IW + IDX_HEADS] * IDX_W_SCALE
    iw_s = jnp.concatenate([iw[:, h:h + 1] for h in range(IDX_HEADS)], axis=0)

    q_pos = q_pos0 + lax.broadcasted_iota(jnp.int32, (tq, 1), 0)
    q_chunk = q_pos // CHUNK
    lane = lax.broadcasted_iota(jnp.int32, (1, tk), 1)

    def visible(kt):
        k_pos = kt * tk + lane
        return ((k_pos // CHUNK) <= q_chunk) & (k_pos < n_keys), k_pos

    def score_body(kt, carry):
        mn, mx = carry
        ik_t = load_ik(pl.multiple_of(kt * tk, tk)).astype(BF16)
        r = jnp.maximum(_dot_nt(iq_s, ik_t), 0.0) * iw_s
        s = r[0:tq] + r[tq:2 * tq] + r[2 * tq:3 * tq] + r[3 * tq:4 * tq]
        vis, _ = visible(kt)
        score_ref[kt] = jnp.where(vis, s, NEG_INF)
        mn = jnp.minimum(mn, jnp.where(vis, s, jnp.inf))
        mx = jnp.maximum(mx, jnp.where(vis, s, NEG_INF))
        return mn, mx

    mn, mx = lax.fori_loop(0, nt, score_body,
                           (jnp.full((tq, tk), jnp.inf, F32), jnp.full((tq, tk), NEG_INF, F32)))
    lo0 = jnp.min(mn, axis=-1, keepdims=True)
    hi0 = jnp.max(mx, axis=-1, keepdims=True)

    n_vis = jnp.minimum((q_chunk + 1) * CHUNK, n_keys)
    k_eff = jnp.minimum(n_vis, topk).astype(F32)

    def count(pred):
        acc = lax.fori_loop(0, nt, lambda kt, a: a + jnp.where(pred(score_ref[kt]), 1.0, 0.0),
                            jnp.zeros((tq, tk), F32))
        return jnp.sum(acc, axis=-1, keepdims=True)

    def snap(lo):
        acc = lax.fori_loop(
            0, nt, lambda kt, a: jnp.minimum(a, jnp.where(score_ref[kt] >= lo, score_ref[kt], jnp.inf)),
            jnp.full((tq, tk), jnp.inf, F32))
        tau = jnp.min(acc, axis=-1, keepdims=True)
        c_gt = count(lambda s: s > tau)
        return tau, c_gt

    def search_round(state):
        it, _, lo, hi, _, _ = state
        for _ in range(SEARCH_STEPS):
            mid = lo + (hi - lo) * 0.5
            mid = jnp.where(mid <= lo, hi, mid)
            ge = count(lambda s: s >= mid) >= k_eff
            lo = jnp.where(ge, mid, lo)
            hi = jnp.where(ge, hi, mid)
        tau, c_gt = snap(lo)
        pending = jnp.max(jnp.where(c_gt < k_eff, 0.0, 1.0))
        return it + 1, pending, lo, hi, tau, c_gt

    init = (jnp.int32(0), jnp.float32(1.0), lo0, hi0, lo0, k_eff)
    _, _, _, _, tau, c_gt = lax.while_loop(
        lambda st: (st[0] < SEARCH_ROUNDS) & (st[1] > 0.0), search_round, init)
    need = k_eff - c_gt

    tri = (lax.broadcasted_iota(jnp.int32, (tk, tk), 0)
           <= lax.broadcasted_iota(jnp.int32, (tk, tk), 1)).astype(BF16)

    def select_body(kt, carry):
        s = score_ref[kt]
        eq = s == tau
        rank = _dot(jnp.where(eq, 1.0, 0.0).astype(BF16), tri) + carry
        sel = (s > tau) | (eq & (rank <= need))
        _, k_pos = visible(kt)
        dist = jnp.abs(q_pos - k_pos).astype(F32)
        score_ref[kt] = jnp.where(sel, -dist, NEG_INF)
        return carry + jnp.sum(jnp.where(eq, 1.0, 0.0), axis=-1, keepdims=True)

    lax.fori_loop(0, nt, select_body, jnp.zeros((tq, 1), F32))

    slopes = [2.0 ** (-8.0 * (h + 1) / A_HEADS) for h in range(A_HEADS)]
    outs = []
    for g in range(A_KV_HEADS):
        q_g = (_stack_heads(qrow[:, g * A_REP * A_HEAD_DIM:(g + 1) * A_REP * A_HEAD_DIM], A_REP, g)
               * (A_HEAD_DIM ** -0.5)).astype(BF16)
        slope_s = jnp.concatenate(
            [jnp.full((tq, 1), slopes[g * A_REP + r], F32) for r in range(A_REP)], axis=0)

        def attn_body(kt, carry, g=g, q_g=q_g, slope_s=slope_s):
            m, l, acc = carry
            row0 = pl.multiple_of(kt * tk, tk)
            k_t = load_k(row0).astype(BF16)
            v_t = load_v(row0).astype(BF16)
            nd = score_ref[kt]
            nd_s = jnp.concatenate([nd] * A_REP, axis=0)
            lg = _dot_nt(q_g, k_t) + slope_s * nd_s
            m_new = jnp.maximum(m, jnp.max(lg, axis=-1, keepdims=True))
            m_safe = jnp.where(m_new == NEG_INF, 0.0, m_new)
            p = jnp.exp(lg - m_safe)
            corr = jnp.exp(m - m_safe)
            l = l * corr + jnp.sum(p, axis=-1, keepdims=True)
            acc = acc * corr + _dot(p.astype(BF16), v_t)
            return m_new, l, acc

        m0 = jnp.full((A_REP * tq, 1), NEG_INF, F32)
        l0 = jnp.zeros((A_REP * tq, 1), F32)
        a0 = jnp.zeros((A_REP * tq, LANES), F32)
        _, l, acc = lax.fori_loop(0, nt, attn_body, (m0, l0, a0))
        o = acc[:, g * A_HEAD_DIM:(g + 1) * A_HEAD_DIM] / l
        outs.extend(o[r * tq:(r + 1) * tq] for r in range(A_REP))
    o_a = jnp.concatenate(outs, axis=-1)
    return o_a * _silu(za)


def _attn_prompt_kernel(q_ref, kv_ref, za_ref, o_ref, score_ref, *, tq, seq, topk):
    i = pl.program_id(1)
    nt = (i * tq + tq + KEY_TILE - 1) // KEY_TILE
    slab = lambda c0: (lambda row0: kv_ref[pl.ds(row0, KEY_TILE), c0:c0 + LANES])
    out = _attend(q_ref[...], za_ref[...], slab(0), slab(KV_V), slab(KV_IK), score_ref,
                  tq=tq, nt=nt, q_pos0=i * tq, n_keys=seq, topk=topk)
    o_ref[...] = out.astype(BF16)


def _attn_prompt(pq, pkv, za, batch, seq):
    tq = 128
    nq = seq // tq
    topk = min(TOPK_MAX, seq // 4)
    return pl.pallas_call(
        functools.partial(_attn_prompt_kernel, tq=tq, seq=seq, topk=topk),
        grid=(batch, nq),
        in_specs=[
            pl.BlockSpec((tq, QW), lambda b, i: (b * nq + i, 0)),
            pl.BlockSpec((seq, KVW), lambda b, i: (b, 0)),
            pl.BlockSpec((tq, A_WIDTH), lambda b, i: (b * nq + i, 0)),
        ],
        out_specs=pl.BlockSpec((tq, A_WIDTH), lambda b, i: (b * nq + i, 0)),
        out_shape=jax.ShapeDtypeStruct((batch * seq, A_WIDTH), BF16),
        scratch_shapes=[pltpu.VMEM((seq // KEY_TILE, tq, KEY_TILE), F32)],
        compiler_params=_cparams(("parallel", "arbitrary")),
        name="attn_prompt",
    )(pq, pkv, za)


def _attn_sample_kernel(q_ref, kv_ref, za_ref, ck_ref, cv_ref, cik_ref, o_ref,
                        k_buf, v_buf, ik_buf, score_ref, *, t, past, topk, nt):
    pad = nt * KEY_TILE - past - t
    k_buf[0:past, :] = ck_ref[...]
    v_buf[0:past, :] = cv_ref[...]
    ik_buf[0:past, 0:IDX_DIM] = cik_ref[...]
    ik_buf[0:past, IDX_DIM:] = jnp.zeros((past, LANES - IDX_DIM), F32)
    k_buf[past:past + t, :] = kv_ref[:, 0:A_KV_WIDTH]
    v_buf[past:past + t, :] = kv_ref[:, KV_V:KV_V + A_KV_WIDTH]
    ik_buf[past:past + t, :] = kv_ref[:, KV_IK:KV_IK + LANES]
    k_buf[past + t:, :] = jnp.zeros((pad, LANES), F32)
    v_buf[past + t:, :] = jnp.zeros((pad, LANES), F32)
    ik_buf[past + t:, :] = jnp.zeros((pad, LANES), F32)
    slab = lambda buf: (lambda row0: buf[pl.ds(row0, KEY_TILE), :])
    out = _attend(q_ref[...], za_ref[...], slab(k_buf), slab(v_buf), slab(ik_buf), score_ref,
                  tq=t, nt=nt, q_pos0=past, n_keys=past + t, topk=topk)
    o_ref[...] = out.astype(BF16)


def _attn_sample(pq, pkv, za, ck, cv, cik, row0, dec_batch, t):
    past = ck.shape[1]
    topk = min(TOPK_MAX, (past + t) // 4)
    nt = -(-(past + t) // KEY_TILE)
    blk0 = row0 // t
    return pl.pallas_call(
        functools.partial(_attn_sample_kernel, t=t, past=past, topk=topk, nt=nt),
        grid=(dec_batch,),
        in_specs=[
            pl.BlockSpec((t, QW), lambda b: (blk0 + b, 0)),
            pl.BlockSpec((t, KVW), lambda b: (blk0 + b, 0)),
            pl.BlockSpec((t, A_WIDTH), lambda b: (blk0 + b, 0)),
            pl.BlockSpec((None, past, A_KV_WIDTH), lambda b: (b, 0, 0)),
            pl.BlockSpec((None, past, A_KV_WIDTH), lambda b: (b, 0, 0)),
            pl.BlockSpec((None, past, IDX_DIM), lambda b: (b, 0, 0)),
        ],
        out_specs=pl.BlockSpec((t, A_WIDTH), lambda b: (b, 0)),
        out_shape=jax.ShapeDtypeStruct((dec_batch * t, A_WIDTH), BF16),
        scratch_shapes=[
            pltpu.VMEM((nt * KEY_TILE, LANES), F32),
            pltpu.VMEM((nt * KEY_TILE, LANES), F32),
            pltpu.VMEM((nt * KEY_TILE, LANES), F32),
            pltpu.VMEM((nt, t, KEY_TILE), F32),
        ],
        compiler_params=_cparams(("parallel",)),
        name="attn_sample",
    )(pq, pkv, za, ck, cv, cik)


def _gmlp_kernel(uv_ref, zb_ref, g_ref, b_ref, w_ref, bs_ref, o_ref, *v_out, width, mask_chunks):
    gw = width // B_GROUPS
    uv = uv_ref[...]
    act = 0.5 * uv * (1.0 + lax.erf(uv * np.float32(1.0 / np.sqrt(2.0))))
    u = act[:, :width]
    v = _ln(act[:, width:], g_ref[...], b_ref[...])
    if v_out:
        v_out[0][...] = v
    if mask_chunks:
        i = lax.broadcasted_iota(jnp.int32, (B_CHUNK, B_CHUNK), 0)
        j = lax.broadcasted_iota(jnp.int32, (B_CHUNK, B_CHUNK), 1)
        keep = (j // CHUNK) <= (i // CHUNK)
    vb = v.astype(BF16)
    zs = _silu(zb_ref[...])
    for g in range(B_GROUPS):
        w = w_ref[g]
        if mask_chunks:
            w = jnp.where(keep, w, 0.0)
        mixed = _dot(w.astype(BF16), vb[:, g * gw:(g + 1) * gw]) + bs_ref[g]
        o_ref[:, g * gw:(g + 1) * gw] = (u[:, g * gw:(g + 1) * gw] * mixed
                                         * zs[:, g * gw:(g + 1) * gw]).astype(BF16)


def _gmlp(uv, zb, ln_g, ln_b, w, bs, row0, rows, mask_chunks, want_v):
    width = zb.shape[1]
    blk0 = row0 // B_CHUNK
    out_specs = [pl.BlockSpec((B_CHUNK, width), lambda i: (i, 0))]
    out_shape = [jax.ShapeDtypeStruct((rows, width), BF16)]
    if want_v:
        out_specs.append(pl.BlockSpec((B_CHUNK, width), lambda i: (i, 0)))
        out_shape.append(jax.ShapeDtypeStruct((rows, width), F32))
    return pl.pallas_call(
        functools.partial(_gmlp_kernel, width=width, mask_chunks=mask_chunks),
        grid=(rows // B_CHUNK,),
        in_specs=[
            pl.BlockSpec((B_CHUNK, 2 * width), lambda i: (blk0 + i, 0)),
            pl.BlockSpec((B_CHUNK, width), lambda i: (blk0 + i, 0)),
            pl.BlockSpec((1, width), lambda i: (0, 0)),
            pl.BlockSpec((1, width), lambda i: (0, 0)),
            pl.BlockSpec((B_GROUPS, B_CHUNK, B_CHUNK), lambda i: (0, 0, 0)),
            pl.BlockSpec((B_GROUPS, B_CHUNK, 1), lambda i: (0, 0, 0)),
        ],
        out_specs=out_specs,
        out_shape=out_shape,
        compiler_params=_cparams(("parallel",)),
        name="gmlp_sample" if want_v else "gmlp_prompt",
    )(uv, zb, ln_g, ln_b, w, bs)


CONV_PAD = 32


def _conv_tile(glu, zc, ext_ref, w_ref, b_ref, g_ref, beta_ref, t, width):
    c = glu[:, :width] * jax.nn.sigmoid(glu[:, width:])
    ext_ref[CONV_PAD:CONV_PAD + t, :] = c
    base = CONV_PAD - (C_CONV - 1)
    y = jnp.zeros((t, width), F32) + b_ref[...]
    for k in range(C_CONV):
        y = y + ext_ref[base + k:base + k + t, :] * w_ref[k:k + 1, :]
    return _silu(_ln(y, g_ref[...], beta_ref[...])) * _silu(zc)


def _conv_prompt_kernel(glu_ref, zc_ref, w_ref, b_ref, g_ref, beta_ref, o_ref, tail_ref, ext_ref, *, t, width):
    i = pl.program_id(1)

    @pl.when(i == 0)
    def _():
        ext_ref[0:CONV_PAD, :] = jnp.zeros((CONV_PAD, width), F32)

    o_ref[...] = _conv_tile(glu_ref[...], zc_ref[...], ext_ref, w_ref, b_ref, g_ref, beta_ref, t, width).astype(BF16)
    tail = ext_ref[t:t + CONV_PAD, :]
    ext_ref[0:CONV_PAD, :] = tail

    @pl.when(i == pl.num_programs(1) - 1)
    def _():
        tail_ref[...] = tail[CONV_PAD - (C_CONV - 1):, :]


def _conv_prompt(glu, zc, w, b, ln_g, ln_b, batch, seq):
    t = ROW_TILE
    width = zc.shape[1]
    nb = seq // t
    return pl.pallas_call(
        functools.partial(_conv_prompt_kernel, t=t, width=width),
        grid=(batch, nb),
        in_specs=[
            pl.BlockSpec((t, 2 * width), lambda b_, i: (b_ * nb + i, 0)),
            pl.BlockSpec((t, width), lambda b_, i: (b_ * nb + i, 0)),
            pl.BlockSpec((C_CONV, width), lambda b_, i: (0, 0)),
            pl.BlockSpec((1, width), lambda b_, i: (0, 0)),
            pl.BlockSpec((1, width), lambda b_, i: (0, 0)),
            pl.BlockSpec((1, width), lambda b_, i: (0, 0)),
        ],
        out_specs=[
            pl.BlockSpec((t, width), lambda b_, i: (b_ * nb + i, 0)),
            pl.BlockSpec((None, C_CONV - 1, width), lambda b_, i: (b_, 0, 0)),
        ],
        out_shape=[
            jax.ShapeDtypeStruct((batch * seq, width), BF16),
            jax.ShapeDtypeStruct((batch, C_CONV - 1, width), F32),
        ],
        scratch_shapes=[pltpu.VMEM((CONV_PAD + t, width), F32)],
        compiler_params=_cparams(("parallel", "arbitrary")),
        name="conv_prompt",
    )(glu, zc, w, b, ln_g, ln_b)


def _conv_sample_kernel(glu_ref, zc_ref, st_ref, w_ref, b_ref, g_ref, beta_ref, o_ref, tail_ref, ext_ref, *, t, width):
    base = CONV_PAD - (C_CONV - 1)
    ext_ref[0:base, :] = jnp.zeros((base, width), F32)
    ext_ref[base:CONV_PAD, :] = st_ref[...]
    o_ref[...] = _conv_tile(glu_ref[...], zc_ref[...], ext_ref, w_ref, b_ref, g_ref, beta_ref, t, width).astype(BF16)
    tail_ref[...] = ext_ref[t + base:t + CONV_PAD, :]


def _conv_sample(glu, zc, state, w, b, ln_g, ln_b, row0, dec_batch, t):
    width = zc.shape[1]
    blk0 = row0 // t
    return pl.pallas_call(
        functools.partial(_conv_sample_kernel, t=t, width=width),
        grid=(dec_batch,),
        in_specs=[
            pl.BlockSpec((t, 2 * width), lambda b_: (blk0 + b_, 0)),
            pl.BlockSpec((t, width), lambda b_: (blk0 + b_, 0)),
            pl.BlockSpec((None, C_CONV - 1, width), lambda b_: (b_, 0, 0)),
            pl.BlockSpec((C_CONV, width), lambda b_: (0, 0)),
            pl.BlockSpec((1, width), lambda b_: (0, 0)),
            pl.BlockSpec((1, width), lambda b_: (0, 0)),
            pl.BlockSpec((1, width), lambda b_: (0, 0)),
        ],
        out_specs=[
            pl.BlockSpec((t, width), lambda b_: (b_, 0)),
            pl.BlockSpec((None, C_CONV - 1, width), lambda b_: (b_, 0, 0)),
        ],
        out_shape=[
            jax.ShapeDtypeStruct((dec_batch * t, width), BF16),
            jax.ShapeDtypeStruct((dec_batch, C_CONV - 1, width), F32),
        ],
        scratch_shapes=[pltpu.VMEM((CONV_PAD + t, width), F32)],
        compiler_params=_cparams(("parallel",)),
        name="conv_sample",
    )(glu, zc, state, w, b, ln_g, ln_b)


def _merge_kernel(a_ref, b_ref, c_ref, gates_ref, x_ref, p_ref, wa_ref, wb_ref, wc_ref, wo_ref,
                  pg_ref, wpg_ref, wple_ref, fg_ref, o_ref, *, d, final):
    merged = (jax.nn.sigmoid(gates_ref[:, 0:d]) * _dot(a_ref[...], wa_ref[...])
              + jax.nn.sigmoid(gates_ref[:, d:2 * d]) * _dot(b_ref[...], wb_ref[...])
              + jax.nn.sigmoid(gates_ref[:, 2 * d:3 * d]) * _dot(c_ref[...], wc_ref[...]))
    x = x_ref[...] + _dot(merged.astype(BF16), wo_ref[...])
    gate = jax.nn.sigmoid(_dot(_rms(x, pg_ref[...]).astype(BF16), wpg_ref[...]))
    x = x + gate * _dot(p_ref[...].astype(BF16), wple_ref[...])
    if final:
        x = _rms(x, fg_ref[...])
    o_ref[...] = x


def _merge(a, b, c, gates, x, p, wa, wb, wc, wo, pg, wpg, wple, fg, final):
    m, d = x.shape
    t = ROW_TILE
    row = lambda wd: pl.BlockSpec((t, wd), lambda i: (i, 0))
    full = lambda arr: pl.BlockSpec(arr.shape, lambda i: (0, 0), pipeline_mode=pl.Buffered(1))
    return pl.pallas_call(
        functools.partial(_merge_kernel, d=d, final=final),
        grid=(m // t,),
        in_specs=[row(a.shape[1]), row(b.shape[1]), row(c.shape[1]), row(3 * d), row(d), row(p.shape[1]),
                  full(wa), full(wb), full(wc), full(wo), full(pg), full(wpg), full(wple), full(fg)],
        out_specs=row(d),
        out_shape=jax.ShapeDtypeStruct((m, d), F32),
        compiler_params=_cparams(("parallel",)),
        name="merge",
    )(a, b, c, gates, x, p, wa, wb, wc, wo, pg, wpg, wple, fg)


def kernel(x_prompt, x_sample, cache_k, cache_v, cache_idx_k, state_conv, p_prompt, p_sample, norm_g, w_in, gmlp_ln_g, gmlp_ln_b, gmlp_ws, gmlp_bs, conv_w, conv_b, conv_ln_g, conv_ln_b, w_branch_a, w_branch_b, w_branch_c, w_out, ple_norm_g, w_ple_gate, w_ple, final_norm_g):
    batch, seq, d = x_prompt.shape
    dec_batch, dec_seq, _ = x_sample.shape
    depth = w_in.shape[0]
    past = cache_k.shape[2]
    bw = gmlp_ln_g.shape[1]
    cw = conv_b.shape[1]
    mp = batch * seq
    ms = dec_batch * dec_seq
    assert mp % ROW_TILE == 0 and ms % ROW_TILE == 0 and seq % ROW_TILE == 0
    assert dec_seq <= CHUNK and B_CHUNK % dec_seq == 0 and past % CHUNK == 0

    widths = (A_WIDTH, A_KV_WIDTH, A_KV_WIDTH, IDX_HEADS * IDX_DIM, IDX_DIM, IDX_HEADS, A_WIDTH,
              2 * bw, bw, 2 * cw, cw, N_BRANCH * d)
    cuts = np.concatenate([[0], np.cumsum(widths)])
    out_widths = (QW, KVW, A_WIDTH, 2 * bw, bw, 2 * cw, cw, N_BRANCH * d)

    x = jnp.concatenate([x_prompt.reshape(mp, d), x_sample.reshape(ms, d)], axis=0)
    rep = B_CHUNK // dec_seq
    eye = jnp.eye(rep, dtype=F32)

    kp, vp, ikp, cvp, ks, vs, iks, cvs, gvs = ([] for _ in range(9))
    for l in range(depth):
        col = lambda j: w_in[l][:, cuts[j]:cuts[j + 1]]
        zpad = lambda n: jnp.zeros((d, n), F32)
        w1 = jnp.concatenate(
            [col(0), col(3), col(5), zpad(QW - Q_IW - IDX_HEADS),
             col(1), col(2), col(4), zpad(KVW - KV_IK - IDX_DIM),
             col(6), col(7), col(8), col(9), col(10), col(11)], axis=1).astype(BF16)
        pq, pkv, za, uv, zb, glu, zc, gates = _proj(x, norm_g[l][None, :], w1, out_widths)

        a_p = _attn_prompt(pq, pkv, za, batch, seq)
        a_s = _attn_sample(pq, pkv, za,
                           cache_k[l].reshape(dec_batch, past, A_KV_WIDTH),
                           cache_v[l].reshape(dec_batch, past, A_KV_WIDTH),
                           cache_idx_k[l], mp, dec_batch, dec_seq)

        ln_g, ln_b = gmlp_ln_g[l][None, :], gmlp_ln_b[l][None, :]
        (b_p,) = _gmlp(uv, zb, ln_g, ln_b, gmlp_ws[l], gmlp_bs[l][:, :, None], 0, mp, True, False)
        ws_s = jnp.einsum('ab,gij->gaibj', eye, gmlp_ws[l][:, :dec_seq, :dec_seq]).reshape(
            B_GROUPS, B_CHUNK, B_CHUNK)
        bs_s = jnp.tile(gmlp_bs[l][:, :dec_seq], (1, rep))[:, :, None]
        b_s, gv = _gmlp(uv, zb, ln_g, ln_b, ws_s, bs_s, mp, ms, False, True)

        cargs = (conv_w[l], conv_b[l][None, :], conv_ln_g[l][None, :], conv_ln_b[l][None, :])
        c_p, tail_p = _conv_prompt(glu, zc, *cargs, batch, seq)
        c_s, tail_s = _conv_sample(glu, zc, state_conv[l], *cargs, mp, dec_batch, dec_seq)

        p = jnp.concatenate([p_prompt[l].reshape(mp, -1), p_sample[l].reshape(ms, -1)], axis=0)
        x = _merge(jnp.concatenate([a_p, a_s], axis=0), jnp.concatenate([b_p, b_s], axis=0),
                   jnp.concatenate([c_p, c_s], axis=0), gates, x, p,
                   w_branch_a[l].astype(BF16), w_branch_b[l].astype(BF16), w_branch_c[l].astype(BF16),
                   w_out[l].astype(BF16), ple_norm_g[l][None, :], w_ple_gate[l].astype(BF16),
                   w_ple[l].astype(BF16), final_norm_g[None, :], l == depth - 1)

        kp.append(pkv[:mp, 0:A_KV_WIDTH].reshape(batch, seq, A_KV_HEADS, A_HEAD_DIM))
        vp.append(pkv[:mp, KV_V:KV_V + A_KV_WIDTH].reshape(batch, seq, A_KV_HEADS, A_HEAD_DIM))
        ikp.append(pkv[:mp, KV_IK:KV_IK + IDX_DIM].reshape(batch, seq, IDX_DIM))
        cvp.append(tail_p)
        ks.append(pkv[mp:, 0:A_KV_WIDTH].reshape(dec_batch, dec_seq, A_KV_HEADS, A_HEAD_DIM))
        vs.append(pkv[mp:, KV_V:KV_V + A_KV_WIDTH].reshape(dec_batch, dec_seq, A_KV_HEADS, A_HEAD_DIM))
        iks.append(pkv[mp:, KV_IK:KV_IK + IDX_DIM].reshape(dec_batch, dec_seq, IDX_DIM))
        cvs.append(tail_s)
        gvs.append(gv.reshape(dec_batch, dec_seq, bw))

    return (x[:mp].reshape(batch, seq, d), x[mp:].reshape(dec_batch, dec_seq, d),
            jnp.stack(kp), jnp.stack(vp), jnp.stack(ikp), jnp.stack(cvp),
            jnp.stack(ks), jnp.stack(vs), jnp.stack(iks), jnp.stack(cvs), jnp.stack(gvs))
```

```python
import functools

import jax
import jax.numpy as jnp
import numpy as np
from jax import lax
from jax.experimental import pallas as pl
from jax.experimental.pallas import tpu as pltpu

F32 = jnp.float32
BF16 = jnp.bfloat16

CHUNK = 64
EPS = 1e-6
A_HEAD_DIM = 64
A_HEADS = 8
A_KV_HEADS = 2
A_REP = A_HEADS // A_KV_HEADS
A_WIDTH = A_HEADS * A_HEAD_DIM
A_KV_WIDTH = A_KV_HEADS * A_HEAD_DIM
IDX_HEADS = 4
IDX_DIM = 64
IDX_W_SCALE = (IDX_HEADS * IDX_DIM) ** -0.5
TOPK_MAX = 256
B_GROUPS = 4
B_CHUNK = 128
C_CONV = 31
N_BRANCH = 3

QW = 896
KVW = 384
Q_IQ = A_WIDTH
Q_IW = A_WIDTH + IDX_HEADS * IDX_DIM
KV_V = A_KV_WIDTH
KV_IK = 2 * A_KV_WIDTH

VMEM_LIMIT = 56 * 1024 * 1024
ROW_TILE = 256
KEY_TILE = 256
SEARCH_STEPS = 8
SEARCH_ROUNDS = 48
NEG_INF = float("-inf")


def _cparams(sem):
    return pltpu.CompilerParams(dimension_semantics=sem, vmem_limit_bytes=VMEM_LIMIT)


def _rms(xf, g):
    return xf * lax.rsqrt(jnp.mean(jnp.square(xf), axis=-1, keepdims=True) + EPS) * g


def _ln(xf, g, b):
    xc = xf - jnp.mean(xf, axis=-1, keepdims=True)
    var = jnp.mean(jnp.square(xc), axis=-1, keepdims=True)
    return xc * lax.rsqrt(var + EPS) * g + b


def _silu(x):
    return x * jax.nn.sigmoid(x)


def _dot(a, b):
    return jnp.dot(a, b, preferred_element_type=F32)


def _dot_nt(a, b):
    return lax.dot_general(a, b, (((1,), (1,)), ((), ())), preferred_element_type=F32)


def _wprep_kernel(w_ref, o_ref, *, pieces):
    rows, n_out = o_ref.shape
    covered = 0
    for src, wd, dst in pieces:
        if dst > covered:
            o_ref[:, covered:dst] = jnp.zeros((rows, dst - covered), BF16)
        for c in range(0, wd, 1024):
            cw = min(1024, wd - c)
            o_ref[:, dst + c:dst + c + cw] = w_ref[:, src + c:src + c + cw].astype(BF16)
        covered = dst + wd
    if covered < n_out:
        o_ref[:, covered:] = jnp.zeros((rows, n_out - covered), BF16)


def _wprep(w, pieces, n_out):
    depth, d, n = w.shape
    return pl.pallas_call(
        functools.partial(_wprep_kernel, pieces=pieces),
        grid=(depth, d // ROW_TILE),
        in_specs=[pl.BlockSpec((None, ROW_TILE, n), lambda l, i: (l, i, 0))],
        out_specs=pl.BlockSpec((None, ROW_TILE, n_out), lambda l, i: (l, i, 0)),
        out_shape=jax.ShapeDtypeStruct((depth, d, n_out), BF16),
        compiler_params=_cparams(("parallel", "parallel")),
        name="wprep",
    )(w)


def _proj_kernel(x_ref, g_ref, w_ref, *out_refs, widths):
    h = _rms(x_ref[...], g_ref[...]).astype(BF16)
    off = 0
    for o_ref, wd in zip(out_refs, widths):
        for c in range(0, wd, 512):
            cw = min(512, wd - c)
            o_ref[:, c:c + cw] = _dot(h, w_ref[:, off + c:off + c + cw])
        off += wd


def _proj(x, g, w, layer, widths):
    m, d = x.shape
    n = w.shape[2]
    return pl.pallas_call(
        functools.partial(_proj_kernel, widths=widths),
        grid=(m // ROW_TILE,),
        in_specs=[
            pl.BlockSpec((ROW_TILE, d), lambda i: (i, 0)),
            pl.BlockSpec((1, d), lambda i: (0, 0)),
            pl.BlockSpec((None, d, n), lambda i: (layer, 0, 0), pipeline_mode=pl.Buffered(1)),
        ],
        out_specs=[pl.BlockSpec((ROW_TILE, wd), lambda i: (i, 0)) for wd in widths],
        out_shape=[jax.ShapeDtypeStruct((m, wd), F32) for wd in widths],
        compiler_params=_cparams(("parallel",)),
        name="proj",
    )(x, g, w)


LANES = 128
TQ = 128


def _rows8(x):
    return [x[8 * j:8 * j + 8] for j in range(x.shape[0] // 8)]


def _tree(parts, op):
    while len(parts) > 1:
        nxt = [op(parts[2 * j], parts[2 * j + 1]) for j in range(len(parts) // 2)]
        if len(parts) % 2:
            nxt.append(parts[-1])
        parts = nxt
    return parts[0]


def _all8(x, op):
    for shift in (4, 2, 1):
        x = op(x, pltpu.roll(x, shift, 0))
    return x


def _fold(x, op):
    return _all8(_tree(_rows8(x), op), op)


def _per_rows(op, x, v8):
    return jnp.concatenate([op(part, v8) for part in _rows8(x)], axis=0)


def _head_slabs(x, n, dst):
    t = x.shape[0]
    keep = (lax.broadcasted_iota(jnp.int32, (t, LANES), 1) // A_HEAD_DIM) == dst
    parts = []
    for h in range(n):
        slab = x[:, (h // 2) * LANES:(h // 2 + 1) * LANES]
        if h % 2 != dst:
            slab = pltpu.roll(slab, A_HEAD_DIM, 1)
        parts.append(jnp.where(keep, slab, 0.0))
    return parts


def _fill_keys(kt, k, v, ik, kb_ref, vt_ref, ikb_ref):
    rows = pl.ds(kt * KEY_TILE, KEY_TILE)
    kb_ref[rows, :] = k.astype(BF16)
    ikb_ref[rows, :] = ik.astype(BF16)
    vt_ref[kt] = jnp.concatenate(
        [v[c:c + LANES, :].T for c in range(0, KEY_TILE, LANES)], axis=1).astype(BF16)


def _attend(qrow, za, kb_ref, vt_ref, ikb_ref, score_ref, lg_ref, p_ref, acc_ref, *, nt, q_pos0, n_keys, topk):
    tk = KEY_TILE
    iq_h = [s.astype(BF16) for s in _head_slabs(qrow[:, Q_IQ:Q_IW], IDX_HEADS, 0)]
    iw_t = qrow[:, Q_IW:Q_IW + LANES].T * IDX_W_SCALE
    iw_rows = [iw_t[h:h + 1, :] for h in range(IDX_HEADS)]

    k_row = lax.broadcasted_iota(jnp.int32, (tk, TQ), 0)
    q_pos = q_pos0 + lax.broadcasted_iota(jnp.int32, (tk, TQ), 1)
    q_chunk = q_pos // CHUNK

    def visible(kt):
        k_pos = kt * tk + k_row
        return ((k_pos // CHUNK) <= q_chunk) & (k_pos < n_keys), k_pos

    def score_body(kt, carry):
        mn, mx = carry
        ik_t = ikb_ref[pl.ds(pl.multiple_of(kt * tk, tk), tk), :]
        s = jnp.zeros((tk, TQ), F32)
        for h in range(IDX_HEADS):
            s = s + jnp.maximum(_dot_nt(ik_t, iq_h[h]), 0.0) * iw_rows[h]
        vis, _ = visible(kt)
        score_ref[kt] = jnp.where(vis, s, NEG_INF)
        mn = jnp.minimum(mn, _tree(_rows8(jnp.where(vis, s, jnp.inf)), jnp.minimum))
        mx = jnp.maximum(mx, _tree(_rows8(jnp.where(vis, s, NEG_INF)), jnp.maximum))
        return mn, mx

    mn, mx = lax.fori_loop(0, nt, score_body,
                           (jnp.full((8, TQ), jnp.inf, F32), jnp.full((8, TQ), NEG_INF, F32)))
    mx = _all8(mx, jnp.maximum)

    q_pos8 = q_pos0 + lax.broadcasted_iota(jnp.int32, (8, TQ), 1)
    n_vis = jnp.minimum((q_pos8 // CHUNK + 1) * CHUNK, n_keys)
    k_eff = jnp.minimum(n_vis, topk).astype(F32)

    def count_ge(thr):
        def body(kt, acc):
            return acc + _tree([jnp.where(part >= thr, 1.0, 0.0) for part in _rows8(score_ref[kt])], jnp.add)
        return _all8(lax.fori_loop(0, nt, body, jnp.zeros((8, TQ), F32)), jnp.add)

    def band_open(lo, hi, still):
        def body(kt, carry):
            bmn, bmx = carry
            lows, highs = [], []
            for part in _rows8(score_ref[kt]):
                inb = (part >= lo) & (part < hi)
                lows.append(jnp.where(inb, part, jnp.inf))
                highs.append(jnp.where(inb, part, NEG_INF))
            return (jnp.minimum(bmn, _tree(lows, jnp.minimum)), jnp.maximum(bmx, _tree(highs, jnp.maximum)))
        bmn, bmx = lax.fori_loop(0, nt, body,
                                 (jnp.full((8, TQ), jnp.inf, F32), jnp.full((8, TQ), NEG_INF, F32)))
        distinct = _all8(bmn, jnp.minimum) < _all8(bmx, jnp.maximum)
        return jnp.max(jnp.where(still & distinct, 1.0, 0.0)).astype(F32)

    def search_round(state):
        it, _, lo, hi, c_lo, c_hi = state
        for _ in range(SEARCH_STEPS):
            mid = lo + (hi - lo) * 0.5
            inside = (mid > lo) & (mid < hi)
            c = count_ge(mid)
            up = inside & (c >= k_eff)
            dn = inside & (c < k_eff)
            lo = jnp.where(up, mid, lo)
            c_lo = jnp.where(up, c, c_lo)
            hi = jnp.where(dn, mid, hi)
            c_hi = jnp.where(dn, c, c_hi)
        still = c_lo != k_eff
        pending = jnp.max(jnp.where(still, 1.0, 0.0))
        pending = lax.cond(pending > 0.0, band_open, lambda *_: jnp.zeros((), F32), lo, hi, still)
        return it + 1, pending, lo, hi, c_lo, c_hi

    lo0 = _all8(mn, jnp.minimum)
    hi0 = mx + jnp.maximum(jnp.abs(mx), 1e-30) * (2.0 ** -10)
    init = (jnp.int32(0), jnp.float32(1.0), lo0, hi0, n_vis.astype(F32), jnp.zeros((8, TQ), F32))
    _, _, lo, hi, _, c_hi = lax.while_loop(
        lambda st: (st[0] < SEARCH_ROUNDS) & (st[1] > 0.0), search_round, init)
    lo_r, hi_r = lo[0:1], hi[0:1]
    need_r = (k_eff - c_hi)[0:1]

    tril = (lax.broadcasted_iota(jnp.int32, (tk, tk), 1)
            <= lax.broadcasted_iota(jnp.int32, (tk, tk), 0)).astype(BF16)

    def select_body(kt, carry):
        s = score_ref[kt]
        inb = (s >= lo_r) & (s < hi_r)
        band = jnp.where(inb, 1.0, 0.0)
        rank = _dot(tril, band.astype(BF16)) + carry
        sel = (s >= hi_r) | (inb & (rank <= need_r))
        _, k_pos = visible(kt)
        dist = jnp.abs(q_pos - k_pos).astype(F32)
        score_ref[kt] = jnp.where(sel, -dist, NEG_INF)
        return carry + jnp.sum(band, axis=0, keepdims=True)

    lax.fori_loop(0, nt, select_body, jnp.zeros((1, TQ), F32))

    slopes = [2.0 ** (-8.0 * (h + 1) / A_HEADS) for h in range(A_HEADS)]
    gw = A_REP * A_HEAD_DIM
    q_gs = []
    for g in range(A_KV_HEADS):
        q_g = jnp.concatenate(_head_slabs(qrow[:, g * gw:(g + 1) * gw], A_REP, g), axis=0)
        q_gs.append((q_g * (A_HEAD_DIM ** -0.5)).astype(BF16))
    acc_ref[...] = jnp.zeros(acc_ref.shape, F32)

    def attn_body(kt, carry):
        ms, ls = carry
        k_t = kb_ref[pl.ds(pl.multiple_of(kt * tk, tk), tk), :]
        v_t = vt_ref[kt]
        nd = score_ref[kt]
        for g in range(A_KV_HEADS):
            lg_ref[g] = _dot_nt(k_t, q_gs[g])
        ms_new, ls_new = [], []
        for g in range(A_KV_HEADS):
            corrs = []
            for r in range(A_REP):
                h = g * A_REP + r
                lgr = lg_ref[g, :, r * TQ:(r + 1) * TQ] + slopes[h] * nd
                m_new = jnp.maximum(ms[h], _fold(lgr, jnp.maximum))
                m_safe = jnp.where(m_new == NEG_INF, 0.0, m_new)
                p = _per_rows(lambda a, m: jnp.exp(a - m), lgr, m_safe)
                corr = jnp.exp(ms[h] - m_safe)
                ls_new.append(ls[h] * corr + _fold(p, jnp.add))
                ms_new.append(m_new)
                corrs.append(corr)
                p_ref[g, :, r * TQ:(r + 1) * TQ] = p.astype(BF16)
            pv = _dot(v_t[g * A_HEAD_DIM:(g + 1) * A_HEAD_DIM, :], p_ref[g])
            acc_ref[g] = _per_rows(jnp.multiply, acc_ref[g], jnp.concatenate(corrs, axis=1)) + pv
        return tuple(ms_new), tuple(ls_new)

    init = (tuple(jnp.full((8, TQ), NEG_INF, F32) for _ in range(A_HEADS)),
            tuple(jnp.zeros((8, TQ), F32) for _ in range(A_HEADS)))
    _, ls = lax.fori_loop(0, nt, attn_body, init)
    pieces = []
    for g in range(A_KV_HEADS):
        o_t = _per_rows(jnp.divide, acc_ref[g], jnp.concatenate(ls[g * A_REP:(g + 1) * A_REP], axis=1))
        for r in range(0, A_REP, 2):
            pair = jnp.concatenate([o_t[:, r * TQ:(r + 1) * TQ], o_t[:, (r + 1) * TQ:(r + 2) * TQ]], axis=0)
            pieces.append(pair.T)
    o_a = jnp.concatenate(pieces, axis=-1)
    return o_a * _silu(za)


def _attn_scratch(n_tiles):
    return [
        pltpu.VMEM((n_tiles * KEY_TILE, LANES), BF16),
        pltpu.VMEM((n_tiles, LANES, KEY_TILE), BF16),
        pltpu.VMEM((n_tiles * KEY_TILE, LANES), BF16),
        pltpu.VMEM((n_tiles, KEY_TILE, TQ), F32),
        pltpu.VMEM((A_KV_HEADS, KEY_TILE, A_REP * TQ), F32),
        pltpu.VMEM((A_KV_HEADS, KEY_TILE, A_REP * TQ), BF16),
        pltpu.VMEM((A_KV_HEADS, A_HEAD_DIM, A_REP * TQ), F32),
    ]


def _attn_prompt_kernel(q_ref, kv_ref, za_ref, o_ref, kb_ref, vt_ref, ikb_ref, *work, seq, topk):
    i = pl.program_id(1)

    @pl.when(i == 0)
    def _():
        for kt in range(seq // KEY_TILE):
            rows = slice(kt * KEY_TILE, (kt + 1) * KEY_TILE)
            _fill_keys(kt, kv_ref[rows, 0:LANES], kv_ref[rows, KV_V:KV_V + LANES],
                       kv_ref[rows, KV_IK:KV_IK + LANES], kb_ref, vt_ref, ikb_ref)

    nt = (i * TQ + TQ + KEY_TILE - 1) // KEY_TILE
    out = _attend(q_ref[...], za_ref[...], kb_ref, vt_ref, ikb_ref, *work,
                  nt=nt, q_pos0=i * TQ, n_keys=seq, topk=topk)
    o_ref[...] = out.astype(BF16)


def _attn_prompt(pq, pkv, za, batch, seq):
    nq = seq // TQ
    topk = min(TOPK_MAX, seq // 4)
    return pl.pallas_call(
        functools.partial(_attn_prompt_kernel, seq=seq, topk=topk),
        grid=(batch, nq),
        in_specs=[
            pl.BlockSpec((TQ, QW), lambda b, i: (b * nq + i, 0)),
            pl.BlockSpec((seq, KVW), lambda b, i: (b, 0)),
            pl.BlockSpec((TQ, A_WIDTH), lambda b, i: (b * nq + i, 0)),
        ],
        out_specs=pl.BlockSpec((TQ, A_WIDTH), lambda b, i: (b * nq + i, 0)),
        out_shape=jax.ShapeDtypeStruct((batch * seq, A_WIDTH), BF16),
        scratch_shapes=_attn_scratch(seq // KEY_TILE),
        compiler_params=_cparams(("parallel", "arbitrary")),
        name="attn_prompt",
    )(pq, pkv, za)


def _attn_sample_kernel(q_ref, kv_ref, za_ref, ck_ref, cv_ref, cik_ref, o_ref,
                        kb_ref, vt_ref, ikb_ref, *work, t, past, topk, nt):
    tk = KEY_TILE
    tail = nt * tk - past
    new = kv_ref[...]
    zeros = lambda n, w: jnp.zeros((n, w), F32)
    for kt in range(past // tk):
        rows = slice(kt * tk, (kt + 1) * tk)
        ik = jnp.concatenate([cik_ref[rows, :], zeros(tk, LANES - IDX_DIM)], axis=1)
        _fill_keys(kt, ck_ref[rows, :], cv_ref[rows, :], ik, kb_ref, vt_ref, ikb_ref)
    pad_rows = lambda x: jnp.concatenate([x, zeros(tail - t, LANES)], axis=0)
    _fill_keys(past // tk, pad_rows(new[:, 0:LANES]), pad_rows(new[:, KV_V:KV_V + LANES]),
               pad_rows(new[:, KV_IK:KV_IK + LANES]), kb_ref, vt_ref, ikb_ref)
    qrow = jnp.concatenate([q_ref[...], zeros(TQ - t, QW)], axis=0)
    za = jnp.concatenate([za_ref[...], zeros(TQ - t, A_WIDTH)], axis=0)
    out = _attend(qrow, za, kb_ref, vt_ref, ikb_ref, *work,
                  nt=nt, q_pos0=past, n_keys=past + t, topk=topk)
    o_ref[...] = out[0:t].astype(BF16)


def _attn_sample(pq, pkv, za, ck, cv, cik, row0, dec_batch, t):
    past = ck.shape[1]
    assert past % KEY_TILE == 0 and t <= KEY_TILE
    topk = min(TOPK_MAX, (past + t) // 4)
    nt = past // KEY_TILE + 1
    blk0 = row0 // t
    return pl.pallas_call(
        functools.partial(_attn_sample_kernel, t=t, past=past, topk=topk, nt=nt),
        grid=(dec_batch,),
        in_specs=[
            pl.BlockSpec((t, QW), lambda b: (blk0 + b, 0)),
            pl.BlockSpec((t, KVW), lambda b: (blk0 + b, 0)),
            pl.BlockSpec((t, A_WIDTH), lambda b: (blk0 + b, 0)),
            pl.BlockSpec((None, past, A_KV_WIDTH), lambda b: (b, 0, 0)),
            pl.BlockSpec((None, past, A_KV_WIDTH), lambda b: (b, 0, 0)),
            pl.BlockSpec((None, past, IDX_DIM), lambda b: (b, 0, 0)),
        ],
        out_specs=pl.BlockSpec((t, A_WIDTH), lambda b: (b, 0)),
        out_shape=jax.ShapeDtypeStruct((dec_batch * t, A_WIDTH), BF16),
        scratch_shapes=_attn_scratch(nt),
        compiler_params=_cparams(("parallel",)),
        name="attn_sample",
    )(pq, pkv, za, ck, cv, cik)


def _gmlp_kernel(uv_ref, zb_ref, g_ref, b_ref, w_ref, bs_ref, o_ref, *v_out, width, mask_chunks):
    gw = width // B_GROUPS
    uv = uv_ref[...]
    act = 0.5 * uv * (1.0 + lax.erf(uv * np.float32(1.0 / np.sqrt(2.0))))
    u = act[:, :width]
    v = _ln(act[:, width:], g_ref[...], b_ref[...])
    if v_out:
        v_out[0][...] = v
    if mask_chunks:
        i = lax.broadcasted_iota(jnp.int32, (B_CHUNK, B_CHUNK), 0)
        j = lax.broadcasted_iota(jnp.int32, (B_CHUNK, B_CHUNK), 1)
        keep = (j // CHUNK) <= (i // CHUNK)
    vb = v.astype(BF16)
    zs = _silu(zb_ref[...])
    for g in range(B_GROUPS):
        w = w_ref[g]
        if mask_chunks:
            w = jnp.where(keep, w, 0.0)
        mixed = _dot(w.astype(BF16), vb[:, g * gw:(g + 1) * gw]) + bs_ref[g]
        o_ref[:, g * gw:(g + 1) * gw] = (u[:, g * gw:(g + 1) * gw] * mixed
                                         * zs[:, g * gw:(g + 1) * gw]).astype(BF16)


def _gmlp(uv, zb, ln_g, ln_b, w, bs, row0, rows, mask_chunks, want_v):
    width = zb.shape[1]
    blk0 = row0 // B_CHUNK
    out_specs = [pl.BlockSpec((B_CHUNK, width), lambda i: (i, 0))]
    out_shape = [jax.ShapeDtypeStruct((rows, width), BF16)]
    if want_v:
        out_specs.append(pl.BlockSpec((B_CHUNK, width), lambda i: (i, 0)))
        out_shape.append(jax.ShapeDtypeStruct((rows, width), F32))
    return pl.pallas_call(
        functools.partial(_gmlp_kernel, width=width, mask_chunks=mask_chunks),
        grid=(rows // B_CHUNK,),
        in_specs=[
            pl.BlockSpec((B_CHUNK, 2 * width), lambda i: (blk0 + i, 0)),
            pl.BlockSpec((B_CHUNK, width), lambda i: (blk0 + i, 0)),
            pl.BlockSpec((1, width), lambda i: (0, 0)),
            pl.BlockSpec((1, width), lambda i: (0, 0)),
            pl.BlockSpec((B_GROUPS, B_CHUNK, B_CHUNK), lambda i: (0, 0, 0)),
            pl.BlockSpec((B_GROUPS, B_CHUNK, 1), lambda i: (0, 0, 0)),
        ],
        out_specs=out_specs,
        out_shape=out_shape,
        compiler_params=_cparams(("parallel",)),
        name="gmlp_sample" if want_v else "gmlp_prompt",
    )(uv, zb, ln_g, ln_b, w, bs)


CONV_PAD = 32


def _conv_tile(glu, zc, ext_ref, w_ref, b_ref, g_ref, beta_ref, t, width):
    c = glu[:, :width] * jax.nn.sigmoid(glu[:, width:])
    ext_ref[CONV_PAD:CONV_PAD + t, :] = c
    base = CONV_PAD - (C_CONV - 1)
    y = jnp.zeros((t, width), F32) + b_ref[...]
    for k in range(C_CONV):
        y = y + ext_ref[base + k:base + k + t, :] * w_ref[k:k + 1, :]
    return _silu(_ln(y, g_ref[...], beta_ref[...])) * _silu(zc)


def _conv_prompt_kernel(glu_ref, zc_ref, w_ref, b_ref, g_ref, beta_ref, o_ref, tail_ref, ext_ref, *, t, width):
    i = pl.program_id(1)

    @pl.when(i == 0)
    def _():
        ext_ref[0:CONV_PAD, :] = jnp.zeros((CONV_PAD, width), F32)

    o_ref[...] = _conv_tile(glu_ref[...], zc_ref[...], ext_ref, w_ref, b_ref, g_ref, beta_ref, t, width).astype(BF16)
    tail = ext_ref[t:t + CONV_PAD, :]
    ext_ref[0:CONV_PAD, :] = tail

    @pl.when(i == pl.num_programs(1) - 1)
    def _():
        tail_ref[...] = tail[CONV_PAD - (C_CONV - 1):, :]


def _conv_prompt(glu, zc, w, b, ln_g, ln_b, batch, seq):
    t = ROW_TILE
    width = zc.shape[1]
    nb = seq // t
    return pl.pallas_call(
        functools.partial(_conv_prompt_kernel, t=t, width=width),
        grid=(batch, nb),
        in_specs=[
            pl.BlockSpec((t, 2 * width), lambda b_, i: (b_ * nb + i, 0)),
            pl.BlockSpec((t, width), lambda b_, i: (b_ * nb + i, 0)),
            pl.BlockSpec((C_CONV, width), lambda b_, i: (0, 0)),
            pl.BlockSpec((1, width), lambda b_, i: (0, 0)),
            pl.BlockSpec((1, width), lambda b_, i: (0, 0)),
            pl.BlockSpec((1, width), lambda b_, i: (0, 0)),
        ],
        out_specs=[
            pl.BlockSpec((t, width), lambda b_, i: (b_ * nb + i, 0)),
            pl.BlockSpec((None, C_CONV - 1, width), lambda b_, i: (b_, 0, 0)),
        ],
        out_shape=[
            jax.ShapeDtypeStruct((batch * seq, width), BF16),
            jax.ShapeDtypeStruct((batch, C_CONV - 1, width), F32),
        ],
        scratch_shapes=[pltpu.VMEM((CONV_PAD + t, width), F32)],
        compiler_params=_cparams(("parallel", "arbitrary")),
        name="conv_prompt",
    )(glu, zc, w, b, ln_g, ln_b)


def _conv_sample_kernel(glu_ref, zc_ref, st_ref, w_ref, b_ref, g_ref, beta_ref, o_ref, tail_ref, ext_ref, *, t, width):
    base = CONV_PAD - (C_CONV - 1)
    ext_ref[0:base, :] = jnp.zeros((base, width), F32)
    ext_ref[base:CONV_PAD, :] = st_ref[...]
    o_ref[...] = _conv_tile(glu_ref[...], zc_ref[...], ext_ref, w_ref, b_ref, g_ref, beta_ref, t, width).astype(BF16)
    tail_ref[...] = ext_ref[t + base:t + CONV_PAD, :]


def _conv_sample(glu, zc, state, w, b, ln_g, ln_b, row0, dec_batch, t):
    width = zc.shape[1]
    blk0 = row0 // t
    return pl.pallas_call(
        functools.partial(_conv_sample_kernel, t=t, width=width),
        grid=(dec_batch,),
        in_specs=[
            pl.BlockSpec((t, 2 * width), lambda b_: (blk0 + b_, 0)),
            pl.BlockSpec((t, width), lambda b_: (blk0 + b_, 0)),
            pl.BlockSpec((None, C_CONV - 1, width), lambda b_: (b_, 0, 0)),
            pl.BlockSpec((C_CONV, width), lambda b_: (0, 0)),
            pl.BlockSpec((1, width), lambda b_: (0, 0)),
            pl.BlockSpec((1, width), lambda b_: (0, 0)),
            pl.BlockSpec((1, width), lambda b_: (0, 0)),
        ],
        out_specs=[
            pl.BlockSpec((t, width), lambda b_: (b_, 0)),
            pl.BlockSpec((None, C_CONV - 1, width), lambda b_: (b_, 0, 0)),
        ],
        out_shape=[
            jax.ShapeDtypeStruct((dec_batch * t, width), BF16),
            jax.ShapeDtypeStruct((dec_batch, C_CONV - 1, width), F32),
        ],
        scratch_shapes=[pltpu.VMEM((CONV_PAD + t, width), F32)],
        compiler_params=_cparams(("parallel",)),
        name="conv_sample",
    )(glu, zc, state, w, b, ln_g, ln_b)


def _merge_kernel(a_ref, b_ref, c_ref, gates_ref, x_ref, p_ref, wa_ref, wb_ref, wc_ref, wo_ref,
                  pg_ref, wpg_ref, wple_ref, fg_ref, o_ref, *, d, final):
    merged = (jax.nn.sigmoid(gates_ref[:, 0:d]) * _dot(a_ref[...], wa_ref[...])
              + jax.nn.sigmoid(gates_ref[:, d:2 * d]) * _dot(b_ref[...], wb_ref[...])
              + jax.nn.sigmoid(gates_ref[:, 2 * d:3 * d]) * _dot(c_ref[...], wc_ref[...]))
    x = x_ref[...] + _dot(merged.astype(BF16), wo_ref[...])
    gate = jax.nn.sigmoid(_dot(_rms(x, pg_ref[...]).astype(BF16), wpg_ref[...]))
    x = x + gate * _dot(p_ref[...].astype(BF16), wple_ref[...])
    if final:
        x = _rms(x, fg_ref[...])
    o_ref[...] = x


def _merge(a, b, c, gates, x, p, wa, wb, wc, wo, pg, wpg, wple, fg, final):
    m, d = x.shape
    t = ROW_TILE
    row = lambda wd: pl.BlockSpec((t, wd), lambda i: (i, 0))
    full = lambda arr: pl.BlockSpec(arr.shape, lambda i: (0, 0), pipeline_mode=pl.Buffered(1))
    return pl.pallas_call(
        functools.partial(_merge_kernel, d=d, final=final),
        grid=(m // t,),
        in_specs=[row(a.shape[1]), row(b.shape[1]), row(c.shape[1]), row(3 * d), row(d), row(p.shape[1]),
                  full(wa), full(wb), full(wc), full(wo), full(pg), full(wpg), full(wple), full(fg)],
        out_specs=row(d),
        out_shape=jax.ShapeDtypeStruct((m, d), F32),
        compiler_params=_cparams(("parallel",)),
        name="merge",
    )(a, b, c, gates, x, p, wa, wb, wc, wo, pg, wpg, wple, fg)


def kernel(x_prompt, x_sample, cache_k, cache_v, cache_idx_k, state_conv, p_prompt, p_sample, norm_g, w_in, gmlp_ln_g, gmlp_ln_b, gmlp_ws, gmlp_bs, conv_w, conv_b, conv_ln_g, conv_ln_b, w_branch_a, w_branch_b, w_branch_c, w_out, ple_norm_g, w_ple_gate, w_ple, final_norm_g):
    batch, seq, d = x_prompt.shape
    dec_batch, dec_seq, _ = x_sample.shape
    depth = w_in.shape[0]
    past = cache_k.shape[2]
    bw = gmlp_ln_g.shape[1]
    cw = conv_b.shape[1]
    mp = batch * seq
    ms = dec_batch * dec_seq
    assert mp % ROW_TILE == 0 and ms % ROW_TILE == 0 and seq % ROW_TILE == 0
    assert dec_seq <= CHUNK and B_CHUNK % dec_seq == 0 and past % CHUNK == 0

    widths = (A_WIDTH, A_KV_WIDTH, A_KV_WIDTH, IDX_HEADS * IDX_DIM, IDX_DIM, IDX_HEADS, A_WIDTH,
              2 * bw, bw, 2 * cw, cw, N_BRANCH * d)
    cuts = np.concatenate([[0], np.cumsum(widths)])
    out_widths = (QW, KVW, A_WIDTH, 2 * bw, bw, 2 * cw, cw, N_BRANCH * d)

    tail0 = QW + KVW
    placed = [(0, 0), (3, Q_IQ), (5, Q_IW), (1, QW), (2, QW + KV_V), (4, QW + KV_IK),
              (6, tail0)]
    pieces = [(int(cuts[j]), int(widths[j]), dst) for j, dst in placed[:-1]]
    pieces.append((int(cuts[6]), int(cuts[12] - cuts[6]), tail0))
    w1 = _wprep(w_in, tuple(pieces), sum(out_widths))

    x = jnp.concatenate([x_prompt.reshape(mp, d), x_sample.reshape(ms, d)], axis=0)
    rep = B_CHUNK // dec_seq
    eye = jnp.eye(rep, dtype=F32)

    kp, vp, ikp, cvp, ks, vs, iks, cvs, gvs = ([] for _ in range(9))
    for l in range(depth):
        pq, pkv, za, uv, zb, glu, zc, gates = _proj(x, norm_g[l][None, :], w1, l, out_widths)

        a_p = _attn_prompt(pq, pkv, za, batch, seq)
        a_s = _attn_sample(pq, pkv, za,
                           cache_k[l].reshape(dec_batch, past, A_KV_WIDTH),
                           cache_v[l].reshape(dec_batch, past, A_KV_WIDTH),
                           cache_idx_k[l], mp, dec_batch, dec_seq)

        ln_g, ln_b = gmlp_ln_g[l][None, :], gmlp_ln_b[l][None, :]
        (b_p,) = _gmlp(uv, zb, ln_g, ln_b, gmlp_ws[l], gmlp_bs[l][:, :, None], 0, mp, True, False)
        ws_s = jnp.einsum('ab,gij->gaibj', eye, gmlp_ws[l][:, :dec_seq, :dec_seq]).reshape(
            B_GROUPS, B_CHUNK, B_CHUNK)
        bs_s = jnp.tile(gmlp_bs[l][:, :dec_seq], (1, rep))[:, :, None]
        b_s, gv = _gmlp(uv, zb, ln_g, ln_b, ws_s, bs_s, mp, ms, False, True)

        cargs = (conv_w[l], conv_b[l][None, :], conv_ln_g[l][None, :], conv_ln_b[l][None, :])
        c_p, tail_p = _conv_prompt(glu, zc, *cargs, batch, seq)
        c_s, tail_s = _conv_sample(glu, zc, state_conv[l], *cargs, mp, dec_batch, dec_seq)

        p = jnp.concatenate([p_prompt[l].reshape(mp, -1), p_sample[l].reshape(ms, -1)], axis=0)
        x = _merge(jnp.concatenate([a_p, a_s], axis=0), jnp.concatenate([b_p, b_s], axis=0),
                   jnp.concatenate([c_p, c_s], axis=0), gates, x, p,
                   w_branch_a[l].astype(BF16), w_branch_b[l].astype(BF16), w_branch_c[l].astype(BF16),
                   w_out[l].astype(BF16), ple_norm_g[l][None, :], w_ple_gate[l].astype(BF16),
                   w_ple[l].astype(BF16), final_norm_g[None, :], l == depth - 1)

        kp.append(pkv[:mp, 0:A_KV_WIDTH].reshape(batch, seq, A_KV_HEADS, A_HEAD_DIM))
        vp.append(pkv[:mp, KV_V:KV_V + A_KV_WIDTH].reshape(batch, seq, A_KV_HEADS, A_HEAD_DIM))
        ikp.append(pkv[:mp, KV_IK:KV_IK + IDX_DIM].reshape(batch, seq, IDX_DIM))
        cvp.append(tail_p)
        ks.append(pkv[mp:, 0:A_KV_WIDTH].reshape(dec_batch, dec_seq, A_KV_HEADS, A_HEAD_DIM))
        vs.append(pkv[mp:, KV_V:KV_V + A_KV_WIDTH].reshape(dec_batch, dec_seq, A_KV_HEADS, A_HEAD_DIM))
        iks.append(pkv[mp:, KV_IK:KV_IK + IDX_DIM].reshape(dec_batch, dec_seq, IDX_DIM))
        cvs.append(tail_s)
        gvs.append(gv.reshape(dec_batch, dec_seq, bw))

    return (x[:mp].reshape(batch, seq, d), x[mp:].reshape(dec_batch, dec_seq, d),
            jnp.stack(kp), jnp.stack(vp), jnp.stack(ikp), jnp.stack(cvp),
            jnp.stack(ks), jnp.stack(vs), jnp.stack(iks), jnp.stack(cvs), jnp.stack(gvs))
```

```python
import functools

import jax
import jax.numpy as jnp
import numpy as np
from jax import lax
from jax.experimental import pallas as pl
from jax.experimental.pallas import tpu as pltpu

F32 = jnp.float32
BF16 = jnp.bfloat16

CHUNK = 64
EPS = 1e-6
A_HEAD_DIM = 64
A_HEADS = 8
A_KV_HEADS = 2
A_REP = A_HEADS // A_KV_HEADS
A_WIDTH = A_HEADS * A_HEAD_DIM
A_KV_WIDTH = A_KV_HEADS * A_HEAD_DIM
IDX_HEADS = 4
IDX_DIM = 64
IDX_W_SCALE = (IDX_HEADS * IDX_DIM) ** -0.5
TOPK_MAX = 256
B_GROUPS = 4
B_CHUNK = 128
C_CONV = 31
N_BRANCH = 3

QW = 896
KVW = 384
Q_IQ = A_WIDTH
Q_IW = A_WIDTH + IDX_HEADS * IDX_DIM
KV_V = A_KV_WIDTH
KV_IK = 2 * A_KV_WIDTH

VMEM_LIMIT = 56 * 1024 * 1024
ROW_TILE = 256
KEY_TILE = 256
SEARCH_STEPS = 4
SEARCH_ROUNDS = 96
TIE_CHECK_FROM = 3
NEG_INF = float("-inf")


def _cparams(sem):
    return pltpu.CompilerParams(dimension_semantics=sem, vmem_limit_bytes=VMEM_LIMIT)


def _rms(xf, g):
    return xf * lax.rsqrt(jnp.mean(jnp.square(xf), axis=-1, keepdims=True) + EPS) * g


def _ln(xf, g, b):
    xc = xf - jnp.mean(xf, axis=-1, keepdims=True)
    var = jnp.mean(jnp.square(xc), axis=-1, keepdims=True)
    return xc * lax.rsqrt(var + EPS) * g + b


def _silu(x):
    return x * jax.nn.sigmoid(x)


def _dot(a, b):
    return jnp.dot(a, b, preferred_element_type=F32)


def _dot_nt(a, b):
    return lax.dot_general(a, b, (((1,), (1,)), ((), ())), preferred_element_type=F32)


def _wprep_kernel(w_ref, o_ref, *, pieces):
    rows, n_out = o_ref.shape
    covered = 0
    for src, wd, dst in pieces:
        if dst > covered:
            o_ref[:, covered:dst] = jnp.zeros((rows, dst - covered), BF16)
        for c in range(0, wd, 1024):
            cw = min(1024, wd - c)
            o_ref[:, dst + c:dst + c + cw] = w_ref[:, src + c:src + c + cw].astype(BF16)
        covered = dst + wd
    if covered < n_out:
        o_ref[:, covered:] = jnp.zeros((rows, n_out - covered), BF16)


def _wprep(w, pieces, n_out):
    depth, d, n = w.shape
    return pl.pallas_call(
        functools.partial(_wprep_kernel, pieces=pieces),
        grid=(depth, d // ROW_TILE),
        in_specs=[pl.BlockSpec((None, ROW_TILE, n), lambda l, i: (l, i, 0))],
        out_specs=pl.BlockSpec((None, ROW_TILE, n_out), lambda l, i: (l, i, 0)),
        out_shape=jax.ShapeDtypeStruct((depth, d, n_out), BF16),
        compiler_params=_cparams(("parallel", "parallel")),
        name="wprep",
    )(w)


def _split_specs(t, width, n_first, second_off):
    first = pl.BlockSpec((t, width), lambda i: (jnp.minimum(i, n_first - 1), 0))
    second = pl.BlockSpec((t, width), lambda i: (jnp.maximum(i - n_first, 0) + second_off, 0))
    return [first, second]


def _pick(n_first, first_ref, second_ref):
    return jnp.where(pl.program_id(0) < n_first, first_ref[...], second_ref[...])


def _proj_kernel(x1_ref, x2_ref, g_ref, w_ref, *out_refs, widths, n_first):
    h = _rms(_pick(n_first, x1_ref, x2_ref), g_ref[...]).astype(BF16)
    off = 0
    for o_ref, wd in zip(out_refs, widths):
        for c in range(0, wd, 512):
            cw = min(512, wd - c)
            o_ref[:, c:c + cw] = _dot(h, w_ref[:, off + c:off + c + cw])
        off += wd


def _proj(x_pair, m, g, w, layer, widths):
    x1, x2, n_first, second_off = x_pair
    d = x1.shape[1]
    n = w.shape[2]
    return pl.pallas_call(
        functools.partial(_proj_kernel, widths=widths, n_first=n_first),
        grid=(m // ROW_TILE,),
        in_specs=_split_specs(ROW_TILE, d, n_first, second_off) + [
            pl.BlockSpec((1, d), lambda i: (0, 0)),
            pl.BlockSpec((None, d, n), lambda i: (layer, 0, 0), pipeline_mode=pl.Buffered(1)),
        ],
        out_specs=[pl.BlockSpec((ROW_TILE, wd), lambda i: (i, 0)) for wd in widths],
        out_shape=[jax.ShapeDtypeStruct((m, wd), F32) for wd in widths],
        compiler_params=_cparams(("parallel",)),
        name="proj",
    )(x1, x2, g, w)


LANES = 128
TQ = 128


def _rows8(x):
    return [x[8 * j:8 * j + 8] for j in range(x.shape[0] // 8)]


def _tree(parts, op):
    while len(parts) > 1:
        nxt = [op(parts[2 * j], parts[2 * j + 1]) for j in range(len(parts) // 2)]
        if len(parts) % 2:
            nxt.append(parts[-1])
        parts = nxt
    return parts[0]


def _all8(x, op):
    for shift in (4, 2, 1):
        x = op(x, pltpu.roll(x, shift, 0))
    return x


def _fold(x, op):
    return _all8(_tree(_rows8(x), op), op)


def _per_rows(op, x, v8):
    return jnp.concatenate([op(part, v8) for part in _rows8(x)], axis=0)


def _head_slabs(x, n, dst):
    t = x.shape[0]
    keep = (lax.broadcasted_iota(jnp.int32, (t, LANES), 1) // A_HEAD_DIM) == dst
    parts = []
    for h in range(n):
        slab = x[:, (h // 2) * LANES:(h // 2 + 1) * LANES]
        if h % 2 != dst:
            slab = pltpu.roll(slab, A_HEAD_DIM, 1)
        parts.append(jnp.where(keep, slab, 0.0))
    return parts


def _fill_keys(kt, k, v, ik, kb_ref, vt_ref, ikb_ref):
    rows = pl.ds(kt * KEY_TILE, KEY_TILE)
    kb_ref[rows, :] = k.astype(BF16)
    ikb_ref[rows, :] = ik.astype(BF16)
    vt_ref[kt] = jnp.concatenate(
        [v[c:c + LANES, :].T for c in range(0, KEY_TILE, LANES)], axis=1).astype(BF16)


def _attend(qrow, za, kb_ref, vt_ref, ikb_ref, score_ref, lg_ref, p_ref, acc_ref, *, nt, q_pos0, n_keys, topk):
    tk = KEY_TILE
    iq_h = [s.astype(BF16) for s in _head_slabs(qrow[:, Q_IQ:Q_IW], IDX_HEADS, 0)]
    iw_t = qrow[:, Q_IW:Q_IW + LANES].T * IDX_W_SCALE
    iw_rows = [iw_t[h:h + 1, :] for h in range(IDX_HEADS)]

    k_row = lax.broadcasted_iota(jnp.int32, (tk, TQ), 0)
    q_pos = q_pos0 + lax.broadcasted_iota(jnp.int32, (tk, TQ), 1)
    q_chunk = q_pos // CHUNK

    def visible(kt):
        k_pos = kt * tk + k_row
        return ((k_pos // CHUNK) <= q_chunk) & (k_pos < n_keys), k_pos

    def score_body(kt, carry):
        mn, mx = carry
        ik_t = ikb_ref[pl.ds(pl.multiple_of(kt * tk, tk), tk), :]
        s = jnp.zeros((tk, TQ), F32)
        for h in range(IDX_HEADS):
            s = s + jnp.maximum(_dot_nt(ik_t, iq_h[h]), 0.0) * iw_rows[h]
        vis, _ = visible(kt)
        score_ref[kt] = jnp.where(vis, s, NEG_INF)
        mn = jnp.minimum(mn, _tree(_rows8(jnp.where(vis, s, jnp.inf)), jnp.minimum))
        mx = jnp.maximum(mx, _tree(_rows8(jnp.where(vis, s, NEG_INF)), jnp.maximum))
        return mn, mx

    mn, mx = lax.fori_loop(0, nt, score_body,
                           (jnp.full((8, TQ), jnp.inf, F32), jnp.full((8, TQ), NEG_INF, F32)))
    mx = _all8(mx, jnp.maximum)

    q_pos8 = q_pos0 + lax.broadcasted_iota(jnp.int32, (8, TQ), 1)
    n_vis = jnp.minimum((q_pos8 // CHUNK + 1) * CHUNK, n_keys)
    k_eff = jnp.minimum(n_vis, topk).astype(F32)

    def count_ge(thr):
        def body(kt, acc):
            return acc + _tree([jnp.where(part >= thr, 1.0, 0.0) for part in _rows8(score_ref[kt])], jnp.add)
        return _all8(lax.fori_loop(0, nt, body, jnp.zeros((8, TQ), F32)), jnp.add)

    def band_open(lo, hi, still):
        def body(kt, carry):
            bmn, bmx = carry
            lows, highs = [], []
            for part in _rows8(score_ref[kt]):
                inb = (part >= lo) & (part < hi)
                lows.append(jnp.where(inb, part, jnp.inf))
                highs.append(jnp.where(inb, part, NEG_INF))
            return (jnp.minimum(bmn, _tree(lows, jnp.minimum)), jnp.maximum(bmx, _tree(highs, jnp.maximum)))
        bmn, bmx = lax.fori_loop(0, nt, body,
                                 (jnp.full((8, TQ), jnp.inf, F32), jnp.full((8, TQ), NEG_INF, F32)))
        distinct = _all8(bmn, jnp.minimum) < _all8(bmx, jnp.maximum)
        return jnp.max(jnp.where(still & distinct, 1.0, 0.0)).astype(F32)

    def search_round(state):
        it, _, lo, hi, c_lo, c_hi = state
        for _ in range(SEARCH_STEPS):
            mid = lo + (hi - lo) * 0.5
            inside = (mid > lo) & (mid < hi)
            c = count_ge(mid)
            up = inside & (c >= k_eff)
            dn = inside & (c < k_eff)
            lo = jnp.where(up, mid, lo)
            c_lo = jnp.where(up, c, c_lo)
            hi = jnp.where(dn, mid, hi)
            c_hi = jnp.where(dn, c, c_hi)
        still = c_lo != k_eff
        pending = jnp.max(jnp.where(still, 1.0, 0.0)).astype(F32)
        pending = lax.cond((pending > 0.0) & (it >= TIE_CHECK_FROM), band_open, lambda *_: pending, lo, hi, still)
        return it + 1, pending, lo, hi, c_lo, c_hi

    lo0 = _all8(mn, jnp.minimum)
    hi0 = mx + jnp.maximum(jnp.abs(mx), 1e-30) * (2.0 ** -10)
    pending0 = jnp.max(jnp.where(n_vis.astype(F32) != k_eff, 1.0, 0.0)).astype(F32)
    init = (jnp.int32(0), pending0, lo0, hi0, n_vis.astype(F32), jnp.zeros((8, TQ), F32))
    _, _, lo, hi, _, c_hi = lax.while_loop(
        lambda st: (st[0] < SEARCH_ROUNDS) & (st[1] > 0.0), search_round, init)
    lo_r, hi_r = lo[0:1], hi[0:1]
    need_r = (k_eff - c_hi)[0:1]

    tril = (lax.broadcasted_iota(jnp.int32, (tk, tk), 1)
            <= lax.broadcasted_iota(jnp.int32, (tk, tk), 0)).astype(BF16)

    def select_body(kt, carry):
        s = score_ref[kt]
        inb = (s >= lo_r) & (s < hi_r)
        band = jnp.where(inb, 1.0, 0.0)
        rank = _dot(tril, band.astype(BF16)) + carry
        sel = (s >= hi_r) | (inb & (rank <= need_r))
        _, k_pos = visible(kt)
        dist = jnp.abs(q_pos - k_pos).astype(F32)
        score_ref[kt] = jnp.where(sel, -dist, NEG_INF)
        return carry + jnp.sum(band, axis=0, keepdims=True)

    lax.fori_loop(0, nt, select_body, jnp.zeros((1, TQ), F32))

    slopes = [2.0 ** (-8.0 * (h + 1) / A_HEADS) for h in range(A_HEADS)]
    gw = A_REP * A_HEAD_DIM
    q_gs = []
    for g in range(A_KV_HEADS):
        q_g = jnp.concatenate(_head_slabs(qrow[:, g * gw:(g + 1) * gw], A_REP, g), axis=0)
        q_gs.append((q_g * (A_HEAD_DIM ** -0.5)).astype(BF16))
    acc_ref[...] = jnp.zeros(acc_ref.shape, F32)

    def attn_body(kt, carry):
        ms, ls = carry
        k_t = kb_ref[pl.ds(pl.multiple_of(kt * tk, tk), tk), :]
        v_t = vt_ref[kt]
        nd = score_ref[kt]
        for g in range(A_KV_HEADS):
            lg_ref[g] = _dot_nt(k_t, q_gs[g])
        ms_new, ls_new = [], []
        for g in range(A_KV_HEADS):
            corrs = []
            for r in range(A_REP):
                h = g * A_REP + r
                lgr = lg_ref[g, :, r * TQ:(r + 1) * TQ] + slopes[h] * nd
                m_new = jnp.maximum(ms[h], _fold(lgr, jnp.maximum))
                m_safe = jnp.where(m_new == NEG_INF, 0.0, m_new)
                p = _per_rows(lambda a, m: jnp.exp(a - m), lgr, m_safe)
                corr = jnp.exp(ms[h] - m_safe)
                ls_new.append(ls[h] * corr + _fold(p, jnp.add))
                ms_new.append(m_new)
                corrs.append(corr)
                p_ref[g, :, r * TQ:(r + 1) * TQ] = p.astype(BF16)
            pv = _dot(v_t[g * A_HEAD_DIM:(g + 1) * A_HEAD_DIM, :], p_ref[g])
            acc_ref[g] = _per_rows(jnp.multiply, acc_ref[g], jnp.concatenate(corrs, axis=1)) + pv
        return tuple(ms_new), tuple(ls_new)

    init = (tuple(jnp.full((8, TQ), NEG_INF, F32) for _ in range(A_HEADS)),
            tuple(jnp.zeros((8, TQ), F32) for _ in range(A_HEADS)))
    _, ls = lax.fori_loop(0, nt, attn_body, init)
    pieces = []
    for g in range(A_KV_HEADS):
        o_t = _per_rows(jnp.divide, acc_ref[g], jnp.concatenate(ls[g * A_REP:(g + 1) * A_REP], axis=1))
        for r in range(0, A_REP, 2):
            pair = jnp.concatenate([o_t[:, r * TQ:(r + 1) * TQ], o_t[:, (r + 1) * TQ:(r + 2) * TQ]], axis=0)
            pieces.append(pair.T)
    o_a = jnp.concatenate(pieces, axis=-1)
    return o_a * _silu(za)


def _attn_scratch(n_tiles):
    return [
        pltpu.VMEM((n_tiles * KEY_TILE, LANES), BF16),
        pltpu.VMEM((n_tiles, LANES, KEY_TILE), BF16),
        pltpu.VMEM((n_tiles * KEY_TILE, LANES), BF16),
        pltpu.VMEM((n_tiles, KEY_TILE, TQ), F32),
        pltpu.VMEM((A_KV_HEADS, KEY_TILE, A_REP * TQ), F32),
        pltpu.VMEM((A_KV_HEADS, KEY_TILE, A_REP * TQ), BF16),
        pltpu.VMEM((A_KV_HEADS, A_HEAD_DIM, A_REP * TQ), F32),
    ]


def _attn_prompt_kernel(q_ref, kv_ref, za_ref, o_ref, kb_ref, vt_ref, ikb_ref, *work, seq, topk):
    i = pl.program_id(1)

    @pl.when(i == 0)
    def _():
        for kt in range(seq // KEY_TILE):
            rows = slice(kt * KEY_TILE, (kt + 1) * KEY_TILE)
            _fill_keys(kt, kv_ref[rows, 0:LANES], kv_ref[rows, KV_V:KV_V + LANES],
                       kv_ref[rows, KV_IK:KV_IK + LANES], kb_ref, vt_ref, ikb_ref)

    nt = (i * TQ + TQ + KEY_TILE - 1) // KEY_TILE
    out = _attend(q_ref[...], za_ref[...], kb_ref, vt_ref, ikb_ref, *work,
                  nt=nt, q_pos0=i * TQ, n_keys=seq, topk=topk)
    o_ref[...] = out.astype(BF16)


def _attn_prompt(pq, pkv, za, batch, seq):
    nq = seq // TQ
    topk = min(TOPK_MAX, seq // 4)
    return pl.pallas_call(
        functools.partial(_attn_prompt_kernel, seq=seq, topk=topk),
        grid=(batch, nq),
        in_specs=[
            pl.BlockSpec((TQ, QW), lambda b, i: (b * nq + i, 0)),
            pl.BlockSpec((seq, KVW), lambda b, i: (b, 0)),
            pl.BlockSpec((TQ, A_WIDTH), lambda b, i: (b * nq + i, 0)),
        ],
        out_specs=pl.BlockSpec((TQ, A_WIDTH), lambda b, i: (b * nq + i, 0)),
        out_shape=jax.ShapeDtypeStruct((batch * seq, A_WIDTH), BF16),
        scratch_shapes=_attn_scratch(seq // KEY_TILE),
        compiler_params=_cparams(("parallel", "arbitrary")),
        name="attn_prompt",
    )(pq, pkv, za)


def _attn_sample_kernel(q_ref, kv_ref, za_ref, ck_ref, cv_ref, cik_ref, o_ref,
                        kb_ref, vt_ref, ikb_ref, *work, t, past, topk, nt):
    tk = KEY_TILE
    tail = nt * tk - past
    new = kv_ref[...]
    zeros = lambda n, w: jnp.zeros((n, w), F32)
    for kt in range(past // tk):
        rows = slice(kt * tk, (kt + 1) * tk)
        ik = jnp.concatenate([cik_ref[rows, :], zeros(tk, LANES - IDX_DIM)], axis=1)
        _fill_keys(kt, ck_ref[rows, :], cv_ref[rows, :], ik, kb_ref, vt_ref, ikb_ref)
    pad_rows = lambda x: jnp.concatenate([x, zeros(tail - t, LANES)], axis=0)
    _fill_keys(past // tk, pad_rows(new[:, 0:LANES]), pad_rows(new[:, KV_V:KV_V + LANES]),
               pad_rows(new[:, KV_IK:KV_IK + LANES]), kb_ref, vt_ref, ikb_ref)
    qrow = jnp.concatenate([q_ref[...], zeros(TQ - t, QW)], axis=0)
    za = jnp.concatenate([za_ref[...], zeros(TQ - t, A_WIDTH)], axis=0)
    out = _attend(qrow, za, kb_ref, vt_ref, ikb_ref, *work,
                  nt=nt, q_pos0=past, n_keys=past + t, topk=topk)
    o_ref[...] = out[0:t].astype(BF16)


def _attn_sample(pq, pkv, za, ck, cv, cik, row0, dec_batch, t):
    past = ck.shape[1]
    assert past % KEY_TILE == 0 and t <= KEY_TILE
    topk = min(TOPK_MAX, (past + t) // 4)
    nt = past // KEY_TILE + 1
    blk0 = row0 // t
    return pl.pallas_call(
        functools.partial(_attn_sample_kernel, t=t, past=past, topk=topk, nt=nt),
        grid=(dec_batch,),
        in_specs=[
            pl.BlockSpec((t, QW), lambda b: (blk0 + b, 0)),
            pl.BlockSpec((t, KVW), lambda b: (blk0 + b, 0)),
            pl.BlockSpec((t, A_WIDTH), lambda b: (blk0 + b, 0)),
            pl.BlockSpec((None, past, A_KV_WIDTH), lambda b: (b, 0, 0)),
            pl.BlockSpec((None, past, A_KV_WIDTH), lambda b: (b, 0, 0)),
            pl.BlockSpec((None, past, IDX_DIM), lambda b: (b, 0, 0)),
        ],
        out_specs=pl.BlockSpec((t, A_WIDTH), lambda b: (b, 0)),
        out_shape=jax.ShapeDtypeStruct((dec_batch * t, A_WIDTH), BF16),
        scratch_shapes=_attn_scratch(nt),
        compiler_params=_cparams(("parallel",)),
        name="attn_sample",
    )(pq, pkv, za, ck, cv, cik)


def _gmlp_kernel(uv_ref, zb_ref, g_ref, b_ref, w_ref, bs_ref, o_ref, *v_out, width, mask_chunks):
    gw = width // B_GROUPS
    uv = uv_ref[...]
    act = 0.5 * uv * (1.0 + lax.erf(uv * np.float32(1.0 / np.sqrt(2.0))))
    u = act[:, :width]
    v = _ln(act[:, width:], g_ref[...], b_ref[...])
    if v_out:
        v_out[0][...] = v
    if mask_chunks:
        i = lax.broadcasted_iota(jnp.int32, (B_CHUNK, B_CHUNK), 0)
        j = lax.broadcasted_iota(jnp.int32, (B_CHUNK, B_CHUNK), 1)
        keep = (j // CHUNK) <= (i // CHUNK)
    vb = v.astype(BF16)
    zs = _silu(zb_ref[...])
    for g in range(B_GROUPS):
        w = w_ref[g]
        if mask_chunks:
            w = jnp.where(keep, w, 0.0)
        mixed = _dot(w.astype(BF16), vb[:, g * gw:(g + 1) * gw]) + bs_ref[g]
        o_ref[:, g * gw:(g + 1) * gw] = (u[:, g * gw:(g + 1) * gw] * mixed
                                         * zs[:, g * gw:(g + 1) * gw]).astype(BF16)


def _gmlp(uv, zb, ln_g, ln_b, w, bs, row0, rows, mask_chunks, want_v):
    width = zb.shape[1]
    blk0 = row0 // B_CHUNK
    out_specs = [pl.BlockSpec((B_CHUNK, width), lambda i: (i, 0))]
    out_shape = [jax.ShapeDtypeStruct((rows, width), BF16)]
    if want_v:
        out_specs.append(pl.BlockSpec((B_CHUNK, width), lambda i: (i, 0)))
        out_shape.append(jax.ShapeDtypeStruct((rows, width), F32))
    return pl.pallas_call(
        functools.partial(_gmlp_kernel, width=width, mask_chunks=mask_chunks),
        grid=(rows // B_CHUNK,),
        in_specs=[
            pl.BlockSpec((B_CHUNK, 2 * width), lambda i: (blk0 + i, 0)),
            pl.BlockSpec((B_CHUNK, width), lambda i: (blk0 + i, 0)),
            pl.BlockSpec((1, width), lambda i: (0, 0)),
            pl.BlockSpec((1, width), lambda i: (0, 0)),
            pl.BlockSpec((B_GROUPS, B_CHUNK, B_CHUNK), lambda i: (0, 0, 0)),
            pl.BlockSpec((B_GROUPS, B_CHUNK, 1), lambda i: (0, 0, 0)),
        ],
        out_specs=out_specs,
        out_shape=out_shape,
        compiler_params=_cparams(("parallel",)),
        name="gmlp_sample" if want_v else "gmlp_prompt",
    )(uv, zb, ln_g, ln_b, w, bs)


CONV_PAD = 32


CONV_ROWS = 32
SUBLANES = 8


def _conv_tile(glu_ref, zc_ref, o_ref, ext_ref, y_ref, w_ref, b_ref, g_ref, beta_ref, t, width):
    ext_ref[0, CONV_PAD:CONV_PAD + t, :] = glu_ref[:, :width] * jax.nn.sigmoid(glu_ref[:, width:])
    n = t + CONV_PAD - SUBLANES
    for s in range(1, SUBLANES):
        ext_ref[s, 0:n, :] = ext_ref[0, s:s + n, :]
    base = CONV_PAD - (C_CONV - 1)

    def rows_step(ci, carry):
        r0 = pl.multiple_of(ci * CONV_ROWS, CONV_ROWS)
        y = jnp.zeros((CONV_ROWS, width), F32)
        for k in range(C_CONV):
            s, q = (base + k) % SUBLANES, (base + k) // SUBLANES
            tap = w_ref[SUBLANES * k:SUBLANES * (k + 1), :]
            y = y + _per_rows(jnp.multiply, ext_ref[s, pl.ds(r0 + SUBLANES * q, CONV_ROWS), :], tap)
        y_ref[pl.ds(r0, CONV_ROWS), :] = y
        return carry

    lax.fori_loop(0, t // CONV_ROWS, rows_step, 0)
    y = y_ref[...] + b_ref[...]
    o_ref[...] = (_silu(_ln(y, g_ref[...], beta_ref[...])) * _silu(zc_ref[...])).astype(BF16)


def _conv_prompt_kernel(glu_ref, zc_ref, w_ref, b_ref, g_ref, beta_ref, o_ref, tail_ref, ext_ref, y_ref,
                        *, t, width):
    i = pl.program_id(1)

    @pl.when(i == 0)
    def _():
        ext_ref[0, 0:CONV_PAD, :] = jnp.zeros((CONV_PAD, width), F32)

    _conv_tile(glu_ref, zc_ref, o_ref, ext_ref, y_ref, w_ref, b_ref, g_ref, beta_ref, t, width)
    tail = ext_ref[0, t:t + CONV_PAD, :]
    ext_ref[0, 0:CONV_PAD, :] = tail

    @pl.when(i == pl.num_programs(1) - 1)
    def _():
        tail_ref[...] = tail[CONV_PAD - (C_CONV - 1):, :]


def _conv_prompt(glu, zc, w, b, ln_g, ln_b, batch, seq):
    t = ROW_TILE
    width = zc.shape[1]
    nb = seq // t
    return pl.pallas_call(
        functools.partial(_conv_prompt_kernel, t=t, width=width),
        grid=(batch, nb),
        in_specs=[
            pl.BlockSpec((t, 2 * width), lambda b_, i: (b_ * nb + i, 0)),
            pl.BlockSpec((t, width), lambda b_, i: (b_ * nb + i, 0)),
            pl.BlockSpec((SUBLANES * C_CONV, width), lambda b_, i: (0, 0)),
            pl.BlockSpec((1, width), lambda b_, i: (0, 0)),
            pl.BlockSpec((1, width), lambda b_, i: (0, 0)),
            pl.BlockSpec((1, width), lambda b_, i: (0, 0)),
        ],
        out_specs=[
            pl.BlockSpec((t, width), lambda b_, i: (b_ * nb + i, 0)),
            pl.BlockSpec((None, C_CONV - 1, width), lambda b_, i: (b_, 0, 0)),
        ],
        out_shape=[
            jax.ShapeDtypeStruct((batch * seq, width), BF16),
            jax.ShapeDtypeStruct((batch, C_CONV - 1, width), F32),
        ],
        scratch_shapes=[pltpu.VMEM((SUBLANES, CONV_PAD + t, width), F32), pltpu.VMEM((t, width), F32)],
        compiler_params=_cparams(("parallel", "arbitrary")),
        name="conv_prompt",
    )(glu, zc, w, b, ln_g, ln_b)


def _conv_sample_kernel(glu_ref, zc_ref, st_ref, w_ref, b_ref, g_ref, beta_ref, o_ref, tail_ref, ext_ref, y_ref,
                        *, t, width):
    base = CONV_PAD - (C_CONV - 1)
    ext_ref[0, 0:base, :] = jnp.zeros((base, width), F32)
    ext_ref[0, base:CONV_PAD, :] = st_ref[...]
    _conv_tile(glu_ref, zc_ref, o_ref, ext_ref, y_ref, w_ref, b_ref, g_ref, beta_ref, t, width)
    tail_ref[...] = ext_ref[0, t + base:t + CONV_PAD, :]


def _conv_sample(glu, zc, state, w, b, ln_g, ln_b, row0, dec_batch, t):
    width = zc.shape[1]
    blk0 = row0 // t
    return pl.pallas_call(
        functools.partial(_conv_sample_kernel, t=t, width=width),
        grid=(dec_batch,),
        in_specs=[
            pl.BlockSpec((t, 2 * width), lambda b_: (blk0 + b_, 0)),
            pl.BlockSpec((t, width), lambda b_: (blk0 + b_, 0)),
            pl.BlockSpec((None, C_CONV - 1, width), lambda b_: (b_, 0, 0)),
            pl.BlockSpec((SUBLANES * C_CONV, width), lambda b_: (0, 0)),
            pl.BlockSpec((1, width), lambda b_: (0, 0)),
            pl.BlockSpec((1, width), lambda b_: (0, 0)),
            pl.BlockSpec((1, width), lambda b_: (0, 0)),
        ],
        out_specs=[
            pl.BlockSpec((t, width), lambda b_: (b_, 0)),
            pl.BlockSpec((None, C_CONV - 1, width), lambda b_: (b_, 0, 0)),
        ],
        out_shape=[
            jax.ShapeDtypeStruct((dec_batch * t, width), BF16),
            jax.ShapeDtypeStruct((dec_batch, C_CONV - 1, width), F32),
        ],
        scratch_shapes=[pltpu.VMEM((SUBLANES, CONV_PAD + t, width), F32), pltpu.VMEM((t, width), F32)],
        compiler_params=_cparams(("parallel",)),
        name="conv_sample",
    )(glu, zc, state, w, b, ln_g, ln_b)


def _merge_kernel(a1, a2, b1, b2, c1, c2, x1, x2, p1, p2, gates_ref, wa_ref, wb_ref, wc_ref, wo_ref,
                  pg_ref, wpg_ref, wple_ref, fg_ref, *o_refs, d, final, n_first):
    pick = functools.partial(_pick, n_first)
    merged = (jax.nn.sigmoid(gates_ref[:, 0:d]) * _dot(pick(a1, a2), wa_ref[...])
              + jax.nn.sigmoid(gates_ref[:, d:2 * d]) * _dot(pick(b1, b2), wb_ref[...])
              + jax.nn.sigmoid(gates_ref[:, 2 * d:3 * d]) * _dot(pick(c1, c2), wc_ref[...]))
    x = pick(x1, x2) + _dot(merged.astype(BF16), wo_ref[...])
    gate = jax.nn.sigmoid(_dot(_rms(x, pg_ref[...]).astype(BF16), wpg_ref[...]))
    x = x + gate * _dot(pick(p1, p2).astype(BF16), wple_ref[...])
    if not final:
        o_refs[0][...] = x
        return
    y = _rms(x, fg_ref[...])
    on_first = pl.program_id(0) < n_first

    @pl.when(on_first)
    def _():
        o_refs[0][...] = y

    @pl.when(jnp.logical_not(on_first))
    def _():
        o_refs[1][...] = y


def _merge(a, b, c, x_pair, p, gates, wa, wb, wc, wo, pg, wpg, wple, fg, final):
    x1, x2, n_first, x_off = x_pair
    m, d = gates.shape[0], x1.shape[1]
    t = ROW_TILE
    pair = lambda arrs, off=0: _split_specs(t, arrs[0].shape[1], n_first, off)
    full = lambda arr: pl.BlockSpec(arr.shape, lambda i: (0, 0), pipeline_mode=pl.Buffered(1))
    if final:
        out_specs = _split_specs(t, d, n_first, 0)
        out_shape = [jax.ShapeDtypeStruct((n_first * t, d), F32), jax.ShapeDtypeStruct((m - n_first * t, d), F32)]
    else:
        out_specs = [pl.BlockSpec((t, d), lambda i: (i, 0))]
        out_shape = [jax.ShapeDtypeStruct((m, d), F32)]
    return pl.pallas_call(
        functools.partial(_merge_kernel, d=d, final=final, n_first=n_first),
        grid=(m // t,),
        in_specs=(pair(a) + pair(b) + pair(c) + pair((x1, x2), x_off) + pair(p)
                  + [pl.BlockSpec((t, 3 * d), lambda i: (i, 0))]
                  + [full(wa), full(wb), full(wc), full(wo), full(pg), full(wpg), full(wple), full(fg)]),
        out_specs=out_specs,
        out_shape=out_shape,
        compiler_params=_cparams(("arbitrary",)),
        name="merge",
    )(*a, *b, *c, x1, x2, *p, gates, wa, wb, wc, wo, pg, wpg, wple, fg)


def kernel(x_prompt, x_sample, cache_k, cache_v, cache_idx_k, state_conv, p_prompt, p_sample, norm_g, w_in, gmlp_ln_g, gmlp_ln_b, gmlp_ws, gmlp_bs, conv_w, conv_b, conv_ln_g, conv_ln_b, w_branch_a, w_branch_b, w_branch_c, w_out, ple_norm_g, w_ple_gate, w_ple, final_norm_g):
    batch, seq, d = x_prompt.shape
    dec_batch, dec_seq, _ = x_sample.shape
    depth = w_in.shape[0]
    past = cache_k.shape[2]
    bw = gmlp_ln_g.shape[1]
    cw = conv_b.shape[1]
    mp = batch * seq
    ms = dec_batch * dec_seq
    assert mp % ROW_TILE == 0 and ms % ROW_TILE == 0 and seq % ROW_TILE == 0
    assert dec_seq <= CHUNK and B_CHUNK % dec_seq == 0 and past % CHUNK == 0

    widths = (A_WIDTH, A_KV_WIDTH, A_KV_WIDTH, IDX_HEADS * IDX_DIM, IDX_DIM, IDX_HEADS, A_WIDTH,
              2 * bw, bw, 2 * cw, cw, N_BRANCH * d)
    cuts = np.concatenate([[0], np.cumsum(widths)])
    out_widths = (QW, KVW, A_WIDTH, 2 * bw, bw, 2 * cw, cw, N_BRANCH * d)

    tail0 = QW + KVW
    placed = [(0, 0), (3, Q_IQ), (5, Q_IW), (1, QW), (2, QW + KV_V), (4, QW + KV_IK),
              (6, tail0)]
    pieces = [(int(cuts[j]), int(widths[j]), dst) for j, dst in placed[:-1]]
    pieces.append((int(cuts[6]), int(cuts[12] - cuts[6]), tail0))
    w1 = _wprep(w_in, tuple(pieces), sum(out_widths))

    n_first = mp // ROW_TILE
    x_pair = (x_prompt.reshape(mp, d), x_sample.reshape(ms, d), n_first, 0)
    rep = B_CHUNK // dec_seq
    eye = jnp.eye(rep, dtype=F32)

    kp, vp, ikp, cvp, ks, vs, iks, cvs, gvs = ([] for _ in range(9))
    for l in range(depth):
        pq, pkv, za, uv, zb, glu, zc, gates = _proj(x_pair, mp + ms, norm_g[l][None, :], w1, l, out_widths)

        a_p = _attn_prompt(pq, pkv, za, batch, seq)
        a_s = _attn_sample(pq, pkv, za,
                           cache_k[l].reshape(dec_batch, past, A_KV_WIDTH),
                           cache_v[l].reshape(dec_batch, past, A_KV_WIDTH),
                           cache_idx_k[l], mp, dec_batch, dec_seq)

        ln_g, ln_b = gmlp_ln_g[l][None, :], gmlp_ln_b[l][None, :]
        (b_p,) = _gmlp(uv, zb, ln_g, ln_b, gmlp_ws[l], gmlp_bs[l][:, :, None], 0, mp, True, False)
        ws_s = jnp.einsum('ab,gij->gaibj', eye, gmlp_ws[l][:, :dec_seq, :dec_seq]).reshape(
            B_GROUPS, B_CHUNK, B_CHUNK)
        bs_s = jnp.tile(gmlp_bs[l][:, :dec_seq], (1, rep))[:, :, None]
        b_s, gv = _gmlp(uv, zb, ln_g, ln_b, ws_s, bs_s, mp, ms, False, True)

        cargs = (jnp.repeat(conv_w[l], SUBLANES, axis=0), conv_b[l][None, :],
                 conv_ln_g[l][None, :], conv_ln_b[l][None, :])
        c_p, tail_p = _conv_prompt(glu, zc, *cargs, batch, seq)
        c_s, tail_s = _conv_sample(glu, zc, state_conv[l], *cargs, mp, dec_batch, dec_seq)

        p = (p_prompt[l].reshape(mp, -1), p_sample[l].reshape(ms, -1))
        outs = _merge((a_p, a_s), (b_p, b_s), (c_p, c_s), x_pair, p, gates,
                      w_branch_a[l].astype(BF16), w_branch_b[l].astype(BF16), w_branch_c[l].astype(BF16),
                      w_out[l].astype(BF16), ple_norm_g[l][None, :], w_ple_gate[l].astype(BF16),
                      w_ple[l].astype(BF16), final_norm_g[None, :], l == depth - 1)
        x_pair = (outs[0], outs[0], n_first, n_first)

        kp.append(pkv[:mp, 0:A_KV_WIDTH].reshape(batch, seq, A_KV_HEADS, A_HEAD_DIM))
        vp.append(pkv[:mp, KV_V:KV_V + A_KV_WIDTH].reshape(batch, seq, A_KV_HEADS, A_HEAD_DIM))
        ikp.append(pkv[:mp, KV_IK:KV_IK + IDX_DIM].reshape(batch, seq, IDX_DIM))
        cvp.append(tail_p)
        ks.append(pkv[mp:, 0:A_KV_WIDTH].reshape(dec_batch, dec_seq, A_KV_HEADS, A_HEAD_DIM))
        vs.append(pkv[mp:, KV_V:KV_V + A_KV_WIDTH].reshape(dec_batch, dec_seq, A_KV_HEADS, A_HEAD_DIM))
        iks.append(pkv[mp:, KV_IK:KV_IK + IDX_DIM].reshape(dec_batch, dec_seq, IDX_DIM))
        cvs.append(tail_s)
        gvs.append(gv.reshape(dec_batch, dec_seq, bw))

    return (outs[0].reshape(batch, seq, d), outs[1].reshape(dec_batch, dec_seq, d),
            jnp.stack(kp), jnp.stack(vp), jnp.stack(ikp), jnp.stack(cvp),
            jnp.stack(ks), jnp.stack(vs), jnp.stack(iks), jnp.stack(cvs), jnp.stack(gvs))
```

```python
import functools

import jax
import jax.numpy as jnp
import numpy as np
from jax import lax
from jax.experimental import pallas as pl
from jax.experimental.pallas import tpu as pltpu

F32 = jnp.float32
BF16 = jnp.bfloat16

CHUNK = 64
EPS = 1e-6
A_HEAD_DIM = 64
A_HEADS = 8
A_KV_HEADS = 2
A_REP = A_HEADS // A_KV_HEADS
A_WIDTH = A_HEADS * A_HEAD_DIM
A_KV_WIDTH = A_KV_HEADS * A_HEAD_DIM
IDX_HEADS = 4
IDX_DIM = 64
IDX_W_SCALE = (IDX_HEADS * IDX_DIM) ** -0.5
TOPK_MAX = 256
B_GROUPS = 4
B_CHUNK = 128
C_CONV = 31
N_BRANCH = 3

QW = 896
KVW = 384
Q_IQ = A_WIDTH
Q_IW = A_WIDTH + IDX_HEADS * IDX_DIM
KV_V = A_KV_WIDTH
KV_IK = 2 * A_KV_WIDTH

VMEM_LIMIT = 56 * 1024 * 1024
ROW_TILE = 256
KEY_TILE = 256
SEARCH_STEPS = 4
SEARCH_ROUNDS = 96
TIE_CHECK_FROM = 3
NEG_INF = float("-inf")


def _cparams(sem):
    return pltpu.CompilerParams(dimension_semantics=sem, vmem_limit_bytes=VMEM_LIMIT)


def _rms(xf, g):
    return xf * lax.rsqrt(jnp.mean(jnp.square(xf), axis=-1, keepdims=True) + EPS) * g


def _ln(xf, g, b):
    xc = xf - jnp.mean(xf, axis=-1, keepdims=True)
    var = jnp.mean(jnp.square(xc), axis=-1, keepdims=True)
    return xc * lax.rsqrt(var + EPS) * g + b


def _silu(x):
    return x * jax.nn.sigmoid(x)


def _dot(a, b):
    return jnp.dot(a, b, preferred_element_type=F32)


def _dot_nt(a, b):
    return lax.dot_general(a, b, (((1,), (1,)), ((), ())), preferred_element_type=F32)


def _wprep_kernel(w_ref, o_ref, *, pieces):
    rows, n_out = o_ref.shape
    covered = 0
    for src, wd, dst in pieces:
        if dst > covered:
            o_ref[:, covered:dst] = jnp.zeros((rows, dst - covered), BF16)
        for c in range(0, wd, 1024):
            cw = min(1024, wd - c)
            o_ref[:, dst + c:dst + c + cw] = w_ref[:, src + c:src + c + cw].astype(BF16)
        covered = dst + wd
    if covered < n_out:
        o_ref[:, covered:] = jnp.zeros((rows, n_out - covered), BF16)


def _wprep(w, pieces, n_out):
    depth, d, n = w.shape
    return pl.pallas_call(
        functools.partial(_wprep_kernel, pieces=pieces),
        grid=(depth, d // ROW_TILE),
        in_specs=[pl.BlockSpec((None, ROW_TILE, n), lambda l, i: (l, i, 0))],
        out_specs=pl.BlockSpec((None, ROW_TILE, n_out), lambda l, i: (l, i, 0)),
        out_shape=jax.ShapeDtypeStruct((depth, d, n_out), BF16),
        compiler_params=_cparams(("parallel", "parallel")),
        name="wprep",
    )(w)


def _split_specs(t, width, n_first, second_off):
    first = pl.BlockSpec((t, width), lambda i: (jnp.minimum(i, n_first - 1), 0))
    second = pl.BlockSpec((t, width), lambda i: (jnp.maximum(i - n_first, 0) + second_off, 0))
    return [first, second]


def _pick(n_first, first_ref, second_ref):
    return jnp.where(pl.program_id(0) < n_first, first_ref[...], second_ref[...])


def _proj_kernel(x1_ref, x2_ref, g_ref, w_ref, *out_refs, widths, n_first):
    h = _rms(_pick(n_first, x1_ref, x2_ref), g_ref[...]).astype(BF16)
    off = 0
    for o_ref, wd in zip(out_refs, widths):
        for c in range(0, wd, 512):
            cw = min(512, wd - c)
            o_ref[:, c:c + cw] = _dot(h, w_ref[:, off + c:off + c + cw])
        off += wd


def _proj(x_pair, m, g, w, layer, widths):
    x1, x2, n_first, second_off = x_pair
    d = x1.shape[1]
    n = w.shape[2]
    return pl.pallas_call(
        functools.partial(_proj_kernel, widths=widths, n_first=n_first),
        grid=(m // ROW_TILE,),
        in_specs=_split_specs(ROW_TILE, d, n_first, second_off) + [
            pl.BlockSpec((1, d), lambda i: (0, 0)),
            pl.BlockSpec((None, d, n), lambda i: (layer, 0, 0), pipeline_mode=pl.Buffered(1)),
        ],
        out_specs=[pl.BlockSpec((ROW_TILE, wd), lambda i: (i, 0)) for wd in widths],
        out_shape=[jax.ShapeDtypeStruct((m, wd), F32) for wd in widths],
        compiler_params=_cparams(("parallel",)),
        name="proj",
    )(x1, x2, g, w)


LANES = 128
TQ_PROMPT = 256
TQ_SAMPLE = 128


def _rows8(x):
    return [x[8 * j:8 * j + 8] for j in range(x.shape[0] // 8)]


def _tree(parts, op):
    while len(parts) > 1:
        nxt = [op(parts[2 * j], parts[2 * j + 1]) for j in range(len(parts) // 2)]
        if len(parts) % 2:
            nxt.append(parts[-1])
        parts = nxt
    return parts[0]


def _all8(x, op):
    for shift in (4, 2, 1):
        x = op(x, pltpu.roll(x, shift, 0))
    return x


def _fold(x, op):
    return _all8(_tree(_rows8(x), op), op)


def _per_rows(op, x, v8):
    return jnp.concatenate([op(part, v8) for part in _rows8(x)], axis=0)


def _head_slabs(x, n, dst):
    t = x.shape[0]
    keep = (lax.broadcasted_iota(jnp.int32, (t, LANES), 1) // A_HEAD_DIM) == dst
    parts = []
    for h in range(n):
        slab = x[:, (h // 2) * LANES:(h // 2 + 1) * LANES]
        if h % 2 != dst:
            slab = pltpu.roll(slab, A_HEAD_DIM, 1)
        parts.append(jnp.where(keep, slab, 0.0))
    return parts


def _fill_keys(kt, k, v, ik, kb_ref, vt_ref, ikb_ref):
    rows = pl.ds(kt * KEY_TILE, KEY_TILE)
    kb_ref[rows, :] = k.astype(BF16)
    ikb_ref[rows, :] = ik.astype(BF16)
    vt_ref[kt] = jnp.concatenate(
        [v[c:c + LANES, :].T for c in range(0, KEY_TILE, LANES)], axis=1).astype(BF16)


def _attend(qrow, za, kb_ref, vt_ref, ikb_ref, score_ref, lg_ref, p_ref, acc_ref, *, nt, q_pos0, n_keys, topk):
    tk = KEY_TILE
    TQ = qrow.shape[0]
    iq_h = [s.astype(BF16) for s in _head_slabs(qrow[:, Q_IQ:Q_IW], IDX_HEADS, 0)]
    iw_t = qrow[:, Q_IW:Q_IW + LANES].T * IDX_W_SCALE
    iw_rows = [iw_t[h:h + 1, :] for h in range(IDX_HEADS)]

    k_row = lax.broadcasted_iota(jnp.int32, (tk, TQ), 0)
    q_pos = q_pos0 + lax.broadcasted_iota(jnp.int32, (tk, TQ), 1)
    q_chunk = q_pos // CHUNK

    def visible(kt):
        k_pos = kt * tk + k_row
        return ((k_pos // CHUNK) <= q_chunk) & (k_pos < n_keys), k_pos

    def score_body(kt, carry):
        mn, mx = carry
        ik_t = ikb_ref[pl.ds(pl.multiple_of(kt * tk, tk), tk), :]
        s = jnp.zeros((tk, TQ), F32)
        for h in range(IDX_HEADS):
            s = s + jnp.maximum(_dot_nt(ik_t, iq_h[h]), 0.0) * iw_rows[h]
        vis, _ = visible(kt)
        score_ref[kt] = jnp.where(vis, s, NEG_INF)
        mn = jnp.minimum(mn, _tree(_rows8(jnp.where(vis, s, jnp.inf)), jnp.minimum))
        mx = jnp.maximum(mx, _tree(_rows8(jnp.where(vis, s, NEG_INF)), jnp.maximum))
        return mn, mx

    mn, mx = lax.fori_loop(0, nt, score_body,
                           (jnp.full((8, TQ), jnp.inf, F32), jnp.full((8, TQ), NEG_INF, F32)))
    mx = _all8(mx, jnp.maximum)

    q_pos8 = q_pos0 + lax.broadcasted_iota(jnp.int32, (8, TQ), 1)
    n_vis = jnp.minimum((q_pos8 // CHUNK + 1) * CHUNK, n_keys)
    k_eff = jnp.minimum(n_vis, topk).astype(F32)

    def count_ge(thr):
        def body(kt, acc):
            return acc + _tree([jnp.where(part >= thr, 1.0, 0.0) for part in _rows8(score_ref[kt])], jnp.add)
        return _all8(lax.fori_loop(0, nt, body, jnp.zeros((8, TQ), F32)), jnp.add)

    def band_open(lo, hi, still):
        def body(kt, carry):
            bmn, bmx = carry
            lows, highs = [], []
            for part in _rows8(score_ref[kt]):
                inb = (part >= lo) & (part < hi)
                lows.append(jnp.where(inb, part, jnp.inf))
                highs.append(jnp.where(inb, part, NEG_INF))
            return (jnp.minimum(bmn, _tree(lows, jnp.minimum)), jnp.maximum(bmx, _tree(highs, jnp.maximum)))
        bmn, bmx = lax.fori_loop(0, nt, body,
                                 (jnp.full((8, TQ), jnp.inf, F32), jnp.full((8, TQ), NEG_INF, F32)))
        distinct = _all8(bmn, jnp.minimum) < _all8(bmx, jnp.maximum)
        return jnp.max(jnp.where(still & distinct, 1.0, 0.0)).astype(F32)

    def search_round(state):
        it, _, lo, hi, c_lo, c_hi = state
        for _ in range(SEARCH_STEPS):
            mid = lo + (hi - lo) * 0.5
            inside = (mid > lo) & (mid < hi)
            c = count_ge(mid)
            up = inside & (c >= k_eff)
            dn = inside & (c < k_eff)
            lo = jnp.where(up, mid, lo)
            c_lo = jnp.where(up, c, c_lo)
            hi = jnp.where(dn, mid, hi)
            c_hi = jnp.where(dn, c, c_hi)
        still = c_lo != k_eff
        pending = jnp.max(jnp.where(still, 1.0, 0.0)).astype(F32)
        pending = lax.cond((pending > 0.0) & (it >= TIE_CHECK_FROM), band_open, lambda *_: pending, lo, hi, still)
        return it + 1, pending, lo, hi, c_lo, c_hi

    lo0 = _all8(mn, jnp.minimum)
    hi0 = mx + jnp.maximum(jnp.abs(mx), 1e-30) * (2.0 ** -10)
    pending0 = jnp.max(jnp.where(n_vis.astype(F32) != k_eff, 1.0, 0.0)).astype(F32)
    init = (jnp.int32(0), pending0, lo0, hi0, n_vis.astype(F32), jnp.zeros((8, TQ), F32))
    _, _, lo, hi, _, c_hi = lax.while_loop(
        lambda st: (st[0] < SEARCH_ROUNDS) & (st[1] > 0.0), search_round, init)
    lo_r, hi_r = lo[0:1], hi[0:1]
    need_r = (k_eff - c_hi)[0:1]

    tril = (lax.broadcasted_iota(jnp.int32, (tk, tk), 1)
            <= lax.broadcasted_iota(jnp.int32, (tk, tk), 0)).astype(BF16)

    def select_body(kt, carry):
        s = score_ref[kt]
        inb = (s >= lo_r) & (s < hi_r)
        band = jnp.where(inb, 1.0, 0.0)
        rank = _dot(tril, band.astype(BF16)) + carry
        sel = (s >= hi_r) | (inb & (rank <= need_r))
        _, k_pos = visible(kt)
        dist = jnp.abs(q_pos - k_pos).astype(F32)
        score_ref[kt] = jnp.where(sel, -dist, NEG_INF)
        return carry + jnp.sum(band, axis=0, keepdims=True)

    lax.fori_loop(0, nt, select_body, jnp.zeros((1, TQ), F32))

    slopes = [2.0 ** (-8.0 * (h + 1) / A_HEADS) for h in range(A_HEADS)]
    gw = A_REP * A_HEAD_DIM
    q_gs = []
    for g in range(A_KV_HEADS):
        q_g = jnp.concatenate(_head_slabs(qrow[:, g * gw:(g + 1) * gw], A_REP, g), axis=0)
        q_gs.append((q_g * (A_HEAD_DIM ** -0.5)).astype(BF16))
    acc_ref[...] = jnp.zeros(acc_ref.shape, F32)

    def attn_body(kt, carry):
        ms, ls = carry
        k_t = kb_ref[pl.ds(pl.multiple_of(kt * tk, tk), tk), :]
        v_t = vt_ref[kt]
        nd = score_ref[kt]
        for g in range(A_KV_HEADS):
            lg_ref[g] = _dot_nt(k_t, q_gs[g])
        ms_new, ls_new = [], []
        for g in range(A_KV_HEADS):
            corrs = []
            for r in range(A_REP):
                h = g * A_REP + r
                lgr = lg_ref[g, :, r * TQ:(r + 1) * TQ] + slopes[h] * nd
                m_new = jnp.maximum(ms[h], _fold(lgr, jnp.maximum))
                m_safe = jnp.where(m_new == NEG_INF, 0.0, m_new)
                p = _per_rows(lambda a, m: jnp.exp(a - m), lgr, m_safe)
                corr = jnp.exp(ms[h] - m_safe)
                ls_new.append(ls[h] * corr + _fold(p, jnp.add))
                ms_new.append(m_new)
                corrs.append(corr)
                p_ref[g, :, r * TQ:(r + 1) * TQ] = p.astype(BF16)
            pv = _dot(v_t[g * A_HEAD_DIM:(g + 1) * A_HEAD_DIM, :], p_ref[g])
            acc_ref[g] = _per_rows(jnp.multiply, acc_ref[g], jnp.concatenate(corrs, axis=1)) + pv
        return tuple(ms_new), tuple(ls_new)

    init = (tuple(jnp.full((8, TQ), NEG_INF, F32) for _ in range(A_HEADS)),
            tuple(jnp.zeros((8, TQ), F32) for _ in range(A_HEADS)))
    _, ls = lax.fori_loop(0, nt, attn_body, init)
    pieces = []
    for g in range(A_KV_HEADS):
        o_t = _per_rows(jnp.divide, acc_ref[g], jnp.concatenate(ls[g * A_REP:(g + 1) * A_REP], axis=1))
        for r in range(0, A_REP, 2):
            pair = jnp.concatenate([o_t[:, r * TQ:(r + 1) * TQ], o_t[:, (r + 1) * TQ:(r + 2) * TQ]], axis=0)
            pieces.append(pair.T)
    o_a = jnp.concatenate(pieces, axis=-1)
    return o_a * _silu(za)


def _attn_scratch(n_tiles, TQ):
    return [
        pltpu.VMEM((n_tiles * KEY_TILE, LANES), BF16),
        pltpu.VMEM((n_tiles, LANES, KEY_TILE), BF16),
        pltpu.VMEM((n_tiles * KEY_TILE, LANES), BF16),
        pltpu.VMEM((n_tiles, KEY_TILE, TQ), F32),
        pltpu.VMEM((A_KV_HEADS, KEY_TILE, A_REP * TQ), F32),
        pltpu.VMEM((A_KV_HEADS, KEY_TILE, A_REP * TQ), BF16),
        pltpu.VMEM((A_KV_HEADS, A_HEAD_DIM, A_REP * TQ), F32),
    ]


def _attn_prompt_kernel(q_ref, kv_ref, za_ref, o_ref, kb_ref, vt_ref, ikb_ref, *work, seq, topk):
    i = pl.program_id(1)

    @pl.when(i == 0)
    def _():
        for kt in range(seq // KEY_TILE):
            rows = slice(kt * KEY_TILE, (kt + 1) * KEY_TILE)
            _fill_keys(kt, kv_ref[rows, 0:LANES], kv_ref[rows, KV_V:KV_V + LANES],
                       kv_ref[rows, KV_IK:KV_IK + LANES], kb_ref, vt_ref, ikb_ref)

    TQ = TQ_PROMPT
    nt = (i * TQ + TQ + KEY_TILE - 1) // KEY_TILE
    out = _attend(q_ref[...], za_ref[...], kb_ref, vt_ref, ikb_ref, *work,
                  nt=nt, q_pos0=i * TQ, n_keys=seq, topk=topk)
    o_ref[...] = out.astype(BF16)


def _attn_prompt(pq, pkv, za, batch, seq):
    TQ = TQ_PROMPT
    nq = seq // TQ
    topk = min(TOPK_MAX, seq // 4)
    return pl.pallas_call(
        functools.partial(_attn_prompt_kernel, seq=seq, topk=topk),
        grid=(batch, nq),
        in_specs=[
            pl.BlockSpec((TQ, QW), lambda b, i: (b * nq + i, 0)),
            pl.BlockSpec((seq, KVW), lambda b, i: (b, 0)),
            pl.BlockSpec((TQ, A_WIDTH), lambda b, i: (b * nq + i, 0)),
        ],
        out_specs=pl.BlockSpec((TQ, A_WIDTH), lambda b, i: (b * nq + i, 0)),
        out_shape=jax.ShapeDtypeStruct((batch * seq, A_WIDTH), BF16),
        scratch_shapes=_attn_scratch(seq // KEY_TILE, TQ),
        compiler_params=_cparams(("parallel", "arbitrary")),
        name="attn_prompt",
    )(pq, pkv, za)


def _attn_sample_kernel(q_ref, kv_ref, za_ref, ck_ref, cv_ref, cik_ref, o_ref,
                        kb_ref, vt_ref, ikb_ref, *work, t, past, topk, nt):
    tk = KEY_TILE
    tail = nt * tk - past
    new = kv_ref[...]
    zeros = lambda n, w: jnp.zeros((n, w), F32)
    for kt in range(past // tk):
        rows = slice(kt * tk, (kt + 1) * tk)
        ik = jnp.concatenate([cik_ref[rows, :], zeros(tk, LANES - IDX_DIM)], axis=1)
        _fill_keys(kt, ck_ref[rows, :], cv_ref[rows, :], ik, kb_ref, vt_ref, ikb_ref)
    pad_rows = lambda x: jnp.concatenate([x, zeros(tail - t, LANES)], axis=0)
    _fill_keys(past // tk, pad_rows(new[:, 0:LANES]), pad_rows(new[:, KV_V:KV_V + LANES]),
               pad_rows(new[:, KV_IK:KV_IK + LANES]), kb_ref, vt_ref, ikb_ref)
    TQ = TQ_SAMPLE
    qrow = jnp.concatenate([q_ref[...], zeros(TQ - t, QW)], axis=0)
    za = jnp.concatenate([za_ref[...], zeros(TQ - t, A_WIDTH)], axis=0)
    out = _attend(qrow, za, kb_ref, vt_ref, ikb_ref, *work,
                  nt=nt, q_pos0=past, n_keys=past + t, topk=topk)
    o_ref[...] = out[0:t].astype(BF16)


def _attn_sample(pq, pkv, za, ck, cv, cik, row0, dec_batch, t):
    past = ck.shape[1]
    assert past % KEY_TILE == 0 and t <= KEY_TILE
    topk = min(TOPK_MAX, (past + t) // 4)
    nt = past // KEY_TILE + 1
    blk0 = row0 // t
    return pl.pallas_call(
        functools.partial(_attn_sample_kernel, t=t, past=past, topk=topk, nt=nt),
        grid=(dec_batch,),
        in_specs=[
            pl.BlockSpec((t, QW), lambda b: (blk0 + b, 0)),
            pl.BlockSpec((t, KVW), lambda b: (blk0 + b, 0)),
            pl.BlockSpec((t, A_WIDTH), lambda b: (blk0 + b, 0)),
            pl.BlockSpec((None, past, A_KV_WIDTH), lambda b: (b, 0, 0)),
            pl.BlockSpec((None, past, A_KV_WIDTH), lambda b: (b, 0, 0)),
            pl.BlockSpec((None, past, IDX_DIM), lambda b: (b, 0, 0)),
        ],
        out_specs=pl.BlockSpec((t, A_WIDTH), lambda b: (b, 0)),
        out_shape=jax.ShapeDtypeStruct((dec_batch * t, A_WIDTH), BF16),
        scratch_shapes=_attn_scratch(nt, TQ_SAMPLE),
        compiler_params=_cparams(("parallel",)),
        name="attn_sample",
    )(pq, pkv, za, ck, cv, cik)


GMLP_ROWS = 512


def _gmlp_kernel(uv_ref, zb_ref, g_ref, b_ref, w_ref, bs_ref, o_ref, *v_out, width, mask_chunks):
    gw = width // B_GROUPS
    ws = []
    for g in range(B_GROUPS):
        w = w_ref[g]
        if mask_chunks:
            i = lax.broadcasted_iota(jnp.int32, (B_CHUNK, B_CHUNK), 0)
            j = lax.broadcasted_iota(jnp.int32, (B_CHUNK, B_CHUNK), 1)
            w = jnp.where((j // CHUNK) <= (i // CHUNK), w, 0.0)
        ws.append(w.astype(BF16))
    for c in range(uv_ref.shape[0] // B_CHUNK):
        rows = slice(c * B_CHUNK, (c + 1) * B_CHUNK)
        uv = uv_ref[rows, :]
        act = 0.5 * uv * (1.0 + lax.erf(uv * np.float32(1.0 / np.sqrt(2.0))))
        u = act[:, :width]
        v = _ln(act[:, width:], g_ref[...], b_ref[...])
        if v_out:
            v_out[0][rows, :] = v
        vb = v.astype(BF16)
        zs = _silu(zb_ref[rows, :])
        for g in range(B_GROUPS):
            cols = slice(g * gw, (g + 1) * gw)
            mixed = _dot(ws[g], vb[:, cols]) + bs_ref[g]
            o_ref[rows, cols] = (u[:, cols] * mixed * zs[:, cols]).astype(BF16)


def _gmlp(uv, zb, ln_g, ln_b, w, bs, row0, rows, mask_chunks, want_v):
    width = zb.shape[1]
    t = GMLP_ROWS
    assert row0 % t == 0 and rows % t == 0
    blk0 = row0 // t
    out_specs = [pl.BlockSpec((t, width), lambda i: (i, 0))]
    out_shape = [jax.ShapeDtypeStruct((rows, width), BF16)]
    if want_v:
        out_specs.append(pl.BlockSpec((t, width), lambda i: (i, 0)))
        out_shape.append(jax.ShapeDtypeStruct((rows, width), F32))
    return pl.pallas_call(
        functools.partial(_gmlp_kernel, width=width, mask_chunks=mask_chunks),
        grid=(rows // t,),
        in_specs=[
            pl.BlockSpec((t, 2 * width), lambda i: (blk0 + i, 0)),
            pl.BlockSpec((t, width), lambda i: (blk0 + i, 0)),
            pl.BlockSpec((1, width), lambda i: (0, 0)),
            pl.BlockSpec((1, width), lambda i: (0, 0)),
            pl.BlockSpec((B_GROUPS, B_CHUNK, B_CHUNK), lambda i: (0, 0, 0)),
            pl.BlockSpec((B_GROUPS, B_CHUNK, 1), lambda i: (0, 0, 0)),
        ],
        out_specs=out_specs,
        out_shape=out_shape,
        compiler_params=_cparams(("parallel",)),
        name="gmlp_sample" if want_v else "gmlp_prompt",
    )(uv, zb, ln_g, ln_b, w, bs)


CONV_PAD = 32


CONV_ROWS = 32
SUBLANES = 8


def _conv_tile(glu_ref, zc_ref, o_ref, ext_ref, y_ref, w_ref, b_ref, g_ref, beta_ref, t, width):
    ext_ref[0, CONV_PAD:CONV_PAD + t, :] = glu_ref[:, :width] * jax.nn.sigmoid(glu_ref[:, width:])
    n = t + CONV_PAD - SUBLANES
    for s in range(1, SUBLANES):
        ext_ref[s, 0:n, :] = ext_ref[0, s:s + n, :]
    base = CONV_PAD - (C_CONV - 1)

    def rows_step(ci, carry):
        r0 = pl.multiple_of(ci * CONV_ROWS, CONV_ROWS)
        y = jnp.zeros((CONV_ROWS, width), F32)
        for k in range(C_CONV):
            s, q = (base + k) % SUBLANES, (base + k) // SUBLANES
            tap = w_ref[SUBLANES * k:SUBLANES * (k + 1), :]
            y = y + _per_rows(jnp.multiply, ext_ref[s, pl.ds(r0 + SUBLANES * q, CONV_ROWS), :], tap)
        y_ref[pl.ds(r0, CONV_ROWS), :] = y
        return carry

    lax.fori_loop(0, t // CONV_ROWS, rows_step, 0)
    y = y_ref[...] + b_ref[...]
    o_ref[...] = (_silu(_ln(y, g_ref[...], beta_ref[...])) * _silu(zc_ref[...])).astype(BF16)


def _conv_prompt_kernel(glu_ref, zc_ref, w_ref, b_ref, g_ref, beta_ref, o_ref, tail_ref, ext_ref, y_ref,
                        *, t, width):
    i = pl.program_id(1)

    @pl.when(i == 0)
    def _():
        ext_ref[0, 0:CONV_PAD, :] = jnp.zeros((CONV_PAD, width), F32)

    _conv_tile(glu_ref, zc_ref, o_ref, ext_ref, y_ref, w_ref, b_ref, g_ref, beta_ref, t, width)
    tail = ext_ref[0, t:t + CONV_PAD, :]
    ext_ref[0, 0:CONV_PAD, :] = tail

    @pl.when(i == pl.num_programs(1) - 1)
    def _():
        tail_ref[...] = tail[CONV_PAD - (C_CONV - 1):, :]


def _conv_prompt(glu, zc, w, b, ln_g, ln_b, batch, seq):
    t = ROW_TILE
    width = zc.shape[1]
    nb = seq // t
    return pl.pallas_call(
        functools.partial(_conv_prompt_kernel, t=t, width=width),
        grid=(batch, nb),
        in_specs=[
            pl.BlockSpec((t, 2 * width), lambda b_, i: (b_ * nb + i, 0)),
            pl.BlockSpec((t, width), lambda b_, i: (b_ * nb + i, 0)),
            pl.BlockSpec((SUBLANES * C_CONV, width), lambda b_, i: (0, 0)),
            pl.BlockSpec((1, width), lambda b_, i: (0, 0)),
            pl.BlockSpec((1, width), lambda b_, i: (0, 0)),
            pl.BlockSpec((1, width), lambda b_, i: (0, 0)),
        ],
        out_specs=[
            pl.BlockSpec((t, width), lambda b_, i: (b_ * nb + i, 0)),
            pl.BlockSpec((None, C_CONV - 1, width), lambda b_, i: (b_, 0, 0)),
        ],
        out_shape=[
            jax.ShapeDtypeStruct((batch * seq, width), BF16),
            jax.ShapeDtypeStruct((batch, C_CONV - 1, width), F32),
        ],
        scratch_shapes=[pltpu.VMEM((SUBLANES, CONV_PAD + t, width), F32), pltpu.VMEM((t, width), F32)],
        compiler_params=_cparams(("parallel", "arbitrary")),
        name="conv_prompt",
    )(glu, zc, w, b, ln_g, ln_b)


def _conv_sample_kernel(glu_ref, zc_ref, st_ref, w_ref, b_ref, g_ref, beta_ref, o_ref, tail_ref, ext_ref, y_ref,
                        *, t, width):
    base = CONV_PAD - (C_CONV - 1)
    ext_ref[0, 0:base, :] = jnp.zeros((base, width), F32)
    ext_ref[0, base:CONV_PAD, :] = st_ref[...]
    _conv_tile(glu_ref, zc_ref, o_ref, ext_ref, y_ref, w_ref, b_ref, g_ref, beta_ref, t, width)
    tail_ref[...] = ext_ref[0, t + base:t + CONV_PAD, :]


def _conv_sample(glu, zc, state, w, b, ln_g, ln_b, row0, dec_batch, t):
    width = zc.shape[1]
    blk0 = row0 // t
    return pl.pallas_call(
        functools.partial(_conv_sample_kernel, t=t, width=width),
        grid=(dec_batch,),
        in_specs=[
            pl.BlockSpec((t, 2 * width), lambda b_: (blk0 + b_, 0)),
            pl.BlockSpec((t, width), lambda b_: (blk0 + b_, 0)),
            pl.BlockSpec((None, C_CONV - 1, width), lambda b_: (b_, 0, 0)),
            pl.BlockSpec((SUBLANES * C_CONV, width), lambda b_: (0, 0)),
            pl.BlockSpec((1, width), lambda b_: (0, 0)),
            pl.BlockSpec((1, width), lambda b_: (0, 0)),
            pl.BlockSpec((1, width), lambda b_: (0, 0)),
        ],
        out_specs=[
            pl.BlockSpec((t, width), lambda b_: (b_, 0)),
            pl.BlockSpec((None, C_CONV - 1, width), lambda b_: (b_, 0, 0)),
        ],
        out_shape=[
            jax.ShapeDtypeStruct((dec_batch * t, width), BF16),
            jax.ShapeDtypeStruct((dec_batch, C_CONV - 1, width), F32),
        ],
        scratch_shapes=[pltpu.VMEM((SUBLANES, CONV_PAD + t, width), F32), pltpu.VMEM((t, width), F32)],
        compiler_params=_cparams(("parallel",)),
        name="conv_sample",
    )(glu, zc, state, w, b, ln_g, ln_b)


def _merge_kernel(a1, a2, b1, b2, c1, c2, x1, x2, p1, p2, gates_ref, wa_ref, wb_ref, wc_ref, wo_ref,
                  pg_ref, wpg_ref, wple_ref, fg_ref, *o_refs, d, final, n_first):
    pick = functools.partial(_pick, n_first)
    merged = (jax.nn.sigmoid(gates_ref[:, 0:d]) * _dot(pick(a1, a2), wa_ref[...])
              + jax.nn.sigmoid(gates_ref[:, d:2 * d]) * _dot(pick(b1, b2), wb_ref[...])
              + jax.nn.sigmoid(gates_ref[:, 2 * d:3 * d]) * _dot(pick(c1, c2), wc_ref[...]))
    x = pick(x1, x2) + _dot(merged.astype(BF16), wo_ref[...])
    gate = jax.nn.sigmoid(_dot(_rms(x, pg_ref[...]).astype(BF16), wpg_ref[...]))
    x = x + gate * _dot(pick(p1, p2).astype(BF16), wple_ref[...])
    if not final:
        o_refs[0][...] = x
        return
    y = _rms(x, fg_ref[...])
    on_first = pl.program_id(0) < n_first

    @pl.when(on_first)
    def _():
        o_refs[0][...] = y

    @pl.when(jnp.logical_not(on_first))
    def _():
        o_refs[1][...] = y


def _merge(a, b, c, x_pair, p, gates, wa, wb, wc, wo, pg, wpg, wple, fg, final):
    x1, x2, n_first, x_off = x_pair
    m, d = gates.shape[0], x1.shape[1]
    t = ROW_TILE
    pair = lambda arrs, off=0: _split_specs(t, arrs[0].shape[1], n_first, off)
    full = lambda arr: pl.BlockSpec(arr.shape, lambda i: (0, 0), pipeline_mode=pl.Buffered(1))
    if final:
        out_specs = _split_specs(t, d, n_first, 0)
        out_shape = [jax.ShapeDtypeStruct((n_first * t, d), F32), jax.ShapeDtypeStruct((m - n_first * t, d), F32)]
    else:
        out_specs = [pl.BlockSpec((t, d), lambda i: (i, 0))]
        out_shape = [jax.ShapeDtypeStruct((m, d), F32)]
    return pl.pallas_call(
        functools.partial(_merge_kernel, d=d, final=final, n_first=n_first),
        grid=(m // t,),
        in_specs=(pair(a) + pair(b) + pair(c) + pair((x1, x2), x_off) + pair(p)
                  + [pl.BlockSpec((t, 3 * d), lambda i: (i, 0))]
                  + [full(wa), full(wb), full(wc), full(wo), full(pg), full(wpg), full(wple), full(fg)]),
        out_specs=out_specs,
        out_shape=out_shape,
        compiler_params=_cparams(("arbitrary",)),
        name="merge",
    )(*a, *b, *c, x1, x2, *p, gates, wa, wb, wc, wo, pg, wpg, wple, fg)


def kernel(x_prompt, x_sample, cache_k, cache_v, cache_idx_k, state_conv, p_prompt, p_sample, norm_g, w_in, gmlp_ln_g, gmlp_ln_b, gmlp_ws, gmlp_bs, conv_w, conv_b, conv_ln_g, conv_ln_b, w_branch_a, w_branch_b, w_branch_c, w_out, ple_norm_g, w_ple_gate, w_ple, final_norm_g):
    batch, seq, d = x_prompt.shape
    dec_batch, dec_seq, _ = x_sample.shape
    depth = w_in.shape[0]
    past = cache_k.shape[2]
    bw = gmlp_ln_g.shape[1]
    cw = conv_b.shape[1]
    mp = batch * seq
    ms = dec_batch * dec_seq
    assert mp % ROW_TILE == 0 and ms % ROW_TILE == 0 and seq % ROW_TILE == 0
    assert dec_seq <= CHUNK and B_CHUNK % dec_seq == 0 and past % CHUNK == 0

    widths = (A_WIDTH, A_KV_WIDTH, A_KV_WIDTH, IDX_HEADS * IDX_DIM, IDX_DIM, IDX_HEADS, A_WIDTH,
              2 * bw, bw, 2 * cw, cw, N_BRANCH * d)
    cuts = np.concatenate([[0], np.cumsum(widths)])
    out_widths = (QW, KVW, A_WIDTH, 2 * bw, bw, 2 * cw, cw, N_BRANCH * d)

    tail0 = QW + KVW
    placed = [(0, 0), (3, Q_IQ), (5, Q_IW), (1, QW), (2, QW + KV_V), (4, QW + KV_IK),
              (6, tail0)]
    pieces = [(int(cuts[j]), int(widths[j]), dst) for j, dst in placed[:-1]]
    pieces.append((int(cuts[6]), int(cuts[12] - cuts[6]), tail0))
    w1 = _wprep(w_in, tuple(pieces), sum(out_widths))

    n_first = mp // ROW_TILE
    x_pair = (x_prompt.reshape(mp, d), x_sample.reshape(ms, d), n_first, 0)
    rep = B_CHUNK // dec_seq
    eye = jnp.eye(rep, dtype=F32)

    kp, vp, ikp, cvp, ks, vs, iks, cvs, gvs = ([] for _ in range(9))
    for l in range(depth):
        pq, pkv, za, uv, zb, glu, zc, gates = _proj(x_pair, mp + ms, norm_g[l][None, :], w1, l, out_widths)

        a_p = _attn_prompt(pq, pkv, za, batch, seq)
        a_s = _attn_sample(pq, pkv, za,
                           cache_k[l].reshape(dec_batch, past, A_KV_WIDTH),
                           cache_v[l].reshape(dec_batch, past, A_KV_WIDTH),
                           cache_idx_k[l], mp, dec_batch, dec_seq)

        ln_g, ln_b = gmlp_ln_g[l][None, :], gmlp_ln_b[l][None, :]
        (b_p,) = _gmlp(uv, zb, ln_g, ln_b, gmlp_ws[l], gmlp_bs[l][:, :, None], 0, mp, True, False)
        ws_s = jnp.einsum('ab,gij->gaibj', eye, gmlp_ws[l][:, :dec_seq, :dec_seq]).reshape(
            B_GROUPS, B_CHUNK, B_CHUNK)
        bs_s = jnp.tile(gmlp_bs[l][:, :dec_seq], (1, rep))[:, :, None]
        b_s, gv = _gmlp(uv, zb, ln_g, ln_b, ws_s, bs_s, mp, ms, False, True)

        cargs = (jnp.repeat(conv_w[l], SUBLANES, axis=0), conv_b[l][None, :],
                 conv_ln_g[l][None, :], conv_ln_b[l][None, :])
        c_p, tail_p = _conv_prompt(glu, zc, *cargs, batch, seq)
        c_s, tail_s = _conv_sample(glu, zc, state_conv[l], *cargs, mp, dec_batch, dec_seq)

        p = (p_prompt[l].reshape(mp, -1), p_sample[l].reshape(ms, -1))
        outs = _merge((a_p, a_s), (b_p, b_s), (c_p, c_s), x_pair, p, gates,
                      w_branch_a[l].astype(BF16), w_branch_b[l].astype(BF16), w_branch_c[l].astype(BF16),
                      w_out[l].astype(BF16), ple_norm_g[l][None, :], w_ple_gate[l].astype(BF16),
                      w_ple[l].astype(BF16), final_norm_g[None, :], l == depth - 1)
        x_pair = (outs[0], outs[0], n_first, n_first)

        kp.append(pkv[:mp, 0:A_KV_WIDTH].reshape(batch, seq, A_KV_HEADS, A_HEAD_DIM))
        vp.append(pkv[:mp, KV_V:KV_V + A_KV_WIDTH].reshape(batch, seq, A_KV_HEADS, A_HEAD_DIM))
        ikp.append(pkv[:mp, KV_IK:KV_IK + IDX_DIM].reshape(batch, seq, IDX_DIM))
        cvp.append(tail_p)
        ks.append(pkv[mp:, 0:A_KV_WIDTH].reshape(dec_batch, dec_seq, A_KV_HEADS, A_HEAD_DIM))
        vs.append(pkv[mp:, KV_V:KV_V + A_KV_WIDTH].reshape(dec_batch, dec_seq, A_KV_HEADS, A_HEAD_DIM))
        iks.append(pkv[mp:, KV_IK:KV_IK + IDX_DIM].reshape(dec_batch, dec_seq, IDX_DIM))
        cvs.append(tail_s)
        gvs.append(gv.reshape(dec_batch, dec_seq, bw))

    return (outs[0].reshape(batch, seq, d), outs[1].reshape(dec_batch, dec_seq, d),
            jnp.stack(kp), jnp.stack(vp), jnp.stack(ikp), jnp.stack(cvp),
            jnp.stack(ks), jnp.stack(vs), jnp.stack(iks), jnp.stack(cvs), jnp.stack(gvs))
```

```python
import functools

import jax
import jax.numpy as jnp
import numpy as np
from jax import lax
from jax.experimental import pallas as pl
from jax.experimental.pallas import tpu as pltpu

F32 = jnp.float32
BF16 = jnp.bfloat16

CHUNK = 64
EPS = 1e-6
A_HEAD_DIM = 64
A_HEADS = 8
A_KV_HEADS = 2
A_REP = A_HEADS // A_KV_HEADS
A_WIDTH = A_HEADS * A_HEAD_DIM
A_KV_WIDTH = A_KV_HEADS * A_HEAD_DIM
IDX_HEADS = 4
IDX_DIM = 64
IDX_W_SCALE = (IDX_HEADS * IDX_DIM) ** -0.5
TOPK_MAX = 256
B_GROUPS = 4
B_CHUNK = 128
C_CONV = 31
N_BRANCH = 3

QW = 896
KVW = 384
Q_IQ = A_WIDTH
Q_IW = A_WIDTH + IDX_HEADS * IDX_DIM
KV_V = A_KV_WIDTH
KV_IK = 2 * A_KV_WIDTH

VMEM_LIMIT = 56 * 1024 * 1024
ROW_TILE = 256
KEY_TILE = 256
SEARCH_STEPS = 4
SEARCH_ROUNDS = 96
TIE_CHECK_FROM = 3
NEG_INF = float("-inf")


def _cparams(sem):
    return pltpu.CompilerParams(dimension_semantics=sem, vmem_limit_bytes=VMEM_LIMIT)


def _rms(xf, g):
    return xf * lax.rsqrt(jnp.mean(jnp.square(xf), axis=-1, keepdims=True) + EPS) * g


def _ln(xf, g, b):
    xc = xf - jnp.mean(xf, axis=-1, keepdims=True)
    var = jnp.mean(jnp.square(xc), axis=-1, keepdims=True)
    return xc * lax.rsqrt(var + EPS) * g + b


def _silu(x):
    return x * jax.nn.sigmoid(x)


def _dot(a, b):
    return jnp.dot(a, b, preferred_element_type=F32)


def _dot_nt(a, b):
    return lax.dot_general(a, b, (((1,), (1,)), ((), ())), preferred_element_type=F32)


def _wprep_kernel(w_ref, o_ref, *, pieces):
    rows, n_out = o_ref.shape
    covered = 0
    for src, wd, dst in pieces:
        if dst > covered:
            o_ref[:, covered:dst] = jnp.zeros((rows, dst - covered), BF16)
        for c in range(0, wd, 1024):
            cw = min(1024, wd - c)
            o_ref[:, dst + c:dst + c + cw] = w_ref[:, src + c:src + c + cw].astype(BF16)
        covered = dst + wd
    if covered < n_out:
        o_ref[:, covered:] = jnp.zeros((rows, n_out - covered), BF16)


def _wprep(w, pieces, n_out):
    depth, d, n = w.shape
    return pl.pallas_call(
        functools.partial(_wprep_kernel, pieces=pieces),
        grid=(depth, d // ROW_TILE),
        in_specs=[pl.BlockSpec((None, ROW_TILE, n), lambda l, i: (l, i, 0))],
        out_specs=pl.BlockSpec((None, ROW_TILE, n_out), lambda l, i: (l, i, 0)),
        out_shape=jax.ShapeDtypeStruct((depth, d, n_out), BF16),
        compiler_params=_cparams(("parallel", "parallel")),
        name="wprep",
    )(w)


def _split_specs(t, width, n_first, second_off, first_off=0):
    first = pl.BlockSpec((t, width), lambda i: (jnp.minimum(i, n_first - 1) + first_off, 0))
    second = pl.BlockSpec((t, width), lambda i: (jnp.maximum(i - n_first, 0) + second_off, 0))
    return [first, second]


def _pick(n_first, first_ref, second_ref):
    return jnp.where(pl.program_id(0) < n_first, first_ref[...], second_ref[...])


def _proj_kernel(x1_ref, x2_ref, g_ref, w_ref, *out_refs, widths, n_first):
    h = _rms(_pick(n_first, x1_ref, x2_ref), g_ref[...]).astype(BF16)
    off = 0
    for o_ref, wd in zip(out_refs, widths):
        for c in range(0, wd, 512):
            cw = min(512, wd - c)
            o_ref[:, c:c + cw] = _dot(h, w_ref[:, off + c:off + c + cw])
        off += wd


def _proj(x_pair, m, g, w, layer, widths):
    x1, x2, n_first, second_off = x_pair
    d = x1.shape[1]
    n = w.shape[2]
    return pl.pallas_call(
        functools.partial(_proj_kernel, widths=widths, n_first=n_first),
        grid=(m // ROW_TILE,),
        in_specs=_split_specs(ROW_TILE, d, n_first, second_off) + [
            pl.BlockSpec((1, d), lambda i: (0, 0)),
            pl.BlockSpec((None, d, n), lambda i: (layer, 0, 0), pipeline_mode=pl.Buffered(1)),
        ],
        out_specs=[pl.BlockSpec((ROW_TILE, wd), lambda i: (i, 0)) for wd in widths],
        out_shape=[jax.ShapeDtypeStruct((m, wd), F32) for wd in widths],
        compiler_params=_cparams(("parallel",)),
        name="proj",
    )(x1, x2, g, w)


LANES = 128
TQ_PROMPT = 256
TQ_SAMPLE = 128


def _rows8(x):
    return [x[8 * j:8 * j + 8] for j in range(x.shape[0] // 8)]


def _tree(parts, op):
    while len(parts) > 1:
        nxt = [op(parts[2 * j], parts[2 * j + 1]) for j in range(len(parts) // 2)]
        if len(parts) % 2:
            nxt.append(parts[-1])
        parts = nxt
    return parts[0]


def _all8(x, op):
    for shift in (4, 2, 1):
        x = op(x, pltpu.roll(x, shift, 0))
    return x


def _fold(x, op):
    return _all8(_tree(_rows8(x), op), op)


def _per_rows(op, x, v8):
    return jnp.concatenate([op(part, v8) for part in _rows8(x)], axis=0)


def _head_slabs(x, n, dst):
    t = x.shape[0]
    keep = (lax.broadcasted_iota(jnp.int32, (t, LANES), 1) // A_HEAD_DIM) == dst
    parts = []
    for h in range(n):
        slab = x[:, (h // 2) * LANES:(h // 2 + 1) * LANES]
        if h % 2 != dst:
            slab = pltpu.roll(slab, A_HEAD_DIM, 1)
        parts.append(jnp.where(keep, slab, 0.0))
    return parts


def _to_cols(x):
    return jnp.concatenate([x[c:c + LANES, :].T for c in range(0, KEY_TILE, LANES)], axis=1)


def _to_rows(x_t):
    return jnp.concatenate([x_t[:, c:c + LANES].T for c in range(0, KEY_TILE, LANES)], axis=0)


def _fill_keys(kt, j, k, v_t, ik, kb_ref, vt_ref, ikb_ref):
    rows = pl.ds(kt * KEY_TILE, KEY_TILE)
    kb_ref[rows, j * LANES:(j + 1) * LANES] = k.astype(BF16)
    ikb_ref[rows, j * LANES:(j + 1) * LANES] = ik.astype(BF16)
    for g in range(A_KV_HEADS):
        vt_ref[kt, g, j * A_HEAD_DIM:(j + 1) * A_HEAD_DIM, :] = (
            v_t[g * A_HEAD_DIM:(g + 1) * A_HEAD_DIM, :].astype(BF16))


def _attend(qrow, za, kb_ref, vt_ref, ikb_ref, score_ref, lg_ref, p_ref, acc_ref, *, nt, q_pos0, n_keys, topk, nb=1):
    tk = KEY_TILE
    TQ = qrow.shape[0]
    tqb = TQ // nb

    def spread(x):
        if nb == 1:
            return x
        row_set = lax.broadcasted_iota(jnp.int32, x.shape, 0) // tqb
        return jnp.concatenate([jnp.where(row_set == j, x, 0.0) for j in range(nb)], axis=1)

    iq_h = [spread(s).astype(BF16) for s in _head_slabs(qrow[:, Q_IQ:Q_IW], IDX_HEADS, 0)]
    iw_t = qrow[:, Q_IW:Q_IW + LANES].T * IDX_W_SCALE
    iw_rows = [iw_t[h:h + 1, :] for h in range(IDX_HEADS)]

    k_row = lax.broadcasted_iota(jnp.int32, (tk, TQ), 0)
    q_pos = q_pos0 + lax.broadcasted_iota(jnp.int32, (tk, TQ), 1) % tqb
    q_chunk = q_pos // CHUNK

    def visible(kt):
        k_pos = kt * tk + k_row
        return ((k_pos // CHUNK) <= q_chunk) & (k_pos < n_keys), k_pos

    def score_body(kt, carry):
        mn, mx = carry
        ik_t = ikb_ref[pl.ds(pl.multiple_of(kt * tk, tk), tk), :]
        s = jnp.zeros((tk, TQ), F32)
        for h in range(IDX_HEADS):
            s = s + jnp.maximum(_dot_nt(ik_t, iq_h[h]), 0.0) * iw_rows[h]
        vis, _ = visible(kt)
        score_ref[kt] = jnp.where(vis, s, NEG_INF)
        mn = jnp.minimum(mn, _tree(_rows8(jnp.where(vis, s, jnp.inf)), jnp.minimum))
        mx = jnp.maximum(mx, _tree(_rows8(jnp.where(vis, s, NEG_INF)), jnp.maximum))
        return mn, mx

    mn, mx = lax.fori_loop(0, nt, score_body,
                           (jnp.full((8, TQ), jnp.inf, F32), jnp.full((8, TQ), NEG_INF, F32)))
    mx = _all8(mx, jnp.maximum)

    q_pos8 = q_pos0 + lax.broadcasted_iota(jnp.int32, (8, TQ), 1) % tqb
    n_vis = jnp.minimum((q_pos8 // CHUNK + 1) * CHUNK, n_keys)
    k_eff = jnp.minimum(n_vis, topk).astype(F32)

    def count_ge(thr):
        def body(kt, acc):
            return acc + _tree([jnp.where(part >= thr, 1.0, 0.0) for part in _rows8(score_ref[kt])], jnp.add)
        return _all8(lax.fori_loop(0, nt, body, jnp.zeros((8, TQ), F32)), jnp.add)

    def band_open(lo, hi, still):
        def body(kt, carry):
            bmn, bmx = carry
            lows, highs = [], []
            for part in _rows8(score_ref[kt]):
                inb = (part >= lo) & (part < hi)
                lows.append(jnp.where(inb, part, jnp.inf))
                highs.append(jnp.where(inb, part, NEG_INF))
            return (jnp.minimum(bmn, _tree(lows, jnp.minimum)), jnp.maximum(bmx, _tree(highs, jnp.maximum)))
        bmn, bmx = lax.fori_loop(0, nt, body,
                                 (jnp.full((8, TQ), jnp.inf, F32), jnp.full((8, TQ), NEG_INF, F32)))
        distinct = _all8(bmn, jnp.minimum) < _all8(bmx, jnp.maximum)
        return jnp.max(jnp.where(still & distinct, 1.0, 0.0)).astype(F32)

    def search_round(state):
        it, _, lo, hi, c_lo, c_hi = state
        for _ in range(SEARCH_STEPS):
            mid = lo + (hi - lo) * 0.5
            inside = (mid > lo) & (mid < hi)
            c = count_ge(mid)
            up = inside & (c >= k_eff)
            dn = inside & (c < k_eff)
            lo = jnp.where(up, mid, lo)
            c_lo = jnp.where(up, c, c_lo)
            hi = jnp.where(dn, mid, hi)
            c_hi = jnp.where(dn, c, c_hi)
        still = c_lo != k_eff
        pending = jnp.max(jnp.where(still, 1.0, 0.0)).astype(F32)
        pending = lax.cond((pending > 0.0) & (it >= TIE_CHECK_FROM), band_open, lambda *_: pending, lo, hi, still)
        return it + 1, pending, lo, hi, c_lo, c_hi

    lo0 = _all8(mn, jnp.minimum)
    hi0 = mx + jnp.maximum(jnp.abs(mx), 1e-30) * (2.0 ** -10)
    pending0 = jnp.max(jnp.where(n_vis.astype(F32) != k_eff, 1.0, 0.0)).astype(F32)
    init = (jnp.int32(0), pending0, lo0, hi0, n_vis.astype(F32), jnp.zeros((8, TQ), F32))
    _, _, lo, hi, _, c_hi = lax.while_loop(
        lambda st: (st[0] < SEARCH_ROUNDS) & (st[1] > 0.0), search_round, init)
    lo_r, hi_r = lo[0:1], hi[0:1]
    need_r = (k_eff - c_hi)[0:1]

    tril = (lax.broadcasted_iota(jnp.int32, (tk, tk), 1)
            <= lax.broadcasted_iota(jnp.int32, (tk, tk), 0)).astype(BF16)

    def select_body(kt, carry):
        s = score_ref[kt]
        inb = (s >= lo_r) & (s < hi_r)
        band = jnp.where(inb, 1.0, 0.0)
        rank = _dot(tril, band.astype(BF16)) + carry
        sel = (s >= hi_r) | (inb & (rank <= need_r))
        _, k_pos = visible(kt)
        dist = jnp.abs(q_pos - k_pos).astype(F32)
        score_ref[kt] = jnp.where(sel, -dist, NEG_INF)
        return carry + jnp.sum(band, axis=0, keepdims=True)

    lax.fori_loop(0, nt, select_body, jnp.zeros((1, TQ), F32))

    slopes = [2.0 ** (-8.0 * (h + 1) / A_HEADS) for h in range(A_HEADS)]
    gw = A_REP * A_HEAD_DIM
    q_gs = []
    for g in range(A_KV_HEADS):
        q_g = jnp.concatenate([spread(s) for s in _head_slabs(qrow[:, g * gw:(g + 1) * gw], A_REP, g)], axis=0)
        q_gs.append((q_g * (A_HEAD_DIM ** -0.5)).astype(BF16))
    acc_ref[...] = jnp.zeros(acc_ref.shape, F32)
    lane_set = (lax.broadcasted_iota(jnp.int32, (A_HEAD_DIM, A_REP * TQ), 1) % TQ) // tqb

    def attn_body(kt, carry):
        ms, ls = carry
        k_t = kb_ref[pl.ds(pl.multiple_of(kt * tk, tk), tk), :]
        nd = score_ref[kt]
        for g in range(A_KV_HEADS):
            lg_ref[g] = _dot_nt(k_t, q_gs[g])
        ms_new, ls_new = [], []
        for g in range(A_KV_HEADS):
            corrs = []
            for r in range(A_REP):
                h = g * A_REP + r
                lgr = lg_ref[g, :, r * TQ:(r + 1) * TQ] + slopes[h] * nd
                m_new = jnp.maximum(ms[h], _fold(lgr, jnp.maximum))
                m_safe = jnp.where(m_new == NEG_INF, 0.0, m_new)
                p = _per_rows(lambda a, m: jnp.exp(a - m), lgr, m_safe)
                corr = jnp.exp(ms[h] - m_safe)
                ls_new.append(ls[h] * corr + _fold(p, jnp.add))
                ms_new.append(m_new)
                corrs.append(corr)
                p_ref[g, :, r * TQ:(r + 1) * TQ] = p.astype(BF16)
            pv = _dot(vt_ref[kt, g], p_ref[g])
            if nb > 1:
                pv = sum(jnp.where(lane_set == j, pv[j * A_HEAD_DIM:(j + 1) * A_HEAD_DIM], 0.0) for j in range(nb))
            acc_ref[g] = _per_rows(jnp.multiply, acc_ref[g], jnp.concatenate(corrs, axis=1)) + pv
        return tuple(ms_new), tuple(ls_new)

    init = (tuple(jnp.full((8, TQ), NEG_INF, F32) for _ in range(A_HEADS)),
            tuple(jnp.zeros((8, TQ), F32) for _ in range(A_HEADS)))
    _, ls = lax.fori_loop(0, nt, attn_body, init)
    pieces = []
    for g in range(A_KV_HEADS):
        o_t = _per_rows(jnp.divide, acc_ref[g], jnp.concatenate(ls[g * A_REP:(g + 1) * A_REP], axis=1))
        for r in range(0, A_REP, 2):
            pair = jnp.concatenate([o_t[:, r * TQ:(r + 1) * TQ], o_t[:, (r + 1) * TQ:(r + 2) * TQ]], axis=0)
            pieces.append(pair.T)
    o_a = jnp.concatenate(pieces, axis=-1)
    return o_a * _silu(za)


def _attn_scratch(n_tiles, TQ, nb):
    return [
        pltpu.VMEM((n_tiles * KEY_TILE, nb * LANES), BF16),
        pltpu.VMEM((n_tiles, A_KV_HEADS, nb * A_HEAD_DIM, KEY_TILE), BF16),
        pltpu.VMEM((n_tiles * KEY_TILE, nb * LANES), BF16),
        pltpu.VMEM((n_tiles, KEY_TILE, TQ), F32),
        pltpu.VMEM((A_KV_HEADS, KEY_TILE, A_REP * TQ), F32),
        pltpu.VMEM((A_KV_HEADS, KEY_TILE, A_REP * TQ), BF16),
        pltpu.VMEM((A_KV_HEADS, A_HEAD_DIM, A_REP * TQ), F32),
    ]


def _attn_prompt_kernel(q_ref, kv_ref, za_ref, o_ref, kt_out, vt_out, ikt_out, kb_ref, vt_ref, ikb_ref, *work,
                        seq, topk):
    i = pl.program_id(1)

    @pl.when(i == 0)
    def _():
        for kt in range(seq // KEY_TILE):
            rows = slice(kt * KEY_TILE, (kt + 1) * KEY_TILE)
            k, ik = kv_ref[rows, 0:LANES], kv_ref[rows, KV_IK:KV_IK + LANES]
            v_t = _to_cols(kv_ref[rows, KV_V:KV_V + LANES])
            _fill_keys(kt, 0, k, v_t, ik, kb_ref, vt_ref, ikb_ref)
            kt_out[:, rows] = _to_cols(k)
            vt_out[:, rows] = v_t
            ikt_out[:, rows] = _to_cols(ik)[0:IDX_DIM, :]

    TQ = TQ_PROMPT
    nt = (i * TQ + TQ + KEY_TILE - 1) // KEY_TILE
    out = _attend(q_ref[...], za_ref[...], kb_ref, vt_ref, ikb_ref, *work,
                  nt=nt, q_pos0=i * TQ, n_keys=seq, topk=topk)
    o_ref[...] = out.astype(BF16)


def _attn_prompt(pq, pkv, za, batch, seq):
    TQ = TQ_PROMPT
    nq = seq // TQ
    topk = min(TOPK_MAX, seq // 4)
    return pl.pallas_call(
        functools.partial(_attn_prompt_kernel, seq=seq, topk=topk),
        grid=(batch, nq),
        in_specs=[
            pl.BlockSpec((TQ, QW), lambda b, i: (b * nq + i, 0)),
            pl.BlockSpec((seq, KVW), lambda b, i: (b, 0)),
            pl.BlockSpec((TQ, A_WIDTH), lambda b, i: (b * nq + i, 0)),
        ],
        out_specs=[
            pl.BlockSpec((TQ, A_WIDTH), lambda b, i: (b * nq + i, 0)),
            pl.BlockSpec((A_KV_WIDTH, seq), lambda b, i: (b, 0)),
            pl.BlockSpec((A_KV_WIDTH, seq), lambda b, i: (b, 0)),
            pl.BlockSpec((IDX_DIM, seq), lambda b, i: (b, 0)),
        ],
        out_shape=[
            jax.ShapeDtypeStruct((batch * seq, A_WIDTH), BF16),
            jax.ShapeDtypeStruct((batch * A_KV_WIDTH, seq), F32),
            jax.ShapeDtypeStruct((batch * A_KV_WIDTH, seq), F32),
            jax.ShapeDtypeStruct((batch * IDX_DIM, seq), F32),
        ],
        scratch_shapes=_attn_scratch(seq // KEY_TILE, TQ, 1),
        compiler_params=_cparams(("parallel", "arbitrary")),
        name="attn_prompt",
    )(pq, pkv, za)


def _attn_sample_kernel(q_ref, kv_ref, za_ref, ck_ref, cv_ref, cik_ref, o_ref,
                        kb_ref, vt_ref, ikb_ref, *work, t, past, topk, nt, nb):
    tk = KEY_TILE
    tail = nt * tk - past
    zeros = lambda n, w: jnp.zeros((n, w), F32)
    pad_rows = lambda x: jnp.concatenate([x, zeros(tail - t, LANES)], axis=0)
    for j in range(nb):
        for kt in range(past // tk):
            cols = slice(kt * tk, (kt + 1) * tk)
            ik_t = jnp.concatenate([cik_ref[j * IDX_DIM:(j + 1) * IDX_DIM, cols], zeros(LANES - IDX_DIM, tk)], axis=0)
            _fill_keys(kt, j, _to_rows(ck_ref[j * LANES:(j + 1) * LANES, cols]),
                       cv_ref[j * LANES:(j + 1) * LANES, cols], _to_rows(ik_t), kb_ref, vt_ref, ikb_ref)
        new = kv_ref[j * t:(j + 1) * t, :]
        _fill_keys(past // tk, j, pad_rows(new[:, 0:LANES]), _to_cols(pad_rows(new[:, KV_V:KV_V + LANES])),
                   pad_rows(new[:, KV_IK:KV_IK + LANES]), kb_ref, vt_ref, ikb_ref)
    out = _attend(q_ref[...], za_ref[...], kb_ref, vt_ref, ikb_ref, *work,
                  nt=nt, q_pos0=past, n_keys=past + t, topk=topk, nb=nb)
    o_ref[...] = out.astype(BF16)


def _attn_sample(pq, pkv, za, ck, cv, cik, layer, row0, dec_batch, t):
    past = ck.shape[1]
    nb = TQ_SAMPLE // t
    cblk0 = layer * (dec_batch // nb)
    assert past % KEY_TILE == 0 and t <= KEY_TILE and TQ_SAMPLE % t == 0 and dec_batch % nb == 0
    topk = min(TOPK_MAX, (past + t) // 4)
    nt = past // KEY_TILE + 1
    blk0 = row0 // TQ_SAMPLE
    return pl.pallas_call(
        functools.partial(_attn_sample_kernel, t=t, past=past, topk=topk, nt=nt, nb=nb),
        grid=(dec_batch // nb,),
        in_specs=[
            pl.BlockSpec((TQ_SAMPLE, QW), lambda b: (blk0 + b, 0)),
            pl.BlockSpec((TQ_SAMPLE, KVW), lambda b: (blk0 + b, 0)),
            pl.BlockSpec((TQ_SAMPLE, A_WIDTH), lambda b: (blk0 + b, 0)),
            pl.BlockSpec((nb * A_KV_WIDTH, past), lambda b: (cblk0 + b, 0)),
            pl.BlockSpec((nb * A_KV_WIDTH, past), lambda b: (cblk0 + b, 0)),
            pl.BlockSpec((nb * IDX_DIM, past), lambda b: (cblk0 + b, 0)),
        ],
        out_specs=pl.BlockSpec((TQ_SAMPLE, A_WIDTH), lambda b: (b, 0)),
        out_shape=jax.ShapeDtypeStruct((dec_batch * t, A_WIDTH), BF16),
        scratch_shapes=_attn_scratch(nt, TQ_SAMPLE, nb),
        compiler_params=_cparams(("parallel",)),
        name="attn_sample",
    )(pq, pkv, za, ck, cv, cik)


GMLP_ROWS = 512


def _gmlp_kernel(uv_ref, zb_ref, g_ref, b_ref, w_ref, bs_ref, o_ref, *v_out, width, mask_chunks):
    gw = width // B_GROUPS
    ws = []
    for g in range(B_GROUPS):
        w = w_ref[g]
        if mask_chunks:
            i = lax.broadcasted_iota(jnp.int32, (B_CHUNK, B_CHUNK), 0)
            j = lax.broadcasted_iota(jnp.int32, (B_CHUNK, B_CHUNK), 1)
            w = jnp.where((j // CHUNK) <= (i // CHUNK), w, 0.0)
        ws.append(w.astype(BF16))
    for c in range(uv_ref.shape[0] // B_CHUNK):
        rows = slice(c * B_CHUNK, (c + 1) * B_CHUNK)
        uv = uv_ref[rows, :]
        act = 0.5 * uv * (1.0 + lax.erf(uv * np.float32(1.0 / np.sqrt(2.0))))
        u = act[:, :width]
        v = _ln(act[:, width:], g_ref[...], b_ref[...])
        if v_out:
            v_out[0][rows, :] = v
        vb = v.astype(BF16)
        zs = _silu(zb_ref[rows, :])
        for g in range(B_GROUPS):
            cols = slice(g * gw, (g + 1) * gw)
            mixed = _dot(ws[g], vb[:, cols]) + bs_ref[g]
            o_ref[rows, cols] = (u[:, cols] * mixed * zs[:, cols]).astype(BF16)


def _gmlp(uv, zb, ln_g, ln_b, w, bs, row0, rows, mask_chunks, want_v):
    width = zb.shape[1]
    t = GMLP_ROWS
    assert row0 % t == 0 and rows % t == 0
    blk0 = row0 // t
    out_specs = [pl.BlockSpec((t, width), lambda i: (i, 0))]
    out_shape = [jax.ShapeDtypeStruct((rows, width), BF16)]
    if want_v:
        out_specs.append(pl.BlockSpec((t, width), lambda i: (i, 0)))
        out_shape.append(jax.ShapeDtypeStruct((rows, width), F32))
    return pl.pallas_call(
        functools.partial(_gmlp_kernel, width=width, mask_chunks=mask_chunks),
        grid=(rows // t,),
        in_specs=[
            pl.BlockSpec((t, 2 * width), lambda i: (blk0 + i, 0)),
            pl.BlockSpec((t, width), lambda i: (blk0 + i, 0)),
            pl.BlockSpec((1, width), lambda i: (0, 0)),
            pl.BlockSpec((1, width), lambda i: (0, 0)),
            pl.BlockSpec((B_GROUPS, B_CHUNK, B_CHUNK), lambda i: (0, 0, 0)),
            pl.BlockSpec((B_GROUPS, B_CHUNK, 1), lambda i: (0, 0, 0)),
        ],
        out_specs=out_specs,
        out_shape=out_shape,
        compiler_params=_cparams(("parallel",)),
        name="gmlp_sample" if want_v else "gmlp_prompt",
    )(uv, zb, ln_g, ln_b, w, bs)


CONV_PAD = 32


CONV_ROWS = 64
SUBLANES = 8


def _conv_tile(glu_ref, zc_ref, o_ref, ext_ref, y_ref, w_ref, b_ref, g_ref, beta_ref, t, width):
    ext_ref[0, CONV_PAD:CONV_PAD + t, :] = glu_ref[:, :width] * jax.nn.sigmoid(glu_ref[:, width:])
    n = t + CONV_PAD - SUBLANES
    for s in range(1, SUBLANES):
        ext_ref[s, 0:n, :] = ext_ref[0, s:s + n, :]
    base = CONV_PAD - (C_CONV - 1)

    rc = min(CONV_ROWS, t)

    def rows_step(ci, carry):
        r0 = pl.multiple_of(ci * rc, rc)
        y = jnp.zeros((rc, width), F32)
        for k in range(C_CONV):
            s, q = (base + k) % SUBLANES, (base + k) // SUBLANES
            tap = w_ref[SUBLANES * k:SUBLANES * (k + 1), :]
            y = y + _per_rows(jnp.multiply, ext_ref[s, pl.ds(r0 + SUBLANES * q, rc), :], tap)
        y_ref[pl.ds(r0, rc), :] = y
        return carry

    lax.fori_loop(0, t // rc, rows_step, 0)
    y = y_ref[...] + b_ref[...]
    o_ref[...] = (_silu(_ln(y, g_ref[...], beta_ref[...])) * _silu(zc_ref[...])).astype(BF16)


def _conv_prompt_kernel(glu_ref, zc_ref, w_ref, b_ref, g_ref, beta_ref, o_ref, tail_ref, ext_ref, y_ref,
                        *, t, width):
    i = pl.program_id(1)

    @pl.when(i == 0)
    def _():
        ext_ref[0, 0:CONV_PAD, :] = jnp.zeros((CONV_PAD, width), F32)

    _conv_tile(glu_ref, zc_ref, o_ref, ext_ref, y_ref, w_ref, b_ref, g_ref, beta_ref, t, width)
    tail = ext_ref[0, t:t + CONV_PAD, :]
    ext_ref[0, 0:CONV_PAD, :] = tail

    @pl.when(i == pl.num_programs(1) - 1)
    def _():
        tail_ref[...] = tail[CONV_PAD - (C_CONV - 1):, :]


def _conv_prompt(glu, zc, w, b, ln_g, ln_b, batch, seq):
    t = ROW_TILE
    width = zc.shape[1]
    nb = seq // t
    return pl.pallas_call(
        functools.partial(_conv_prompt_kernel, t=t, width=width),
        grid=(batch, nb),
        in_specs=[
            pl.BlockSpec((t, 2 * width), lambda b_, i: (b_ * nb + i, 0)),
            pl.BlockSpec((t, width), lambda b_, i: (b_ * nb + i, 0)),
            pl.BlockSpec((SUBLANES * C_CONV, width), lambda b_, i: (0, 0)),
            pl.BlockSpec((1, width), lambda b_, i: (0, 0)),
            pl.BlockSpec((1, width), lambda b_, i: (0, 0)),
            pl.BlockSpec((1, width), lambda b_, i: (0, 0)),
        ],
        out_specs=[
            pl.BlockSpec((t, width), lambda b_, i: (b_ * nb + i, 0)),
            pl.BlockSpec((None, C_CONV - 1, width), lambda b_, i: (b_, 0, 0)),
        ],
        out_shape=[
            jax.ShapeDtypeStruct((batch * seq, width), BF16),
            jax.ShapeDtypeStruct((batch, C_CONV - 1, width), F32),
        ],
        scratch_shapes=[pltpu.VMEM((SUBLANES, CONV_PAD + t, width), F32), pltpu.VMEM((t, width), F32)],
        compiler_params=_cparams(("parallel", "arbitrary")),
        name="conv_prompt",
    )(glu, zc, w, b, ln_g, ln_b)


def _conv_sample_kernel(glu_ref, zc_ref, st_ref, w_ref, b_ref, g_ref, beta_ref, o_ref, tail_ref, ext_ref, y_ref,
                        *, t, width):
    base = CONV_PAD - (C_CONV - 1)
    ext_ref[0, 0:base, :] = jnp.zeros((base, width), F32)
    ext_ref[0, base:CONV_PAD, :] = st_ref[...]
    _conv_tile(glu_ref, zc_ref, o_ref, ext_ref, y_ref, w_ref, b_ref, g_ref, beta_ref, t, width)
    tail_ref[...] = ext_ref[0, t + base:t + CONV_PAD, :]


def _conv_sample(glu, zc, state, w, b, ln_g, ln_b, row0, dec_batch, t):
    width = zc.shape[1]
    blk0 = row0 // t
    return pl.pallas_call(
        functools.partial(_conv_sample_kernel, t=t, width=width),
        grid=(dec_batch,),
        in_specs=[
            pl.BlockSpec((t, 2 * width), lambda b_: (blk0 + b_, 0)),
            pl.BlockSpec((t, width), lambda b_: (blk0 + b_, 0)),
            pl.BlockSpec((None, C_CONV - 1, width), lambda b_: (b_, 0, 0)),
            pl.BlockSpec((SUBLANES * C_CONV, width), lambda b_: (0, 0)),
            pl.BlockSpec((1, width), lambda b_: (0, 0)),
            pl.BlockSpec((1, width), lambda b_: (0, 0)),
            pl.BlockSpec((1, width), lambda b_: (0, 0)),
        ],
        out_specs=[
            pl.BlockSpec((t, width), lambda b_: (b_, 0)),
            pl.BlockSpec((None, C_CONV - 1, width), lambda b_: (b_, 0, 0)),
        ],
        out_shape=[
            jax.ShapeDtypeStruct((dec_batch * t, width), BF16),
            jax.ShapeDtypeStruct((dec_batch, C_CONV - 1, width), F32),
        ],
        scratch_shapes=[pltpu.VMEM((SUBLANES, CONV_PAD + t, width), F32), pltpu.VMEM((t, width), F32)],
        compiler_params=_cparams(("parallel",)),
        name="conv_sample",
    )(glu, zc, state, w, b, ln_g, ln_b)


def _merge_kernel(a1, a2, b1, b2, c1, c2, x1, x2, p1, p2, gates_ref, wa_ref, wb_ref, wc_ref, wo_ref,
                  pg_ref, wpg_ref, wple_ref, fg_ref, *o_refs, d, final, n_first):
    pick = functools.partial(_pick, n_first)
    merged = (jax.nn.sigmoid(gates_ref[:, 0:d]) * _dot(pick(a1, a2), wa_ref[...])
              + jax.nn.sigmoid(gates_ref[:, d:2 * d]) * _dot(pick(b1, b2), wb_ref[...])
              + jax.nn.sigmoid(gates_ref[:, 2 * d:3 * d]) * _dot(pick(c1, c2), wc_ref[...]))
    x = pick(x1, x2) + _dot(merged.astype(BF16), wo_ref[...])
    gate = jax.nn.sigmoid(_dot(_rms(x, pg_ref[...]).astype(BF16), wpg_ref[...]))
    x = x + gate * _dot(pick(p1, p2).astype(BF16), wple_ref[...])
    if not final:
        o_refs[0][...] = x
        return
    y = _rms(x, fg_ref[...])
    on_first = pl.program_id(0) < n_first

    @pl.when(on_first)
    def _():
        o_refs[0][...] = y

    @pl.when(jnp.logical_not(on_first))
    def _():
        o_refs[1][...] = y


def _merge(a, b, c, x_pair, p_quad, gates, wa, wb, wc, wo, pg, wpg, wple, fg, final):
    x1, x2, n_first, x_off = x_pair
    p = p_quad[:2]
    m, d = gates.shape[0], x1.shape[1]
    t = ROW_TILE
    pair = lambda arrs, off=0, off1=0: _split_specs(t, arrs[0].shape[1], n_first, off, off1)
    full = lambda arr: pl.BlockSpec(arr.shape, lambda i: (0, 0), pipeline_mode=pl.Buffered(1))
    if final:
        out_specs = _split_specs(t, d, n_first, 0)
        out_shape = [jax.ShapeDtypeStruct((n_first * t, d), F32), jax.ShapeDtypeStruct((m - n_first * t, d), F32)]
    else:
        out_specs = [pl.BlockSpec((t, d), lambda i: (i, 0))]
        out_shape = [jax.ShapeDtypeStruct((m, d), F32)]
    return pl.pallas_call(
        functools.partial(_merge_kernel, d=d, final=final, n_first=n_first),
        grid=(m // t,),
        in_specs=(pair(a) + pair(b) + pair(c) + pair((x1, x2), x_off) + pair(p, p_quad[3], p_quad[2])
                  + [pl.BlockSpec((t, 3 * d), lambda i: (i, 0))]
                  + [full(wa), full(wb), full(wc), full(wo), full(pg), full(wpg), full(wple), full(fg)]),
        out_specs=out_specs,
        out_shape=out_shape,
        compiler_params=_cparams(("arbitrary",)),
        name="merge",
    )(*a, *b, *c, x1, x2, *p, gates, wa, wb, wc, wo, pg, wpg, wple, fg)


def kernel(x_prompt, x_sample, cache_k, cache_v, cache_idx_k, state_conv, p_prompt, p_sample, norm_g, w_in, gmlp_ln_g, gmlp_ln_b, gmlp_ws, gmlp_bs, conv_w, conv_b, conv_ln_g, conv_ln_b, w_branch_a, w_branch_b, w_branch_c, w_out, ple_norm_g, w_ple_gate, w_ple, final_norm_g):
    batch, seq, d = x_prompt.shape
    dec_batch, dec_seq, _ = x_sample.shape
    depth = w_in.shape[0]
    past = cache_k.shape[2]
    bw = gmlp_ln_g.shape[1]
    cw = conv_b.shape[1]
    mp = batch * seq
    ms = dec_batch * dec_seq
    assert mp % ROW_TILE == 0 and ms % ROW_TILE == 0 and seq % ROW_TILE == 0
    assert dec_seq <= CHUNK and B_CHUNK % dec_seq == 0 and past % CHUNK == 0

    widths = (A_WIDTH, A_KV_WIDTH, A_KV_WIDTH, IDX_HEADS * IDX_DIM, IDX_DIM, IDX_HEADS, A_WIDTH,
              2 * bw, bw, 2 * cw, cw, N_BRANCH * d)
    cuts = np.concatenate([[0], np.cumsum(widths)])
    out_widths = (QW, KVW, A_WIDTH, 2 * bw, bw, 2 * cw, cw, N_BRANCH * d)

    tail0 = QW + KVW
    placed = [(0, 0), (3, Q_IQ), (5, Q_IW), (1, QW), (2, QW + KV_V), (4, QW + KV_IK),
              (6, tail0)]
    pieces = [(int(cuts[j]), int(widths[j]), dst) for j, dst in placed[:-1]]
    pieces.append((int(cuts[6]), int(cuts[12] - cuts[6]), tail0))
    w1 = _wprep(w_in, tuple(pieces), sum(out_widths))

    n_first = mp // ROW_TILE
    x_pair = (x_prompt.reshape(mp, d), x_sample.reshape(ms, d), n_first, 0)
    rep = B_CHUNK // dec_seq
    eye = jnp.eye(rep, dtype=F32)
    p_all_prompt = p_prompt.reshape(depth * mp, -1)
    p_all_sample = p_sample.reshape(depth * ms, -1)
    ck_t = jnp.transpose(cache_k, (0, 1, 3, 4, 2)).reshape(depth * dec_batch * A_KV_WIDTH, past)
    cv_t = jnp.transpose(cache_v, (0, 1, 3, 4, 2)).reshape(depth * dec_batch * A_KV_WIDTH, past)
    cik_t = jnp.transpose(cache_idx_k, (0, 1, 3, 2)).reshape(depth * dec_batch * IDX_DIM, past)

    kp, vp, ikp, cvp, ks, vs, iks, cvs, gvs = ([] for _ in range(9))
    for l in range(depth):
        pq, pkv, za, uv, zb, glu, zc, gates = _proj(x_pair, mp + ms, norm_g[l][None, :], w1, l, out_widths)

        a_p, k_t, v_t, ik_t = _attn_prompt(pq, pkv, za, batch, seq)
        a_s = _attn_sample(pq, pkv, za, ck_t, cv_t, cik_t, l, mp, dec_batch, dec_seq)

        ln_g, ln_b = gmlp_ln_g[l][None, :], gmlp_ln_b[l][None, :]
        (b_p,) = _gmlp(uv, zb, ln_g, ln_b, gmlp_ws[l], gmlp_bs[l][:, :, None], 0, mp, True, False)
        ws_s = jnp.einsum('ab,gij->gaibj', eye, gmlp_ws[l][:, :dec_seq, :dec_seq]).reshape(
            B_GROUPS, B_CHUNK, B_CHUNK)
        bs_s = jnp.tile(gmlp_bs[l][:, :dec_seq], (1, rep))[:, :, None]
        b_s, gv = _gmlp(uv, zb, ln_g, ln_b, ws_s, bs_s, mp, ms, False, True)

        cargs = (jnp.repeat(conv_w[l], SUBLANES, axis=0), conv_b[l][None, :],
                 conv_ln_g[l][None, :], conv_ln_b[l][None, :])
        c_p, tail_p = _conv_prompt(glu, zc, *cargs, batch, seq)
        c_s, tail_s = _conv_sample(glu, zc, state_conv[l], *cargs, mp, dec_batch, dec_seq)

        p = (p_all_prompt, p_all_sample, l * n_first, l * (ms // ROW_TILE))
        outs = _merge((a_p, a_s), (b_p, b_s), (c_p, c_s), x_pair, p, gates,
                      w_branch_a[l].astype(BF16), w_branch_b[l].astype(BF16), w_branch_c[l].astype(BF16),
                      w_out[l].astype(BF16), ple_norm_g[l][None, :], w_ple_gate[l].astype(BF16),
                      w_ple[l].astype(BF16), final_norm_g[None, :], l == depth - 1)
        x_pair = (outs[0], outs[0], n_first, n_first)

        heads_last = lambda x_t: jnp.transpose(x_t.reshape(batch, A_KV_HEADS, A_HEAD_DIM, seq), (0, 3, 1, 2))
        kp.append(heads_last(k_t))
        vp.append(heads_last(v_t))
        ikp.append(jnp.transpose(ik_t.reshape(batch, IDX_DIM, seq), (0, 2, 1)))
        cvp.append(tail_p)
        ks.append(pkv[mp:, 0:A_KV_WIDTH].reshape(dec_batch, dec_seq, A_KV_HEADS, A_HEAD_DIM))
        vs.append(pkv[mp:, KV_V:KV_V + A_KV_WIDTH].reshape(dec_batch, dec_seq, A_KV_HEADS, A_HEAD_DIM))
        iks.append(pkv[mp:, KV_IK:KV_IK + IDX_DIM].reshape(dec_batch, dec_seq, IDX_DIM))
        cvs.append(tail_s)
        gvs.append(gv.reshape(dec_batch, dec_seq, bw))

    return (outs[0].reshape(batch, seq, d), outs[1].reshape(dec_batch, dec_seq, d),
            jnp.stack(kp), jnp.stack(vp), jnp.stack(ikp), jnp.stack(cvp),
            jnp.stack(ks), jnp.stack(vs), jnp.stack(iks), jnp.stack(cvs), jnp.stack(gvs))
```

```python
import functools

import jax
import jax.numpy as jnp
import numpy as np
from jax import lax
from jax.experimental import pallas as pl
from jax.experimental.pallas import tpu as pltpu

F32 = jnp.float32
BF16 = jnp.bfloat16

CHUNK = 64
EPS = 1e-6
A_HEAD_DIM = 64
A_HEADS = 8
A_KV_HEADS = 2
A_REP = A_HEADS // A_KV_HEADS
A_WIDTH = A_HEADS * A_HEAD_DIM
A_KV_WIDTH = A_KV_HEADS * A_HEAD_DIM
IDX_HEADS = 4
IDX_DIM = 64
IDX_W_SCALE = (IDX_HEADS * IDX_DIM) ** -0.5
TOPK_MAX = 256
B_GROUPS = 4
B_CHUNK = 128
C_CONV = 31
N_BRANCH = 3

QW = 896
KVW = 384
Q_IQ = A_WIDTH
Q_IW = A_WIDTH + IDX_HEADS * IDX_DIM
KV_V = A_KV_WIDTH
KV_IK = 2 * A_KV_WIDTH

VMEM_LIMIT = 56 * 1024 * 1024
ROW_TILE = 256
KEY_TILE = 256
SEARCH_STEPS = 4
SEARCH_ROUNDS = 96
TIE_CHECK_FROM = 4
NEG_INF = float("-inf")


def _cparams(sem):
    return pltpu.CompilerParams(dimension_semantics=sem, vmem_limit_bytes=VMEM_LIMIT)


def _rms(xf, g):
    return xf * lax.rsqrt(jnp.mean(jnp.square(xf), axis=-1, keepdims=True) + EPS) * g


def _ln(xf, g, b):
    xc = xf - jnp.mean(xf, axis=-1, keepdims=True)
    var = jnp.mean(jnp.square(xc), axis=-1, keepdims=True)
    return xc * lax.rsqrt(var + EPS) * g + b


def _silu(x):
    return x * jax.nn.sigmoid(x)


def _dot(a, b):
    return jnp.dot(a, b, preferred_element_type=F32)


def _dot_nt(a, b):
    return lax.dot_general(a, b, (((1,), (1,)), ((), ())), preferred_element_type=F32)


WPREP_ROWS = 128


def _wprep_kernel(src_ref, cnt_ref, w_ref, o_ref):
    i = pl.program_id(0)
    keep = lax.broadcasted_iota(jnp.int32, (WPREP_ROWS, w_ref.shape[2]), 0) < cnt_ref[i]
    for l in range(w_ref.shape[1]):
        o_ref[l] = jnp.where(keep, w_ref[:, l, :], 0.0).astype(BF16)


def _wprep(w_t, pieces, n_out):
    n, depth, d = w_t.shape
    src, cnt = [], []
    for dst0 in range(0, n_out, WPREP_ROWS):
        hit = [(s + dst0 - t, min(WPREP_ROWS, t + wd - dst0)) for s, wd, t in pieces if t <= dst0 < t + wd]
        assert len(hit) == 1 and hit[0][0] + WPREP_ROWS <= n
        src.append(hit[0][0])
        cnt.append(hit[0][1])
    grid_spec = pltpu.PrefetchScalarGridSpec(
        num_scalar_prefetch=2,
        grid=(n_out // WPREP_ROWS,),
        in_specs=[pl.BlockSpec((pl.Element(WPREP_ROWS), pl.Element(depth), pl.Element(d)),
                               lambda i, src_ref, cnt_ref: (src_ref[i], 0, 0))],
        out_specs=pl.BlockSpec((depth, WPREP_ROWS, d), lambda i, src_ref, cnt_ref: (0, i, 0)),
    )
    return pl.pallas_call(
        _wprep_kernel,
        grid_spec=grid_spec,
        out_shape=jax.ShapeDtypeStruct((depth, n_out, d), BF16),
        compiler_params=_cparams(("parallel",)),
        name="wprep",
    )(jnp.asarray(src, jnp.int32), jnp.asarray(cnt, jnp.int32), w_t)


def _split_specs(t, width, n_first, second_off, first_off=0):
    first = pl.BlockSpec((t, width), lambda i: (jnp.minimum(i, n_first - 1) + first_off, 0))
    second = pl.BlockSpec((t, width), lambda i: (jnp.maximum(i - n_first, 0) + second_off, 0))
    return [first, second]


def _pick(n_first, first_ref, second_ref):
    return jnp.where(pl.program_id(0) < n_first, first_ref[...], second_ref[...])


def _proj_kernel(x1_ref, x2_ref, g_ref, w_ref, *out_refs, widths, n_first):
    h = _rms(_pick(n_first, x1_ref, x2_ref), g_ref[...]).astype(BF16)
    off = 0
    for o_ref, wd in zip(out_refs, widths):
        for c in range(0, wd, 512):
            cw = min(512, wd - c)
            o_ref[:, c:c + cw] = _dot_nt(h, w_ref[off + c:off + c + cw, :])
        off += wd


def _proj(x_pair, m, g, w, layer, widths):
    x1, x2, n_first, second_off = x_pair
    d = x1.shape[1]
    n = w.shape[1]
    return pl.pallas_call(
        functools.partial(_proj_kernel, widths=widths, n_first=n_first),
        grid=(m // ROW_TILE,),
        in_specs=_split_specs(ROW_TILE, d, n_first, second_off) + [
            pl.BlockSpec((1, d), lambda i: (0, 0)),
            pl.BlockSpec((None, n, d), lambda i: (layer, 0, 0), pipeline_mode=pl.Buffered(1)),
        ],
        out_specs=[pl.BlockSpec((ROW_TILE, wd), lambda i: (i, 0)) for wd in widths],
        out_shape=[jax.ShapeDtypeStruct((m, wd), F32) for wd in widths],
        compiler_params=_cparams(("parallel",)),
        name="proj",
    )(x1, x2, g, w)


LANES = 128
TQ_PROMPT = 256
TQ_SAMPLE = 128


def _rows8(x):
    return [x[8 * j:8 * j + 8] for j in range(x.shape[0] // 8)]


def _tree(parts, op):
    while len(parts) > 1:
        nxt = [op(parts[2 * j], parts[2 * j + 1]) for j in range(len(parts) // 2)]
        if len(parts) % 2:
            nxt.append(parts[-1])
        parts = nxt
    return parts[0]


def _all8(x, op):
    for shift in (4, 2, 1):
        x = op(x, pltpu.roll(x, shift, 0))
    return x


def _fold(x, op):
    return _all8(_tree(_rows8(x), op), op)


def _per_rows(op, x, v8):
    return jnp.concatenate([op(part, v8) for part in _rows8(x)], axis=0)


def _head_slabs(x, n, dst):
    t = x.shape[0]
    keep = (lax.broadcasted_iota(jnp.int32, (t, LANES), 1) // A_HEAD_DIM) == dst
    parts = []
    for h in range(n):
        slab = x[:, (h // 2) * LANES:(h // 2 + 1) * LANES]
        if h % 2 != dst:
            slab = pltpu.roll(slab, A_HEAD_DIM, 1)
        parts.append(jnp.where(keep, slab, 0.0))
    return parts


def _to_cols(x):
    return jnp.concatenate([x[c:c + LANES, :].T for c in range(0, KEY_TILE, LANES)], axis=1)


def _to_rows(x_t):
    return jnp.concatenate([x_t[:, c:c + LANES].T for c in range(0, KEY_TILE, LANES)], axis=0)


def _fill_keys(kt, j, k, v_t, ik, kb_ref, vt_ref, ikb_ref):
    rows = pl.ds(kt * KEY_TILE, KEY_TILE)
    kb_ref[rows, j * LANES:(j + 1) * LANES] = k.astype(BF16)
    ikb_ref[rows, j * LANES:(j + 1) * LANES] = ik.astype(BF16)
    for g in range(A_KV_HEADS):
        vt_ref[kt, g, j * A_HEAD_DIM:(j + 1) * A_HEAD_DIM, :] = (
            v_t[g * A_HEAD_DIM:(g + 1) * A_HEAD_DIM, :].astype(BF16))


def _attend(qrow, za, kb_ref, vt_ref, ikb_ref, score_ref, lg_ref, p_ref, acc_ref, *, nt, q_pos0, n_keys, topk, nb=1):
    tk = KEY_TILE
    TQ = qrow.shape[0]
    tqb = TQ // nb

    def spread(x):
        if nb == 1:
            return x
        row_set = lax.broadcasted_iota(jnp.int32, x.shape, 0) // tqb
        return jnp.concatenate([jnp.where(row_set == j, x, 0.0) for j in range(nb)], axis=1)

    iq_all = jnp.concatenate([spread(s).astype(BF16) for s in _head_slabs(qrow[:, Q_IQ:Q_IW], IDX_HEADS, 0)],
                             axis=0)
    iw_t = qrow[:, Q_IW:Q_IW + LANES].T * IDX_W_SCALE
    iw_rows = [iw_t[h:h + 1, :] for h in range(IDX_HEADS)]

    k_row = lax.broadcasted_iota(jnp.int32, (tk, TQ), 0)
    q_pos = q_pos0 + lax.broadcasted_iota(jnp.int32, (tk, TQ), 1) % tqb
    q_chunk = q_pos // CHUNK

    def visible(kt):
        k_pos = kt * tk + k_row
        return ((k_pos // CHUNK) <= q_chunk) & (k_pos < n_keys), k_pos

    def score_body(kt, carry):
        mn, mx = carry
        ik_t = ikb_ref[pl.ds(pl.multiple_of(kt * tk, tk), tk), :]
        r = _dot_nt(ik_t, iq_all)
        s = jnp.zeros((tk, TQ), F32)
        for h in range(IDX_HEADS):
            s = s + jnp.maximum(r[:, h * TQ:(h + 1) * TQ], 0.0) * iw_rows[h]
        vis, _ = visible(kt)
        masked = jnp.where(vis, s, NEG_INF)
        score_ref[kt] = masked
        mn = jnp.minimum(mn, _tree(_rows8(jnp.where(vis, s, jnp.inf)), jnp.minimum))
        mx = jnp.maximum(mx, _tree(_rows8(masked), jnp.maximum))
        return mn, mx

    mn, mx = lax.fori_loop(0, nt, score_body,
                           (jnp.full((8, TQ), jnp.inf, F32), jnp.full((8, TQ), NEG_INF, F32)))
    mx = _all8(mx, jnp.maximum)

    q_pos8 = q_pos0 + lax.broadcasted_iota(jnp.int32, (8, TQ), 1) % tqb
    n_vis = jnp.minimum((q_pos8 // CHUNK + 1) * CHUNK, n_keys)
    k_eff = jnp.minimum(n_vis, topk).astype(F32)

    def count_ge(thr):
        def body(kt, acc):
            return acc + _tree([jnp.where(part >= thr, 1.0, 0.0) for part in _rows8(score_ref[kt])], jnp.add)
        return _all8(lax.fori_loop(0, nt, body, jnp.zeros((8, TQ), F32)), jnp.add)

    def band_open(lo, hi, still):
        def body(kt, carry):
            bmn, bmx = carry
            lows, highs = [], []
            for part in _rows8(score_ref[kt]):
                inb = (part >= lo) & (part < hi)
                lows.append(jnp.where(inb, part, jnp.inf))
                highs.append(jnp.where(inb, part, NEG_INF))
            return (jnp.minimum(bmn, _tree(lows, jnp.minimum)), jnp.maximum(bmx, _tree(highs, jnp.maximum)))
        bmn, bmx = lax.fori_loop(0, nt, body,
                                 (jnp.full((8, TQ), jnp.inf, F32), jnp.full((8, TQ), NEG_INF, F32)))
        distinct = _all8(bmn, jnp.minimum) < _all8(bmx, jnp.maximum)
        return jnp.max(jnp.where(still & distinct, 1.0, 0.0)).astype(F32)

    def search_round(state):
        it, _, lo, hi, c_lo, c_hi = state
        for _ in range(SEARCH_STEPS):
            mid = lo + (hi - lo) * 0.5
            inside = (mid > lo) & (mid < hi)
            c = count_ge(mid)
            up = inside & (c >= k_eff)
            dn = inside & (c < k_eff)
            lo = jnp.where(up, mid, lo)
            c_lo = jnp.where(up, c, c_lo)
            hi = jnp.where(dn, mid, hi)
            c_hi = jnp.where(dn, c, c_hi)
        still = c_lo != k_eff
        pending = jnp.max(jnp.where(still, 1.0, 0.0)).astype(F32)
        pending = lax.cond((pending > 0.0) & (it >= TIE_CHECK_FROM), band_open, lambda *_: pending, lo, hi, still)
        return it + 1, pending, lo, hi, c_lo, c_hi

    lo0 = _all8(mn, jnp.minimum)
    hi0 = mx + jnp.maximum(jnp.abs(mx), 1e-30) * (2.0 ** -10)
    pending0 = jnp.max(jnp.where(n_vis.astype(F32) != k_eff, 1.0, 0.0)).astype(F32)
    init = (jnp.int32(0), pending0, lo0, hi0, n_vis.astype(F32), jnp.zeros((8, TQ), F32))
    _, _, lo, hi, _, c_hi = lax.while_loop(
        lambda st: (st[0] < SEARCH_ROUNDS) & (st[1] > 0.0), search_round, init)
    lo_r, hi_r = lo[0:1], hi[0:1]
    need_r = (k_eff - c_hi)[0:1]

    tril = (lax.broadcasted_iota(jnp.int32, (tk, tk), 1)
            <= lax.broadcasted_iota(jnp.int32, (tk, tk), 0)).astype(BF16)

    def select_body(kt, carry):
        s = score_ref[kt]
        inb = (s >= lo_r) & (s < hi_r)
        band = jnp.where(inb, 1.0, 0.0)
        rank = _dot(tril, band.astype(BF16)) + carry
        sel = (s >= hi_r) | (inb & (rank <= need_r))
        _, k_pos = visible(kt)
        dist = jnp.abs(q_pos - k_pos).astype(F32)
        score_ref[kt] = jnp.where(sel, -dist, NEG_INF)
        return carry + jnp.sum(band, axis=0, keepdims=True)

    lax.fori_loop(0, nt, select_body, jnp.zeros((1, TQ), F32))

    slopes = [2.0 ** (-8.0 * (h + 1) / A_HEADS) for h in range(A_HEADS)]
    gw = A_REP * A_HEAD_DIM
    q_gs = []
    for g in range(A_KV_HEADS):
        q_g = jnp.concatenate([spread(s) for s in _head_slabs(qrow[:, g * gw:(g + 1) * gw], A_REP, g)], axis=0)
        q_gs.append((q_g * (A_HEAD_DIM ** -0.5)).astype(BF16))
    acc_ref[...] = jnp.zeros(acc_ref.shape, F32)
    lane_set = (lax.broadcasted_iota(jnp.int32, (A_HEAD_DIM, A_REP * TQ), 1) % TQ) // tqb

    def attn_body(kt, carry):
        ms, ls = carry
        k_t = kb_ref[pl.ds(pl.multiple_of(kt * tk, tk), tk), :]
        nd = score_ref[kt]
        for g in range(A_KV_HEADS):
            lg_ref[g] = _dot_nt(k_t, q_gs[g])
        ms_new, ls_new = [], []
        for g in range(A_KV_HEADS):
            corrs = []
            for r in range(A_REP):
                h = g * A_REP + r
                lgr = lg_ref[g, :, r * TQ:(r + 1) * TQ] + slopes[h] * nd
                m_new = jnp.maximum(ms[h], _fold(lgr, jnp.maximum))
                m_safe = jnp.where(m_new == NEG_INF, 0.0, m_new)
                p = _per_rows(lambda a, m: jnp.exp(a - m), lgr, m_safe)
                corr = jnp.exp(ms[h] - m_safe)
                ls_new.append(ls[h] * corr + _fold(p, jnp.add))
                ms_new.append(m_new)
                corrs.append(corr)
                p_ref[g, :, r * TQ:(r + 1) * TQ] = p.astype(BF16)
            pv = _dot(vt_ref[kt, g], p_ref[g])
            if nb > 1:
                pv = sum(jnp.where(lane_set == j, pv[j * A_HEAD_DIM:(j + 1) * A_HEAD_DIM], 0.0) for j in range(nb))
            acc_ref[g] = _per_rows(jnp.multiply, acc_ref[g], jnp.concatenate(corrs, axis=1)) + pv
        return tuple(ms_new), tuple(ls_new)

    init = (tuple(jnp.full((8, TQ), NEG_INF, F32) for _ in range(A_HEADS)),
            tuple(jnp.zeros((8, TQ), F32) for _ in range(A_HEADS)))
    _, ls = lax.fori_loop(0, nt, attn_body, init)
    pieces = []
    for g in range(A_KV_HEADS):
        o_t = _per_rows(jnp.divide, acc_ref[g], jnp.concatenate(ls[g * A_REP:(g + 1) * A_REP], axis=1))
        for r in range(0, A_REP, 2):
            pair = jnp.concatenate([o_t[:, r * TQ:(r + 1) * TQ], o_t[:, (r + 1) * TQ:(r + 2) * TQ]], axis=0)
            pieces.append(pair.T)
    o_a = jnp.concatenate(pieces, axis=-1)
    return o_a * _silu(za)


def _attn_scratch(n_tiles, TQ, nb):
    return [
        pltpu.VMEM((n_tiles * KEY_TILE, nb * LANES), BF16),
        pltpu.VMEM((n_tiles, A_KV_HEADS, nb * A_HEAD_DIM, KEY_TILE), BF16),
        pltpu.VMEM((n_tiles * KEY_TILE, nb * LANES), BF16),
        pltpu.VMEM((n_tiles, KEY_TILE, TQ), F32),
        pltpu.VMEM((A_KV_HEADS, KEY_TILE, A_REP * TQ), F32),
        pltpu.VMEM((A_KV_HEADS, KEY_TILE, A_REP * TQ), BF16),
        pltpu.VMEM((A_KV_HEADS, A_HEAD_DIM, A_REP * TQ), F32),
    ]


def _attn_prompt_kernel(q_ref, kv_ref, za_ref, o_ref, kt_out, vt_out, ikt_out, kb_ref, vt_ref, ikb_ref, *work,
                        seq, topk):
    i = pl.program_id(1)

    @pl.when(i == 0)
    def _():
        for kt in range(seq // KEY_TILE):
            rows = slice(kt * KEY_TILE, (kt + 1) * KEY_TILE)
            k, ik = kv_ref[rows, 0:LANES], kv_ref[rows, KV_IK:KV_IK + LANES]
            v_t = _to_cols(kv_ref[rows, KV_V:KV_V + LANES])
            _fill_keys(kt, 0, k, v_t, ik, kb_ref, vt_ref, ikb_ref)
            kt_out[:, rows] = _to_cols(k)
            vt_out[:, rows] = v_t
            ikt_out[:, rows] = _to_cols(ik)[0:IDX_DIM, :]

    TQ = TQ_PROMPT
    nt = (i * TQ + TQ + KEY_TILE - 1) // KEY_TILE
    out = _attend(q_ref[...], za_ref[...], kb_ref, vt_ref, ikb_ref, *work,
                  nt=nt, q_pos0=i * TQ, n_keys=seq, topk=topk)
    o_ref[...] = out.astype(BF16)


def _attn_prompt(pq, pkv, za, batch, seq):
    TQ = TQ_PROMPT
    nq = seq // TQ
    topk = min(TOPK_MAX, seq // 4)
    return pl.pallas_call(
        functools.partial(_attn_prompt_kernel, seq=seq, topk=topk),
        grid=(batch, nq),
        in_specs=[
            pl.BlockSpec((TQ, QW), lambda b, i: (b * nq + i, 0)),
            pl.BlockSpec((seq, KVW), lambda b, i: (b, 0)),
            pl.BlockSpec((TQ, A_WIDTH), lambda b, i: (b * nq + i, 0)),
        ],
        out_specs=[
            pl.BlockSpec((TQ, A_WIDTH), lambda b, i: (b * nq + i, 0)),
            pl.BlockSpec((A_KV_WIDTH, seq), lambda b, i: (b, 0)),
            pl.BlockSpec((A_KV_WIDTH, seq), lambda b, i: (b, 0)),
            pl.BlockSpec((IDX_DIM, seq), lambda b, i: (b, 0)),
        ],
        out_shape=[
            jax.ShapeDtypeStruct((batch * seq, A_WIDTH), BF16),
            jax.ShapeDtypeStruct((batch * A_KV_WIDTH, seq), F32),
            jax.ShapeDtypeStruct((batch * A_KV_WIDTH, seq), F32),
            jax.ShapeDtypeStruct((batch * IDX_DIM, seq), F32),
        ],
        scratch_shapes=_attn_scratch(seq // KEY_TILE, TQ, 1),
        compiler_params=_cparams(("parallel", "arbitrary")),
        name="attn_prompt",
    )(pq, pkv, za)


def _attn_sample_kernel(q_ref, kv_ref, za_ref, ck_ref, cv_ref, cik_ref, o_ref,
                        kb_ref, vt_ref, ikb_ref, *work, t, past, topk, nt, nb):
    tk = KEY_TILE
    tail = nt * tk - past
    zeros = lambda n, w: jnp.zeros((n, w), F32)
    pad_rows = lambda x: jnp.concatenate([x, zeros(tail - t, LANES)], axis=0)
    for j in range(nb):
        for kt in range(past // tk):
            cols = slice(kt * tk, (kt + 1) * tk)
            ik_t = jnp.concatenate([cik_ref[j * IDX_DIM:(j + 1) * IDX_DIM, cols], zeros(LANES - IDX_DIM, tk)], axis=0)
            _fill_keys(kt, j, _to_rows(ck_ref[j * LANES:(j + 1) * LANES, cols]),
                       cv_ref[j * LANES:(j + 1) * LANES, cols], _to_rows(ik_t), kb_ref, vt_ref, ikb_ref)
        new = kv_ref[j * t:(j + 1) * t, :]
        _fill_keys(past // tk, j, pad_rows(new[:, 0:LANES]), _to_cols(pad_rows(new[:, KV_V:KV_V + LANES])),
                   pad_rows(new[:, KV_IK:KV_IK + LANES]), kb_ref, vt_ref, ikb_ref)
    out = _attend(q_ref[...], za_ref[...], kb_ref, vt_ref, ikb_ref, *work,
                  nt=nt, q_pos0=past, n_keys=past + t, topk=topk, nb=nb)
    o_ref[...] = out.astype(BF16)


def _attn_sample(pq, pkv, za, ck, cv, cik, layer, row0, dec_batch, t):
    past = ck.shape[1]
    nb = TQ_SAMPLE // t
    cblk0 = layer * (dec_batch // nb)
    assert past % KEY_TILE == 0 and t <= KEY_TILE and TQ_SAMPLE % t == 0 and dec_batch % nb == 0
    topk = min(TOPK_MAX, (past + t) // 4)
    nt = past // KEY_TILE + 1
    blk0 = row0 // TQ_SAMPLE
    return pl.pallas_call(
        functools.partial(_attn_sample_kernel, t=t, past=past, topk=topk, nt=nt, nb=nb),
        grid=(dec_batch // nb,),
        in_specs=[
            pl.BlockSpec((TQ_SAMPLE, QW), lambda b: (blk0 + b, 0)),
            pl.BlockSpec((TQ_SAMPLE, KVW), lambda b: (blk0 + b, 0)),
            pl.BlockSpec((TQ_SAMPLE, A_WIDTH), lambda b: (blk0 + b, 0)),
            pl.BlockSpec((nb * A_KV_WIDTH, past), lambda b: (cblk0 + b, 0)),
            pl.BlockSpec((nb * A_KV_WIDTH, past), lambda b: (cblk0 + b, 0)),
            pl.BlockSpec((nb * IDX_DIM, past), lambda b: (cblk0 + b, 0)),
        ],
        out_specs=pl.BlockSpec((TQ_SAMPLE, A_WIDTH), lambda b: (b, 0)),
        out_shape=jax.ShapeDtypeStruct((dec_batch * t, A_WIDTH), BF16),
        scratch_shapes=_attn_scratch(nt, TQ_SAMPLE, nb),
        compiler_params=_cparams(("parallel",)),
        name="attn_sample",
    )(pq, pkv, za, ck, cv, cik)


GMLP_ROWS = 512


def _gmlp_kernel(uv_ref, zb_ref, g_ref, b_ref, w_ref, bs_ref, o_ref, *v_out, width, mask_chunks):
    gw = width // B_GROUPS
    ws = []
    for g in range(B_GROUPS):
        w = w_ref[g]
        if mask_chunks:
            i = lax.broadcasted_iota(jnp.int32, (B_CHUNK, B_CHUNK), 0)
            j = lax.broadcasted_iota(jnp.int32, (B_CHUNK, B_CHUNK), 1)
            w = jnp.where((j // CHUNK) <= (i // CHUNK), w, 0.0)
        ws.append(w.astype(BF16))
    for c in range(uv_ref.shape[0] // B_CHUNK):
        rows = slice(c * B_CHUNK, (c + 1) * B_CHUNK)
        uv = uv_ref[rows, :]
        act = 0.5 * uv * (1.0 + lax.erf(uv * np.float32(1.0 / np.sqrt(2.0))))
        u = act[:, :width]
        v = _ln(act[:, width:], g_ref[...], b_ref[...])
        if v_out:
            v_out[0][rows, :] = v
        vb = v.astype(BF16)
        zs = _silu(zb_ref[rows, :])
        for g in range(B_GROUPS):
            cols = slice(g * gw, (g + 1) * gw)
            mixed = _dot(ws[g], vb[:, cols]) + bs_ref[g]
            o_ref[rows, cols] = (u[:, cols] * mixed * zs[:, cols]).astype(BF16)


def _gmlp(uv, zb, ln_g, ln_b, w, bs, row0, rows, mask_chunks, want_v):
    width = zb.shape[1]
    t = GMLP_ROWS
    assert row0 % t == 0 and rows % t == 0
    blk0 = row0 // t
    out_specs = [pl.BlockSpec((t, width), lambda i: (i, 0))]
    out_shape = [jax.ShapeDtypeStruct((rows, width), BF16)]
    if want_v:
        out_specs.append(pl.BlockSpec((t, width), lambda i: (i, 0)))
        out_shape.append(jax.ShapeDtypeStruct((rows, width), F32))
    return pl.pallas_call(
        functools.partial(_gmlp_kernel, width=width, mask_chunks=mask_chunks),
        grid=(rows // t,),
        in_specs=[
            pl.BlockSpec((t, 2 * width), lambda i: (blk0 + i, 0)),
            pl.BlockSpec((t, width), lambda i: (blk0 + i, 0)),
            pl.BlockSpec((1, width), lambda i: (0, 0)),
            pl.BlockSpec((1, width), lambda i: (0, 0)),
            pl.BlockSpec((B_GROUPS, B_CHUNK, B_CHUNK), lambda i: (0, 0, 0)),
            pl.BlockSpec((B_GROUPS, B_CHUNK, 1), lambda i: (0, 0, 0)),
        ],
        out_specs=out_specs,
        out_shape=out_shape,
        compiler_params=_cparams(("parallel",)),
        name="gmlp_sample" if want_v else "gmlp_prompt",
    )(uv, zb, ln_g, ln_b, w, bs)


CONV_PAD = 32


CONV_ROWS = 64
SUBLANES = 8


def _conv_tile(glu_ref, zc_ref, o_ref, ext_ref, y_ref, w_ref, b_ref, g_ref, beta_ref, t, width):
    ext_ref[0, CONV_PAD:CONV_PAD + t, :] = glu_ref[:, :width] * jax.nn.sigmoid(glu_ref[:, width:])
    n = t + CONV_PAD - SUBLANES
    for s in range(1, SUBLANES):
        ext_ref[s, 0:n, :] = ext_ref[0, s:s + n, :]
    base = CONV_PAD - (C_CONV - 1)

    rc = min(CONV_ROWS, t)

    def rows_step(ci, carry):
        r0 = pl.multiple_of(ci * rc, rc)
        y = jnp.zeros((rc, width), F32)
        for k in range(C_CONV):
            s, q = (base + k) % SUBLANES, (base + k) // SUBLANES
            tap = w_ref[SUBLANES * k:SUBLANES * (k + 1), :]
            y = y + _per_rows(jnp.multiply, ext_ref[s, pl.ds(r0 + SUBLANES * q, rc), :], tap)
        y_ref[pl.ds(r0, rc), :] = y
        return carry

    lax.fori_loop(0, t // rc, rows_step, 0)
    y = y_ref[...] + b_ref[...]
    o_ref[...] = (_silu(_ln(y, g_ref[...], beta_ref[...])) * _silu(zc_ref[...])).astype(BF16)


def _conv_prompt_kernel(glu_ref, zc_ref, w_ref, b_ref, g_ref, beta_ref, o_ref, tail_ref, ext_ref, y_ref,
                        *, t, width):
    i = pl.program_id(1)

    @pl.when(i == 0)
    def _():
        ext_ref[0, 0:CONV_PAD, :] = jnp.zeros((CONV_PAD, width), F32)

    _conv_tile(glu_ref, zc_ref, o_ref, ext_ref, y_ref, w_ref, b_ref, g_ref, beta_ref, t, width)
    tail = ext_ref[0, t:t + CONV_PAD, :]
    ext_ref[0, 0:CONV_PAD, :] = tail

    @pl.when(i == pl.num_programs(1) - 1)
    def _():
        tail_ref[...] = tail[CONV_PAD - (C_CONV - 1):, :]


def _conv_prompt(glu, zc, w, b, ln_g, ln_b, batch, seq):
    t = ROW_TILE
    width = zc.shape[1]
    nb = seq // t
    return pl.pallas_call(
        functools.partial(_conv_prompt_kernel, t=t, width=width),
        grid=(batch, nb),
        in_specs=[
            pl.BlockSpec((t, 2 * width), lambda b_, i: (b_ * nb + i, 0)),
            pl.BlockSpec((t, width), lambda b_, i: (b_ * nb + i, 0)),
            pl.BlockSpec((SUBLANES * C_CONV, width), lambda b_, i: (0, 0)),
            pl.BlockSpec((1, width), lambda b_, i: (0, 0)),
            pl.BlockSpec((1, width), lambda b_, i: (0, 0)),
            pl.BlockSpec((1, width), lambda b_, i: (0, 0)),
        ],
        out_specs=[
            pl.BlockSpec((t, width), lambda b_, i: (b_ * nb + i, 0)),
            pl.BlockSpec((None, C_CONV - 1, width), lambda b_, i: (b_, 0, 0)),
        ],
        out_shape=[
            jax.ShapeDtypeStruct((batch * seq, width), BF16),
            jax.ShapeDtypeStruct((batch, C_CONV - 1, width), F32),
        ],
        scratch_shapes=[pltpu.VMEM((SUBLANES, CONV_PAD + t, width), F32), pltpu.VMEM((t, width), F32)],
        compiler_params=_cparams(("parallel", "arbitrary")),
        name="conv_prompt",
    )(glu, zc, w, b, ln_g, ln_b)


def _conv_sample_kernel(glu_ref, zc_ref, st_ref, w_ref, b_ref, g_ref, beta_ref, o_ref, tail_ref, ext_ref, y_ref,
                        *, t, width):
    base = CONV_PAD - (C_CONV - 1)
    ext_ref[0, 0:base, :] = jnp.zeros((base, width), F32)
    ext_ref[0, base:CONV_PAD, :] = st_ref[...]
    _conv_tile(glu_ref, zc_ref, o_ref, ext_ref, y_ref, w_ref, b_ref, g_ref, beta_ref, t, width)
    tail_ref[...] = ext_ref[0, t + base:t + CONV_PAD, :]


def _conv_sample(glu, zc, state, w, b, ln_g, ln_b, row0, dec_batch, t):
    width = zc.shape[1]
    blk0 = row0 // t
    return pl.pallas_call(
        functools.partial(_conv_sample_kernel, t=t, width=width),
        grid=(dec_batch,),
        in_specs=[
            pl.BlockSpec((t, 2 * width), lambda b_: (blk0 + b_, 0)),
            pl.BlockSpec((t, width), lambda b_: (blk0 + b_, 0)),
            pl.BlockSpec((None, C_CONV - 1, width), lambda b_: (b_, 0, 0)),
            pl.BlockSpec((SUBLANES * C_CONV, width), lambda b_: (0, 0)),
            pl.BlockSpec((1, width), lambda b_: (0, 0)),
            pl.BlockSpec((1, width), lambda b_: (0, 0)),
            pl.BlockSpec((1, width), lambda b_: (0, 0)),
        ],
        out_specs=[
            pl.BlockSpec((t, width), lambda b_: (b_, 0)),
            pl.BlockSpec((None, C_CONV - 1, width), lambda b_: (b_, 0, 0)),
        ],
        out_shape=[
            jax.ShapeDtypeStruct((dec_batch * t, width), BF16),
            jax.ShapeDtypeStruct((dec_batch, C_CONV - 1, width), F32),
        ],
        scratch_shapes=[pltpu.VMEM((SUBLANES, CONV_PAD + t, width), F32), pltpu.VMEM((t, width), F32)],
        compiler_params=_cparams(("parallel",)),
        name="conv_sample",
    )(glu, zc, state, w, b, ln_g, ln_b)


def _merge_kernel(a1, a2, b1, b2, c1, c2, x1, x2, p1, p2, gates_ref, wa_ref, wb_ref, wc_ref, wo_ref,
                  pg_ref, wpg_ref, wple_ref, fg_ref, *o_refs, d, final, n_first):
    pick = functools.partial(_pick, n_first)
    merged = (jax.nn.sigmoid(gates_ref[:, 0:d]) * _dot(pick(a1, a2), wa_ref[...])
              + jax.nn.sigmoid(gates_ref[:, d:2 * d]) * _dot(pick(b1, b2), wb_ref[...])
              + jax.nn.sigmoid(gates_ref[:, 2 * d:3 * d]) * _dot(pick(c1, c2), wc_ref[...]))
    x = pick(x1, x2) + _dot(merged.astype(BF16), wo_ref[...])
    gate = jax.nn.sigmoid(_dot(_rms(x, pg_ref[...]).astype(BF16), wpg_ref[...]))
    x = x + gate * _dot(pick(p1, p2).astype(BF16), wple_ref[...])
    if not final:
        o_refs[0][...] = x
        return
    y = _rms(x, fg_ref[...])
    on_first = pl.program_id(0) < n_first

    @pl.when(on_first)
    def _():
        o_refs[0][...] = y

    @pl.when(jnp.logical_not(on_first))
    def _():
        o_refs[1][...] = y


def _merge(a, b, c, x_pair, p_quad, gates, wa, wb, wc, wo, pg, wpg, wple, fg, final):
    x1, x2, n_first, x_off = x_pair
    p = p_quad[:2]
    m, d = gates.shape[0], x1.shape[1]
    t = ROW_TILE
    pair = lambda arrs, off=0, off1=0: _split_specs(t, arrs[0].shape[1], n_first, off, off1)
    full = lambda arr: pl.BlockSpec(arr.shape, lambda i: (0, 0), pipeline_mode=pl.Buffered(1))
    if final:
        out_specs = _split_specs(t, d, n_first, 0)
        out_shape = [jax.ShapeDtypeStruct((n_first * t, d), F32), jax.ShapeDtypeStruct((m - n_first * t, d), F32)]
    else:
        out_specs = [pl.BlockSpec((t, d), lambda i: (i, 0))]
        out_shape = [jax.ShapeDtypeStruct((m, d), F32)]
    return pl.pallas_call(
        functools.partial(_merge_kernel, d=d, final=final, n_first=n_first),
        grid=(m // t,),
        in_specs=(pair(a) + pair(b) + pair(c) + pair((x1, x2), x_off) + pair(p, p_quad[3], p_quad[2])
                  + [pl.BlockSpec((t, 3 * d), lambda i: (i, 0))]
                  + [full(wa), full(wb), full(wc), full(wo), full(pg), full(wpg), full(wple), full(fg)]),
        out_specs=out_specs,
        out_shape=out_shape,
        compiler_params=_cparams(("arbitrary",)),
        name="merge",
    )(*a, *b, *c, x1, x2, *p, gates, wa, wb, wc, wo, pg, wpg, wple, fg)


def kernel(x_prompt, x_sample, cache_k, cache_v, cache_idx_k, state_conv, p_prompt, p_sample, norm_g, w_in, gmlp_ln_g, gmlp_ln_b, gmlp_ws, gmlp_bs, conv_w, conv_b, conv_ln_g, conv_ln_b, w_branch_a, w_branch_b, w_branch_c, w_out, ple_norm_g, w_ple_gate, w_ple, final_norm_g):
    batch, seq, d = x_prompt.shape
    dec_batch, dec_seq, _ = x_sample.shape
    depth = w_in.shape[0]
    past = cache_k.shape[2]
    bw = gmlp_ln_g.shape[1]
    cw = conv_b.shape[1]
    mp = batch * seq
    ms = dec_batch * dec_seq
    assert mp % ROW_TILE == 0 and ms % ROW_TILE == 0 and seq % ROW_TILE == 0
    assert dec_seq <= CHUNK and B_CHUNK % dec_seq == 0 and past % CHUNK == 0

    widths = (A_WIDTH, A_KV_WIDTH, A_KV_WIDTH, IDX_HEADS * IDX_DIM, IDX_DIM, IDX_HEADS, A_WIDTH,
              2 * bw, bw, 2 * cw, cw, N_BRANCH * d)
    cuts = np.concatenate([[0], np.cumsum(widths)])
    out_widths = (QW, KVW, A_WIDTH, 2 * bw, bw, 2 * cw, cw, N_BRANCH * d)

    tail0 = QW + KVW
    placed = [(0, 0), (3, Q_IQ), (5, Q_IW), (1, QW), (2, QW + KV_V), (4, QW + KV_IK),
              (6, tail0)]
    pieces = [(int(cuts[j]), int(widths[j]), dst) for j, dst in placed[:-1]]
    pieces.append((int(cuts[6]), int(cuts[12] - cuts[6]), tail0))
    w1 = _wprep(jnp.transpose(w_in, (2, 0, 1)), tuple(pieces), sum(out_widths))

    n_first = mp // ROW_TILE
    x_pair = (x_prompt.reshape(mp, d), x_sample.reshape(ms, d), n_first, 0)
    rep = B_CHUNK // dec_seq
    eye = jnp.eye(rep, dtype=F32)
    p_all_prompt = p_prompt.reshape(depth * mp, -1)
    p_all_sample = p_sample.reshape(depth * ms, -1)
    ck_t = jnp.transpose(cache_k, (0, 1, 3, 4, 2)).reshape(depth * dec_batch * A_KV_WIDTH, past)
    cv_t = jnp.transpose(cache_v, (0, 1, 3, 4, 2)).reshape(depth * dec_batch * A_KV_WIDTH, past)
    cik_t = jnp.transpose(cache_idx_k, (0, 1, 3, 2)).reshape(depth * dec_batch * IDX_DIM, past)

    kp, vp, ikp, cvp, ks, vs, iks, cvs, gvs = ([] for _ in range(9))
    for l in range(depth):
        pq, pkv, za, uv, zb, glu, zc, gates = _proj(x_pair, mp + ms, norm_g[l][None, :], w1, l, out_widths)

        a_p, k_t, v_t, ik_t = _attn_prompt(pq, pkv, za, batch, seq)
        a_s = _attn_sample(pq, pkv, za, ck_t, cv_t, cik_t, l, mp, dec_batch, dec_seq)

        ln_g, ln_b = gmlp_ln_g[l][None, :], gmlp_ln_b[l][None, :]
        (b_p,) = _gmlp(uv, zb, ln_g, ln_b, gmlp_ws[l], gmlp_bs[l][:, :, None], 0, mp, True, False)
        ws_s = jnp.einsum('ab,gij->gaibj', eye, gmlp_ws[l][:, :dec_seq, :dec_seq]).reshape(
            B_GROUPS, B_CHUNK, B_CHUNK)
        bs_s = jnp.tile(gmlp_bs[l][:, :dec_seq], (1, rep))[:, :, None]
        b_s, gv = _gmlp(uv, zb, ln_g, ln_b, ws_s, bs_s, mp, ms, False, True)

        cargs = (jnp.repeat(conv_w[l], SUBLANES, axis=0), conv_b[l][None, :],
                 conv_ln_g[l][None, :], conv_ln_b[l][None, :])
        c_p, tail_p = _conv_prompt(glu, zc, *cargs, batch, seq)
        c_s, tail_s = _conv_sample(glu, zc, state_conv[l], *cargs, mp, dec_batch, dec_seq)

        p = (p_all_prompt, p_all_sample, l * n_first, l * (ms // ROW_TILE))
        outs = _merge((a_p, a_s), (b_p, b_s), (c_p, c_s), x_pair, p, gates,
                      w_branch_a[l].astype(BF16), w_branch_b[l].astype(BF16), w_branch_c[l].astype(BF16),
                      w_out[l].astype(BF16), ple_norm_g[l][None, :], w_ple_gate[l].astype(BF16),
                      w_ple[l].astype(BF16), final_norm_g[None, :], l == depth - 1)
        x_pair = (outs[0], outs[0], n_first, n_first)

        heads_last = lambda x_t: jnp.transpose(x_t.reshape(batch, A_KV_HEADS, A_HEAD_DIM, seq), (0, 3, 1, 2))
        kp.append(heads_last(k_t))
        vp.append(heads_last(v_t))
        ikp.append(jnp.transpose(ik_t.reshape(batch, IDX_DIM, seq), (0, 2, 1)))
        cvp.append(tail_p)
        ks.append(pkv[mp:, 0:A_KV_WIDTH].reshape(dec_batch, dec_seq, A_KV_HEADS, A_HEAD_DIM))
        vs.append(pkv[mp:, KV_V:KV_V + A_KV_WIDTH].reshape(dec_batch, dec_seq, A_KV_HEADS, A_HEAD_DIM))
        iks.append(pkv[mp:, KV_IK:KV_IK + IDX_DIM].reshape(dec_batch, dec_seq, IDX_DIM))
        cvs.append(tail_s)
        gvs.append(gv.reshape(dec_batch, dec_seq, bw))

    return (outs[0].reshape(batch, seq, d), outs[1].reshape(dec_batch, dec_seq, d),
            jnp.stack(kp), jnp.stack(vp), jnp.stack(ikp), jnp.stack(cvp),
            jnp.stack(ks), jnp.stack(vs), jnp.stack(iks), jnp.stack(cvs), jnp.stack(gvs))
```

```python
import functools

import jax
import jax.numpy as jnp
import numpy as np
from jax import lax
from jax.experimental import pallas as pl
from jax.experimental.pallas import tpu as pltpu

F32 = jnp.float32
BF16 = jnp.bfloat16

CHUNK = 64
EPS = 1e-6
A_HEAD_DIM = 64
A_HEADS = 8
A_KV_HEADS = 2
A_REP = A_HEADS // A_KV_HEADS
A_WIDTH = A_HEADS * A_HEAD_DIM
A_KV_WIDTH = A_KV_HEADS * A_HEAD_DIM
IDX_HEADS = 4
IDX_DIM = 64
IDX_W_SCALE = (IDX_HEADS * IDX_DIM) ** -0.5
TOPK_MAX = 256
B_GROUPS = 4
B_CHUNK = 128
C_CONV = 31
N_BRANCH = 3

QW = 896
KVW = 384
Q_IQ = A_WIDTH
Q_IW = A_WIDTH + IDX_HEADS * IDX_DIM
KV_V = A_KV_WIDTH
KV_IK = 2 * A_KV_WIDTH

VMEM_LIMIT = 56 * 1024 * 1024
ROW_TILE = 256
KEY_TILE = 256
SEARCH_STEPS = 4
SEARCH_ROUNDS = 96
TIE_CHECK_FROM = 4
NEG_INF = float("-inf")


def _cparams(sem):
    return pltpu.CompilerParams(dimension_semantics=sem, vmem_limit_bytes=VMEM_LIMIT)


def _rms(xf, g):
    return xf * lax.rsqrt(jnp.mean(jnp.square(xf), axis=-1, keepdims=True) + EPS) * g


def _ln(xf, g, b):
    xc = xf - jnp.mean(xf, axis=-1, keepdims=True)
    var = jnp.mean(jnp.square(xc), axis=-1, keepdims=True)
    return xc * lax.rsqrt(var + EPS) * g + b


def _silu(x):
    return x * jax.nn.sigmoid(x)


def _dot(a, b):
    return jnp.dot(a, b, preferred_element_type=F32)


def _dot_nt(a, b):
    return lax.dot_general(a, b, (((1,), (1,)), ((), ())), preferred_element_type=F32)


WPREP_ROWS = 128


def _wprep_kernel(src_ref, cnt_ref, w_ref, o_ref):
    i = pl.program_id(0)
    keep = lax.broadcasted_iota(jnp.int32, (WPREP_ROWS, w_ref.shape[2]), 0) < cnt_ref[i]
    for l in range(w_ref.shape[1]):
        o_ref[l] = jnp.where(keep, w_ref[:, l, :], 0.0).astype(BF16)


def _wprep(w_t, pieces, n_out):
    n, depth, d = w_t.shape
    src, cnt = [], []
    for dst0 in range(0, n_out, WPREP_ROWS):
        hit = [(s + dst0 - t, min(WPREP_ROWS, t + wd - dst0)) for s, wd, t in pieces if t <= dst0 < t + wd]
        assert len(hit) == 1 and hit[0][0] + WPREP_ROWS <= n
        src.append(hit[0][0])
        cnt.append(hit[0][1])
    grid_spec = pltpu.PrefetchScalarGridSpec(
        num_scalar_prefetch=2,
        grid=(n_out // WPREP_ROWS,),
        in_specs=[pl.BlockSpec((pl.Element(WPREP_ROWS), pl.Element(depth), pl.Element(d)),
                               lambda i, src_ref, cnt_ref: (src_ref[i], 0, 0))],
        out_specs=pl.BlockSpec((depth, WPREP_ROWS, d), lambda i, src_ref, cnt_ref: (0, i, 0)),
    )
    return pl.pallas_call(
        _wprep_kernel,
        grid_spec=grid_spec,
        out_shape=jax.ShapeDtypeStruct((depth, n_out, d), BF16),
        compiler_params=_cparams(("parallel",)),
        name="wprep",
    )(jnp.asarray(src, jnp.int32), jnp.asarray(cnt, jnp.int32), w_t)


def _split_specs(t, width, n_first, second_off, first_off=0):
    first = pl.BlockSpec((t, width), lambda i: (jnp.minimum(i, n_first - 1) + first_off, 0))
    second = pl.BlockSpec((t, width), lambda i: (jnp.maximum(i - n_first, 0) + second_off, 0))
    return [first, second]


def _pick(n_first, first_ref, second_ref):
    return jnp.where(pl.program_id(0) < n_first, first_ref[...], second_ref[...])


def _proj_kernel(x1_ref, x2_ref, g_ref, w_ref, *out_refs, widths, n_first):
    h = _rms(_pick(n_first, x1_ref, x2_ref), g_ref[...]).astype(BF16)
    off = 0
    for o_ref, wd in zip(out_refs, widths):
        for c in range(0, wd, 512):
            cw = min(512, wd - c)
            o_ref[:, c:c + cw] = _dot_nt(h, w_ref[off + c:off + c + cw, :]).astype(o_ref.dtype)
        off += wd


def _proj(x_pair, m, g, w, layer, widths, dtypes):
    x1, x2, n_first, second_off = x_pair
    d = x1.shape[1]
    n = w.shape[1]
    return pl.pallas_call(
        functools.partial(_proj_kernel, widths=widths, n_first=n_first),
        grid=(m // ROW_TILE,),
        in_specs=_split_specs(ROW_TILE, d, n_first, second_off) + [
            pl.BlockSpec((1, d), lambda i: (0, 0)),
            pl.BlockSpec((None, n, d), lambda i: (layer, 0, 0), pipeline_mode=pl.Buffered(1)),
        ],
        out_specs=[pl.BlockSpec((ROW_TILE, wd), lambda i: (i, 0)) for wd in widths],
        out_shape=[jax.ShapeDtypeStruct((m, wd), dt) for wd, dt in zip(widths, dtypes)],
        compiler_params=_cparams(("parallel",)),
        name="proj",
    )(x1, x2, g, w)


LANES = 128
TQ_PROMPT = 256
TQ_SAMPLE = 128


def _rows8(x):
    return [x[8 * j:8 * j + 8] for j in range(x.shape[0] // 8)]


def _tree(parts, op):
    while len(parts) > 1:
        nxt = [op(parts[2 * j], parts[2 * j + 1]) for j in range(len(parts) // 2)]
        if len(parts) % 2:
            nxt.append(parts[-1])
        parts = nxt
    return parts[0]


def _all8(x, op):
    for shift in (4, 2, 1):
        x = op(x, pltpu.roll(x, shift, 0))
    return x


def _fold(x, op):
    return _all8(_tree(_rows8(x), op), op)


def _per_rows(op, x, v8):
    return jnp.concatenate([op(part, v8) for part in _rows8(x)], axis=0)


def _head_slabs(x, n, dst):
    t = x.shape[0]
    keep = (lax.broadcasted_iota(jnp.int32, (t, LANES), 1) // A_HEAD_DIM) == dst
    parts = []
    for h in range(n):
        slab = x[:, (h // 2) * LANES:(h // 2 + 1) * LANES]
        if h % 2 != dst:
            slab = pltpu.roll(slab, A_HEAD_DIM, 1)
        parts.append(jnp.where(keep, slab, 0.0))
    return parts


def _to_cols(x):
    return jnp.concatenate([x[c:c + LANES, :].T for c in range(0, KEY_TILE, LANES)], axis=1)


def _to_rows(x_t):
    return jnp.concatenate([x_t[:, c:c + LANES].T for c in range(0, KEY_TILE, LANES)], axis=0)


def _fill_keys(kt, j, k, v_t, ik, kb_ref, vt_ref, ikb_ref):
    rows = pl.ds(kt * KEY_TILE, KEY_TILE)
    kb_ref[rows, j * LANES:(j + 1) * LANES] = k.astype(BF16)
    ikb_ref[rows, j * LANES:(j + 1) * LANES] = ik.astype(BF16)
    for g in range(A_KV_HEADS):
        vt_ref[kt, g, j * A_HEAD_DIM:(j + 1) * A_HEAD_DIM, :] = (
            v_t[g * A_HEAD_DIM:(g + 1) * A_HEAD_DIM, :].astype(BF16))


def _attend(qrow, za, kb_ref, vt_ref, ikb_ref, score_ref, lg_ref, p_ref, acc_ref, *, nt, q_pos0, n_keys, topk, nb=1):
    tk = KEY_TILE
    TQ = qrow.shape[0]
    tqb = TQ // nb

    def spread(x):
        if nb == 1:
            return x
        row_set = lax.broadcasted_iota(jnp.int32, x.shape, 0) // tqb
        return jnp.concatenate([jnp.where(row_set == j, x, 0.0) for j in range(nb)], axis=1)

    iq_all = jnp.concatenate([spread(s).astype(BF16) for s in _head_slabs(qrow[:, Q_IQ:Q_IW], IDX_HEADS, 0)],
                             axis=0)
    iw_t = qrow[:, Q_IW:Q_IW + LANES].T * IDX_W_SCALE
    iw_rows = [iw_t[h:h + 1, :] for h in range(IDX_HEADS)]

    k_row = lax.broadcasted_iota(jnp.int32, (tk, TQ), 0)
    q_pos = q_pos0 + lax.broadcasted_iota(jnp.int32, (tk, TQ), 1) % tqb
    q_chunk = q_pos // CHUNK

    def visible(kt):
        k_pos = kt * tk + k_row
        return ((k_pos // CHUNK) <= q_chunk) & (k_pos < n_keys), k_pos

    def score_body(kt, carry):
        mn, mx = carry
        ik_t = ikb_ref[pl.ds(pl.multiple_of(kt * tk, tk), tk), :]
        r = _dot_nt(ik_t, iq_all)
        s = jnp.zeros((tk, TQ), F32)
        for h in range(IDX_HEADS):
            s = s + jnp.maximum(r[:, h * TQ:(h + 1) * TQ], 0.0) * iw_rows[h]
        vis, _ = visible(kt)
        masked = jnp.where(vis, s, NEG_INF)
        score_ref[kt] = masked
        mn = jnp.minimum(mn, _tree(_rows8(jnp.where(vis, s, jnp.inf)), jnp.minimum))
        mx = jnp.maximum(mx, _tree(_rows8(masked), jnp.maximum))
        return mn, mx

    mn, mx = lax.fori_loop(0, nt, score_body,
                           (jnp.full((8, TQ), jnp.inf, F32), jnp.full((8, TQ), NEG_INF, F32)))
    mx = _all8(mx, jnp.maximum)

    q_pos8 = q_pos0 + lax.broadcasted_iota(jnp.int32, (8, TQ), 1) % tqb
    n_vis = jnp.minimum((q_pos8 // CHUNK + 1) * CHUNK, n_keys)
    k_eff = jnp.minimum(n_vis, topk).astype(F32)

    def count_ge(thr):
        def body(kt, acc):
            return acc + _tree([jnp.where(part >= thr, 1.0, 0.0) for part in _rows8(score_ref[kt])], jnp.add)
        return _all8(lax.fori_loop(0, nt, body, jnp.zeros((8, TQ), F32)), jnp.add)

    def band_open(lo, hi, still):
        def body(kt, carry):
            bmn, bmx = carry
            lows, highs = [], []
            for part in _rows8(score_ref[kt]):
                inb = (part >= lo) & (part < hi)
                lows.append(jnp.where(inb, part, jnp.inf))
                highs.append(jnp.where(inb, part, NEG_INF))
            return (jnp.minimum(bmn, _tree(lows, jnp.minimum)), jnp.maximum(bmx, _tree(highs, jnp.maximum)))
        bmn, bmx = lax.fori_loop(0, nt, body,
                                 (jnp.full((8, TQ), jnp.inf, F32), jnp.full((8, TQ), NEG_INF, F32)))
        distinct = _all8(bmn, jnp.minimum) < _all8(bmx, jnp.maximum)
        return jnp.max(jnp.where(still & distinct, 1.0, 0.0)).astype(F32)

    def search_round(state):
        it, _, lo, hi, c_lo, c_hi = state
        for _ in range(SEARCH_STEPS):
            mid = lo + (hi - lo) * 0.5
            inside = (mid > lo) & (mid < hi)
            c = count_ge(mid)
            up = inside & (c >= k_eff)
            dn = inside & (c < k_eff)
            lo = jnp.where(up, mid, lo)
            c_lo = jnp.where(up, c, c_lo)
            hi = jnp.where(dn, mid, hi)
            c_hi = jnp.where(dn, c, c_hi)
        still = c_lo != k_eff
        pending = jnp.max(jnp.where(still, 1.0, 0.0)).astype(F32)
        pending = lax.cond((pending > 0.0) & (it >= TIE_CHECK_FROM), band_open, lambda *_: pending, lo, hi, still)
        return it + 1, pending, lo, hi, c_lo, c_hi

    lo0 = _all8(mn, jnp.minimum)
    hi0 = mx + jnp.maximum(jnp.abs(mx), 1e-30) * (2.0 ** -10)
    pending0 = jnp.max(jnp.where(n_vis.astype(F32) != k_eff, 1.0, 0.0)).astype(F32)
    init = (jnp.int32(0), pending0, lo0, hi0, n_vis.astype(F32), jnp.zeros((8, TQ), F32))
    _, _, lo, hi, _, c_hi = lax.while_loop(
        lambda st: (st[0] < SEARCH_ROUNDS) & (st[1] > 0.0), search_round, init)
    lo_r, hi_r = lo[0:1], hi[0:1]
    need_r = (k_eff - c_hi)[0:1]

    tril = (lax.broadcasted_iota(jnp.int32, (tk, tk), 1)
            <= lax.broadcasted_iota(jnp.int32, (tk, tk), 0)).astype(BF16)

    def select_body(kt, carry):
        s = score_ref[kt]
        inb = (s >= lo_r) & (s < hi_r)
        band = jnp.where(inb, 1.0, 0.0)
        rank = _dot(tril, band.astype(BF16)) + carry
        sel = (s >= hi_r) | (inb & (rank <= need_r))
        _, k_pos = visible(kt)
        dist = jnp.abs(q_pos - k_pos).astype(F32)
        score_ref[kt] = jnp.where(sel, -dist, NEG_INF)
        return carry + jnp.sum(band, axis=0, keepdims=True)

    lax.fori_loop(0, nt, select_body, jnp.zeros((1, TQ), F32))

    slopes = [2.0 ** (-8.0 * (h + 1) / A_HEADS) for h in range(A_HEADS)]
    gw = A_REP * A_HEAD_DIM
    q_gs = []
    for g in range(A_KV_HEADS):
        q_g = jnp.concatenate([spread(s) for s in _head_slabs(qrow[:, g * gw:(g + 1) * gw], A_REP, g)], axis=0)
        q_gs.append((q_g * (A_HEAD_DIM ** -0.5)).astype(BF16))
    acc_ref[...] = jnp.zeros(acc_ref.shape, F32)
    lane_set = (lax.broadcasted_iota(jnp.int32, (A_HEAD_DIM, A_REP * TQ), 1) % TQ) // tqb

    def attn_body(kt, carry):
        ms, ls = carry
        k_t = kb_ref[pl.ds(pl.multiple_of(kt * tk, tk), tk), :]
        nd = score_ref[kt]
        for g in range(A_KV_HEADS):
            lg_ref[g] = _dot_nt(k_t, q_gs[g])
        ms_new, ls_new = [], []
        for g in range(A_KV_HEADS):
            corrs = []
            for r in range(A_REP):
                h = g * A_REP + r
                lgr = lg_ref[g, :, r * TQ:(r + 1) * TQ] + slopes[h] * nd
                m_new = jnp.maximum(ms[h], _fold(lgr, jnp.maximum))
                m_safe = jnp.where(m_new == NEG_INF, 0.0, m_new)
                p = _per_rows(lambda a, m: jnp.exp(a - m), lgr, m_safe)
                corr = jnp.exp(ms[h] - m_safe)
                ls_new.append(ls[h] * corr + _fold(p, jnp.add))
                ms_new.append(m_new)
                corrs.append(corr)
                p_ref[g, :, r * TQ:(r + 1) * TQ] = p.astype(BF16)
            pv = _dot(vt_ref[kt, g], p_ref[g])
            if nb > 1:
                pv = sum(jnp.where(lane_set == j, pv[j * A_HEAD_DIM:(j + 1) * A_HEAD_DIM], 0.0) for j in range(nb))
            acc_ref[g] = _per_rows(jnp.multiply, acc_ref[g], jnp.concatenate(corrs, axis=1)) + pv
        return tuple(ms_new), tuple(ls_new)

    init = (tuple(jnp.full((8, TQ), NEG_INF, F32) for _ in range(A_HEADS)),
            tuple(jnp.zeros((8, TQ), F32) for _ in range(A_HEADS)))
    _, ls = lax.fori_loop(0, nt, attn_body, init)
    pieces = []
    for g in range(A_KV_HEADS):
        o_t = _per_rows(jnp.divide, acc_ref[g], jnp.concatenate(ls[g * A_REP:(g + 1) * A_REP], axis=1))
        for r in range(0, A_REP, 2):
            pair = jnp.concatenate([o_t[:, r * TQ:(r + 1) * TQ], o_t[:, (r + 1) * TQ:(r + 2) * TQ]], axis=0)
            pieces.append(pair.T)
    o_a = jnp.concatenate(pieces, axis=-1)
    return o_a * _silu(za.astype(F32))


def _attn_scratch(n_tiles, TQ, nb):
    return [
        pltpu.VMEM((n_tiles * KEY_TILE, nb * LANES), BF16),
        pltpu.VMEM((n_tiles, A_KV_HEADS, nb * A_HEAD_DIM, KEY_TILE), BF16),
        pltpu.VMEM((n_tiles * KEY_TILE, nb * LANES), BF16),
        pltpu.VMEM((n_tiles, KEY_TILE, TQ), F32),
        pltpu.VMEM((A_KV_HEADS, KEY_TILE, A_REP * TQ), F32),
        pltpu.VMEM((A_KV_HEADS, KEY_TILE, A_REP * TQ), BF16),
        pltpu.VMEM((A_KV_HEADS, A_HEAD_DIM, A_REP * TQ), F32),
    ]


def _attn_prompt_kernel(q_ref, kv_ref, za_ref, o_ref, kt_out, vt_out, ikt_out, kb_ref, vt_ref, ikb_ref, *work,
                        seq, topk):
    i = pl.program_id(1)

    @pl.when(i == 0)
    def _():
        for kt in range(seq // KEY_TILE):
            rows = slice(kt * KEY_TILE, (kt + 1) * KEY_TILE)
            k, ik = kv_ref[rows, 0:LANES], kv_ref[rows, KV_IK:KV_IK + LANES]
            v_t = _to_cols(kv_ref[rows, KV_V:KV_V + LANES])
            _fill_keys(kt, 0, k, v_t, ik, kb_ref, vt_ref, ikb_ref)
            kt_out[:, rows] = _to_cols(k)
            vt_out[:, rows] = v_t
            ikt_out[:, rows] = _to_cols(ik)[0:IDX_DIM, :]

    TQ = TQ_PROMPT
    nt = (i * TQ + TQ + KEY_TILE - 1) // KEY_TILE
    out = _attend(q_ref[...], za_ref[...], kb_ref, vt_ref, ikb_ref, *work,
                  nt=nt, q_pos0=i * TQ, n_keys=seq, topk=topk)
    o_ref[...] = out.astype(BF16)


def _attn_prompt(pq, pkv, za, batch, seq):
    TQ = TQ_PROMPT
    nq = seq // TQ
    topk = min(TOPK_MAX, seq // 4)
    return pl.pallas_call(
        functools.partial(_attn_prompt_kernel, seq=seq, topk=topk),
        grid=(batch, nq),
        in_specs=[
            pl.BlockSpec((TQ, QW), lambda b, i: (b * nq + i, 0)),
            pl.BlockSpec((seq, KVW), lambda b, i: (b, 0)),
            pl.BlockSpec((TQ, A_WIDTH), lambda b, i: (b * nq + i, 0)),
        ],
        out_specs=[
            pl.BlockSpec((TQ, A_WIDTH), lambda b, i: (b * nq + i, 0)),
            pl.BlockSpec((A_KV_WIDTH, seq), lambda b, i: (b, 0)),
            pl.BlockSpec((A_KV_WIDTH, seq), lambda b, i: (b, 0)),
            pl.BlockSpec((IDX_DIM, seq), lambda b, i: (b, 0)),
        ],
        out_shape=[
            jax.ShapeDtypeStruct((batch * seq, A_WIDTH), BF16),
            jax.ShapeDtypeStruct((batch * A_KV_WIDTH, seq), F32),
            jax.ShapeDtypeStruct((batch * A_KV_WIDTH, seq), F32),
            jax.ShapeDtypeStruct((batch * IDX_DIM, seq), F32),
        ],
        scratch_shapes=_attn_scratch(seq // KEY_TILE, TQ, 1),
        compiler_params=_cparams(("parallel", "arbitrary")),
        name="attn_prompt",
    )(pq, pkv, za)


def _attn_sample_kernel(q_ref, kv_ref, za_ref, ck_ref, cv_ref, cik_ref, o_ref,
                        kb_ref, vt_ref, ikb_ref, *work, t, past, topk, nt, nb):
    tk = KEY_TILE
    tail = nt * tk - past
    zeros = lambda n, w: jnp.zeros((n, w), F32)
    pad_rows = lambda x: jnp.concatenate([x, zeros(tail - t, LANES)], axis=0)
    for j in range(nb):
        for kt in range(past // tk):
            cols = slice(kt * tk, (kt + 1) * tk)
            ik_t = jnp.concatenate([cik_ref[j * IDX_DIM:(j + 1) * IDX_DIM, cols], zeros(LANES - IDX_DIM, tk)], axis=0)
            _fill_keys(kt, j, _to_rows(ck_ref[j * LANES:(j + 1) * LANES, cols]),
                       cv_ref[j * LANES:(j + 1) * LANES, cols], _to_rows(ik_t), kb_ref, vt_ref, ikb_ref)
        new = kv_ref[j * t:(j + 1) * t, :]
        _fill_keys(past // tk, j, pad_rows(new[:, 0:LANES]), _to_cols(pad_rows(new[:, KV_V:KV_V + LANES])),
                   pad_rows(new[:, KV_IK:KV_IK + LANES]), kb_ref, vt_ref, ikb_ref)
    out = _attend(q_ref[...], za_ref[...], kb_ref, vt_ref, ikb_ref, *work,
                  nt=nt, q_pos0=past, n_keys=past + t, topk=topk, nb=nb)
    o_ref[...] = out.astype(BF16)


def _attn_sample(pq, pkv, za, ck, cv, cik, layer, row0, dec_batch, t):
    past = ck.shape[1]
    nb = TQ_SAMPLE // t
    cblk0 = layer * (dec_batch // nb)
    assert past % KEY_TILE == 0 and t <= KEY_TILE and TQ_SAMPLE % t == 0 and dec_batch % nb == 0
    topk = min(TOPK_MAX, (past + t) // 4)
    nt = past // KEY_TILE + 1
    blk0 = row0 // TQ_SAMPLE
    return pl.pallas_call(
        functools.partial(_attn_sample_kernel, t=t, past=past, topk=topk, nt=nt, nb=nb),
        grid=(dec_batch // nb,),
        in_specs=[
            pl.BlockSpec((TQ_SAMPLE, QW), lambda b: (blk0 + b, 0)),
            pl.BlockSpec((TQ_SAMPLE, KVW), lambda b: (blk0 + b, 0)),
            pl.BlockSpec((TQ_SAMPLE, A_WIDTH), lambda b: (blk0 + b, 0)),
            pl.BlockSpec((nb * A_KV_WIDTH, past), lambda b: (cblk0 + b, 0)),
            pl.BlockSpec((nb * A_KV_WIDTH, past), lambda b: (cblk0 + b, 0)),
            pl.BlockSpec((nb * IDX_DIM, past), lambda b: (cblk0 + b, 0)),
        ],
        out_specs=pl.BlockSpec((TQ_SAMPLE, A_WIDTH), lambda b: (b, 0)),
        out_shape=jax.ShapeDtypeStruct((dec_batch * t, A_WIDTH), BF16),
        scratch_shapes=_attn_scratch(nt, TQ_SAMPLE, nb),
        compiler_params=_cparams(("parallel",)),
        name="attn_sample",
    )(pq, pkv, za, ck, cv, cik)


GMLP_ROWS = 512


def _gmlp_kernel(uv_ref, zb_ref, g_ref, b_ref, w_ref, bs_ref, o_ref, *v_out, width, mask_chunks):
    gw = width // B_GROUPS
    ws = []
    for g in range(B_GROUPS):
        w = w_ref[g]
        if mask_chunks:
            i = lax.broadcasted_iota(jnp.int32, (B_CHUNK, B_CHUNK), 0)
            j = lax.broadcasted_iota(jnp.int32, (B_CHUNK, B_CHUNK), 1)
            w = jnp.where((j // CHUNK) <= (i // CHUNK), w, 0.0)
        ws.append(w.astype(BF16))
    for c in range(uv_ref.shape[0] // B_CHUNK):
        rows = slice(c * B_CHUNK, (c + 1) * B_CHUNK)
        uv = uv_ref[rows, :]
        act = 0.5 * uv * (1.0 + lax.erf(uv * np.float32(1.0 / np.sqrt(2.0))))
        u = act[:, :width]
        v = _ln(act[:, width:], g_ref[...], b_ref[...])
        if v_out:
            v_out[0][rows, :] = v
        vb = v.astype(BF16)
        zs = _silu(zb_ref[rows, :].astype(F32))
        for g in range(B_GROUPS):
            cols = slice(g * gw, (g + 1) * gw)
            mixed = _dot(ws[g], vb[:, cols]) + bs_ref[g]
            o_ref[rows, cols] = (u[:, cols] * mixed * zs[:, cols]).astype(BF16)


def _gmlp(uv, zb, ln_g, ln_b, w, bs, row0, rows, mask_chunks, want_v):
    width = zb.shape[1]
    t = GMLP_ROWS
    assert row0 % t == 0 and rows % t == 0
    blk0 = row0 // t
    out_specs = [pl.BlockSpec((t, width), lambda i: (i, 0))]
    out_shape = [jax.ShapeDtypeStruct((rows, width), BF16)]
    if want_v:
        out_specs.append(pl.BlockSpec((t, width), lambda i: (i, 0)))
        out_shape.append(jax.ShapeDtypeStruct((rows, width), F32))
    return pl.pallas_call(
        functools.partial(_gmlp_kernel, width=width, mask_chunks=mask_chunks),
        grid=(rows // t,),
        in_specs=[
            pl.BlockSpec((t, 2 * width), lambda i: (blk0 + i, 0)),
            pl.BlockSpec((t, width), lambda i: (blk0 + i, 0)),
            pl.BlockSpec((1, width), lambda i: (0, 0)),
            pl.BlockSpec((1, width), lambda i: (0, 0)),
            pl.BlockSpec((B_GROUPS, B_CHUNK, B_CHUNK), lambda i: (0, 0, 0)),
            pl.BlockSpec((B_GROUPS, B_CHUNK, 1), lambda i: (0, 0, 0)),
        ],
        out_specs=out_specs,
        out_shape=out_shape,
        compiler_params=_cparams(("parallel",)),
        name="gmlp_sample" if want_v else "gmlp_prompt",
    )(uv, zb, ln_g, ln_b, w, bs)


CONV_PAD = 32


CONV_ROWS = 64
SUBLANES = 8


def _conv_tile(glu_ref, zc_ref, o_ref, ext_ref, y_ref, w_ref, b_ref, g_ref, beta_ref, t, width):
    ext_ref[0, CONV_PAD:CONV_PAD + t, :] = glu_ref[:, :width] * jax.nn.sigmoid(glu_ref[:, width:])
    n = t + CONV_PAD - SUBLANES
    for s in range(1, SUBLANES):
        ext_ref[s, 0:n, :] = ext_ref[0, s:s + n, :]
    base = CONV_PAD - (C_CONV - 1)

    rc = min(CONV_ROWS, t)

    def rows_step(ci, carry):
        r0 = pl.multiple_of(ci * rc, rc)
        y = jnp.zeros((rc, width), F32)
        for k in range(C_CONV):
            s, q = (base + k) % SUBLANES, (base + k) // SUBLANES
            tap = w_ref[SUBLANES * k:SUBLANES * (k + 1), :]
            y = y + _per_rows(jnp.multiply, ext_ref[s, pl.ds(r0 + SUBLANES * q, rc), :], tap)
        y_ref[pl.ds(r0, rc), :] = y
        return carry

    lax.fori_loop(0, t // rc, rows_step, 0)
    y = y_ref[...] + b_ref[...]
    o_ref[...] = (_silu(_ln(y, g_ref[...], beta_ref[...])) * _silu(zc_ref[...].astype(F32))).astype(BF16)


def _conv_prompt_kernel(glu_ref, zc_ref, w_ref, b_ref, g_ref, beta_ref, o_ref, tail_ref, ext_ref, y_ref,
                        *, t, width):
    i = pl.program_id(1)

    @pl.when(i == 0)
    def _():
        ext_ref[0, 0:CONV_PAD, :] = jnp.zeros((CONV_PAD, width), F32)

    _conv_tile(glu_ref, zc_ref, o_ref, ext_ref, y_ref, w_ref, b_ref, g_ref, beta_ref, t, width)
    tail = ext_ref[0, t:t + CONV_PAD, :]
    ext_ref[0, 0:CONV_PAD, :] = tail

    @pl.when(i == pl.num_programs(1) - 1)
    def _():
        tail_ref[...] = tail[CONV_PAD - (C_CONV - 1):, :]


def _conv_prompt(glu, zc, w, b, ln_g, ln_b, batch, seq):
    t = ROW_TILE
    width = zc.shape[1]
    nb = seq // t
    return pl.pallas_call(
        functools.partial(_conv_prompt_kernel, t=t, width=width),
        grid=(batch, nb),
        in_specs=[
            pl.BlockSpec((t, 2 * width), lambda b_, i: (b_ * nb + i, 0)),
            pl.BlockSpec((t, width), lambda b_, i: (b_ * nb + i, 0)),
            pl.BlockSpec((SUBLANES * C_CONV, width), lambda b_, i: (0, 0)),
            pl.BlockSpec((1, width), lambda b_, i: (0, 0)),
            pl.BlockSpec((1, width), lambda b_, i: (0, 0)),
            pl.BlockSpec((1, width), lambda b_, i: (0, 0)),
        ],
        out_specs=[
            pl.BlockSpec((t, width), lambda b_, i: (b_ * nb + i, 0)),
            pl.BlockSpec((None, C_CONV - 1, width), lambda b_, i: (b_, 0, 0)),
        ],
        out_shape=[
            jax.ShapeDtypeStruct((batch * seq, width), BF16),
            jax.ShapeDtypeStruct((batch, C_CONV - 1, width), F32),
        ],
        scratch_shapes=[pltpu.VMEM((SUBLANES, CONV_PAD + t, width), F32), pltpu.VMEM((t, width), F32)],
        compiler_params=_cparams(("parallel", "arbitrary")),
        name="conv_prompt",
    )(glu, zc, w, b, ln_g, ln_b)


def _conv_sample_kernel(glu_ref, zc_ref, st_ref, w_ref, b_ref, g_ref, beta_ref, o_ref, tail_ref, ext_ref, y_ref,
                        *, t, width):
    base = CONV_PAD - (C_CONV - 1)
    ext_ref[0, 0:base, :] = jnp.zeros((base, width), F32)
    ext_ref[0, base:CONV_PAD, :] = st_ref[...]
    _conv_tile(glu_ref, zc_ref, o_ref, ext_ref, y_ref, w_ref, b_ref, g_ref, beta_ref, t, width)
    tail_ref[...] = ext_ref[0, t + base:t + CONV_PAD, :]


def _conv_sample(glu, zc, state, w, b, ln_g, ln_b, row0, dec_batch, t):
    width = zc.shape[1]
    blk0 = row0 // t
    return pl.pallas_call(
        functools.partial(_conv_sample_kernel, t=t, width=width),
        grid=(dec_batch,),
        in_specs=[
            pl.BlockSpec((t, 2 * width), lambda b_: (blk0 + b_, 0)),
            pl.BlockSpec((t, width), lambda b_: (blk0 + b_, 0)),
            pl.BlockSpec((None, C_CONV - 1, width), lambda b_: (b_, 0, 0)),
            pl.BlockSpec((SUBLANES * C_CONV, width), lambda b_: (0, 0)),
            pl.BlockSpec((1, width), lambda b_: (0, 0)),
            pl.BlockSpec((1, width), lambda b_: (0, 0)),
            pl.BlockSpec((1, width), lambda b_: (0, 0)),
        ],
        out_specs=[
            pl.BlockSpec((t, width), lambda b_: (b_, 0)),
            pl.BlockSpec((None, C_CONV - 1, width), lambda b_: (b_, 0, 0)),
        ],
        out_shape=[
            jax.ShapeDtypeStruct((dec_batch * t, width), BF16),
            jax.ShapeDtypeStruct((dec_batch, C_CONV - 1, width), F32),
        ],
        scratch_shapes=[pltpu.VMEM((SUBLANES, CONV_PAD + t, width), F32), pltpu.VMEM((t, width), F32)],
        compiler_params=_cparams(("parallel",)),
        name="conv_sample",
    )(glu, zc, state, w, b, ln_g, ln_b)


def _merge_kernel(a1, a2, b1, b2, c1, c2, x1, x2, p1, p2, gates_ref, wa_ref, wb_ref, wc_ref, wo_ref,
                  pg_ref, wpg_ref, wple_ref, fg_ref, *o_refs, d, final, n_first):
    pick = functools.partial(_pick, n_first)
    branch_gate = lambda j: jax.nn.sigmoid(gates_ref[:, j * d:(j + 1) * d].astype(F32))
    merged = (branch_gate(0) * _dot(pick(a1, a2), wa_ref[...])
              + branch_gate(1) * _dot(pick(b1, b2), wb_ref[...])
              + branch_gate(2) * _dot(pick(c1, c2), wc_ref[...]))
    x = pick(x1, x2) + _dot(merged.astype(BF16), wo_ref[...])
    gate = jax.nn.sigmoid(_dot(_rms(x, pg_ref[...]).astype(BF16), wpg_ref[...]))
    x = x + gate * _dot(pick(p1, p2).astype(BF16), wple_ref[...])
    if not final:
        o_refs[0][...] = x
        return
    y = _rms(x, fg_ref[...])
    on_first = pl.program_id(0) < n_first

    @pl.when(on_first)
    def _():
        o_refs[0][...] = y

    @pl.when(jnp.logical_not(on_first))
    def _():
        o_refs[1][...] = y


def _merge(a, b, c, x_pair, p_quad, gates, wa, wb, wc, wo, pg, wpg, wple, fg, final):
    x1, x2, n_first, x_off = x_pair
    p = p_quad[:2]
    m, d = gates.shape[0], x1.shape[1]
    t = ROW_TILE
    pair = lambda arrs, off=0, off1=0: _split_specs(t, arrs[0].shape[1], n_first, off, off1)
    full = lambda arr: pl.BlockSpec(arr.shape, lambda i: (0, 0), pipeline_mode=pl.Buffered(1))
    if final:
        out_specs = _split_specs(t, d, n_first, 0)
        out_shape = [jax.ShapeDtypeStruct((n_first * t, d), F32), jax.ShapeDtypeStruct((m - n_first * t, d), F32)]
    else:
        out_specs = [pl.BlockSpec((t, d), lambda i: (i, 0))]
        out_shape = [jax.ShapeDtypeStruct((m, d), F32)]
    return pl.pallas_call(
        functools.partial(_merge_kernel, d=d, final=final, n_first=n_first),
        grid=(m // t,),
        in_specs=(pair(a) + pair(b) + pair(c) + pair((x1, x2), x_off) + pair(p, p_quad[3], p_quad[2])
                  + [pl.BlockSpec((t, 3 * d), lambda i: (i, 0))]
                  + [full(wa), full(wb), full(wc), full(wo), full(pg), full(wpg), full(wple), full(fg)]),
        out_specs=out_specs,
        out_shape=out_shape,
        compiler_params=_cparams(("arbitrary",)),
        name="merge",
    )(*a, *b, *c, x1, x2, *p, gates, wa, wb, wc, wo, pg, wpg, wple, fg)


def kernel(x_prompt, x_sample, cache_k, cache_v, cache_idx_k, state_conv, p_prompt, p_sample, norm_g, w_in, gmlp_ln_g, gmlp_ln_b, gmlp_ws, gmlp_bs, conv_w, conv_b, conv_ln_g, conv_ln_b, w_branch_a, w_branch_b, w_branch_c, w_out, ple_norm_g, w_ple_gate, w_ple, final_norm_g):
    batch, seq, d = x_prompt.shape
    dec_batch, dec_seq, _ = x_sample.shape
    depth = w_in.shape[0]
    past = cache_k.shape[2]
    bw = gmlp_ln_g.shape[1]
    cw = conv_b.shape[1]
    mp = batch * seq
    ms = dec_batch * dec_seq
    assert mp % ROW_TILE == 0 and ms % ROW_TILE == 0 and seq % ROW_TILE == 0
    assert dec_seq <= CHUNK and B_CHUNK % dec_seq == 0 and past % CHUNK == 0

    widths = (A_WIDTH, A_KV_WIDTH, A_KV_WIDTH, IDX_HEADS * IDX_DIM, IDX_DIM, IDX_HEADS, A_WIDTH,
              2 * bw, bw, 2 * cw, cw, N_BRANCH * d)
    cuts = np.concatenate([[0], np.cumsum(widths)])
    out_widths = (QW, KVW, A_WIDTH, 2 * bw, bw, 2 * cw, cw, N_BRANCH * d)
    out_dtypes = (F32, F32, BF16, F32, BF16, F32, BF16, BF16)

    tail0 = QW + KVW
    placed = [(0, 0), (3, Q_IQ), (5, Q_IW), (1, QW), (2, QW + KV_V), (4, QW + KV_IK),
              (6, tail0)]
    pieces = [(int(cuts[j]), int(widths[j]), dst) for j, dst in placed[:-1]]
    pieces.append((int(cuts[6]), int(cuts[12] - cuts[6]), tail0))
    w1 = _wprep(jnp.transpose(w_in, (2, 0, 1)), tuple(pieces), sum(out_widths))

    n_first = mp // ROW_TILE
    x_pair = (x_prompt.reshape(mp, d), x_sample.reshape(ms, d), n_first, 0)
    rep = B_CHUNK // dec_seq
    eye = jnp.eye(rep, dtype=F32)
    p_all_prompt = p_prompt.reshape(depth * mp, -1)
    p_all_sample = p_sample.reshape(depth * ms, -1)
    ck_t = jnp.transpose(cache_k, (0, 1, 3, 4, 2)).reshape(depth * dec_batch * A_KV_WIDTH, past)
    cv_t = jnp.transpose(cache_v, (0, 1, 3, 4, 2)).reshape(depth * dec_batch * A_KV_WIDTH, past)
    cik_t = jnp.transpose(cache_idx_k, (0, 1, 3, 2)).reshape(depth * dec_batch * IDX_DIM, past)

    kp, vp, ikp, cvp, ks, vs, iks, cvs, gvs = ([] for _ in range(9))
    for l in range(depth):
        pq, pkv, za, uv, zb, glu, zc, gates = _proj(x_pair, mp + ms, norm_g[l][None, :], w1, l,
                                                    out_widths, out_dtypes)

        a_p, k_t, v_t, ik_t = _attn_prompt(pq, pkv, za, batch, seq)
        a_s = _attn_sample(pq, pkv, za, ck_t, cv_t, cik_t, l, mp, dec_batch, dec_seq)

        ln_g, ln_b = gmlp_ln_g[l][None, :], gmlp_ln_b[l][None, :]
        (b_p,) = _gmlp(uv, zb, ln_g, ln_b, gmlp_ws[l], gmlp_bs[l][:, :, None], 0, mp, True, False)
        ws_s = jnp.einsum('ab,gij->gaibj', eye, gmlp_ws[l][:, :dec_seq, :dec_seq]).reshape(
            B_GROUPS, B_CHUNK, B_CHUNK)
        bs_s = jnp.tile(gmlp_bs[l][:, :dec_seq], (1, rep))[:, :, None]
        b_s, gv = _gmlp(uv, zb, ln_g, ln_b, ws_s, bs_s, mp, ms, False, True)

        cargs = (jnp.repeat(conv_w[l], SUBLANES, axis=0), conv_b[l][None, :],
                 conv_ln_g[l][None, :], conv_ln_b[l][None, :])
        c_p, tail_p = _conv_prompt(glu, zc, *cargs, batch, seq)
        c_s, tail_s = _conv_sample(glu, zc, state_conv[l], *cargs, mp, dec_batch, dec_seq)

        p = (p_all_prompt, p_all_sample, l * n_first, l * (ms // ROW_TILE))
        outs = _merge((a_p, a_s), (b_p, b_s), (c_p, c_s), x_pair, p, gates,
                      w_branch_a[l].astype(BF16), w_branch_b[l].astype(BF16), w_branch_c[l].astype(BF16),
                      w_out[l].astype(BF16), ple_norm_g[l][None, :], w_ple_gate[l].astype(BF16),
                      w_ple[l].astype(BF16), final_norm_g[None, :], l == depth - 1)
        x_pair = (outs[0], outs[0], n_first, n_first)

        heads_last = lambda x_t: jnp.transpose(x_t.reshape(batch, A_KV_HEADS, A_HEAD_DIM, seq), (0, 3, 1, 2))
        kp.append(heads_last(k_t))
        vp.append(heads_last(v_t))
        ikp.append(jnp.transpose(ik_t.reshape(batch, IDX_DIM, seq), (0, 2, 1)))
        cvp.append(tail_p)
        ks.append(pkv[mp:, 0:A_KV_WIDTH].reshape(dec_batch, dec_seq, A_KV_HEADS, A_HEAD_DIM))
        vs.append(pkv[mp:, KV_V:KV_V + A_KV_WIDTH].reshape(dec_batch, dec_seq, A_KV_HEADS, A_HEAD_DIM))
        iks.append(pkv[mp:, KV_IK:KV_IK + IDX_DIM].reshape(dec_batch, dec_seq, IDX_DIM))
        cvs.append(tail_s)
        gvs.append(gv.reshape(dec_batch, dec_seq, bw))

    return (outs[0].reshape(batch, seq, d), outs[1].reshape(dec_batch, dec_seq, d),
            jnp.stack(kp), jnp.stack(vp), jnp.stack(ikp), jnp.stack(cvp),
            jnp.stack(ks), jnp.stack(vs), jnp.stack(iks), jnp.stack(cvs), jnp.stack(gvs))
```

```python
import functools

import jax
import jax.numpy as jnp
import numpy as np
from jax import lax
from jax.experimental import pallas as pl
from jax.experimental.pallas import tpu as pltpu

F32 = jnp.float32
BF16 = jnp.bfloat16

CHUNK = 64
EPS = 1e-6
A_HEAD_DIM = 64
A_HEADS = 8
A_KV_HEADS = 2
A_REP = A_HEADS // A_KV_HEADS
A_WIDTH = A_HEADS * A_HEAD_DIM
A_KV_WIDTH = A_KV_HEADS * A_HEAD_DIM
IDX_HEADS = 4
IDX_DIM = 64
IDX_W_SCALE = (IDX_HEADS * IDX_DIM) ** -0.5
TOPK_MAX = 256
B_GROUPS = 4
B_CHUNK = 128
C_CONV = 31
N_BRANCH = 3

QW = 896
KVW = 384
Q_IQ = A_WIDTH
Q_IW = A_WIDTH + IDX_HEADS * IDX_DIM
KV_V = A_KV_WIDTH
KV_IK = 2 * A_KV_WIDTH

VMEM_LIMIT = 56 * 1024 * 1024
ROW_TILE = 256
KEY_TILE = 256
BISECT_STEPS = 16
NEG_INF = float("-inf")


def _cparams(sem):
    return pltpu.CompilerParams(dimension_semantics=sem, vmem_limit_bytes=VMEM_LIMIT)


def _rms(xf, g):
    return xf * lax.rsqrt(jnp.mean(jnp.square(xf), axis=-1, keepdims=True) + EPS) * g


def _ln(xf, g, b):
    xc = xf - jnp.mean(xf, axis=-1, keepdims=True)
    var = jnp.mean(jnp.square(xc), axis=-1, keepdims=True)
    return xc * lax.rsqrt(var + EPS) * g + b


def _silu(x):
    return x * jax.nn.sigmoid(x)


def _dot(a, b):
    return jnp.dot(a, b, preferred_element_type=F32)


def _dot_nt(a, b):
    return lax.dot_general(a, b, (((1,), (1,)), ((), ())), preferred_element_type=F32)


WPREP_ROWS = 128


def _wprep_kernel(src_ref, cnt_ref, w_ref, o_ref):
    i = pl.program_id(0)
    keep = lax.broadcasted_iota(jnp.int32, (WPREP_ROWS, w_ref.shape[2]), 0) < cnt_ref[i]
    for l in range(w_ref.shape[1]):
        o_ref[l] = jnp.where(keep, w_ref[:, l, :], 0.0).astype(BF16)


def _wprep(w_t, pieces, n_out):
    n, depth, d = w_t.shape
    src, cnt = [], []
    for dst0 in range(0, n_out, WPREP_ROWS):
        hit = [(s + dst0 - t, min(WPREP_ROWS, t + wd - dst0)) for s, wd, t in pieces if t <= dst0 < t + wd]
        assert len(hit) == 1 and hit[0][0] + WPREP_ROWS <= n
        src.append(hit[0][0])
        cnt.append(hit[0][1])
    grid_spec = pltpu.PrefetchScalarGridSpec(
        num_scalar_prefetch=2,
        grid=(n_out // WPREP_ROWS,),
        in_specs=[pl.BlockSpec((pl.Element(WPREP_ROWS), pl.Element(depth), pl.Element(d)),
                               lambda i, src_ref, cnt_ref: (src_ref[i], 0, 0))],
        out_specs=pl.BlockSpec((depth, WPREP_ROWS, d), lambda i, src_ref, cnt_ref: (0, i, 0)),
    )
    return pl.pallas_call(
        _wprep_kernel,
        grid_spec=grid_spec,
        out_shape=jax.ShapeDtypeStruct((depth, n_out, d), BF16),
        compiler_params=_cparams(("parallel",)),
        name="wprep",
    )(jnp.asarray(src, jnp.int32), jnp.asarray(cnt, jnp.int32), w_t)


def _split_specs(t, width, n_first, second_off, first_off=0):
    first = pl.BlockSpec((t, width), lambda i: (jnp.minimum(i, n_first - 1) + first_off, 0))
    second = pl.BlockSpec((t, width), lambda i: (jnp.maximum(i - n_first, 0) + second_off, 0))
    return [first, second]


def _pick(n_first, first_ref, second_ref):
    return jnp.where(pl.program_id(0) < n_first, first_ref[...], second_ref[...])


def _proj_kernel(x1_ref, x2_ref, g_ref, w_ref, *out_refs, widths, n_first):
    h = _rms(_pick(n_first, x1_ref, x2_ref), g_ref[...]).astype(BF16)
    off = 0
    for o_ref, wd in zip(out_refs, widths):
        for c in range(0, wd, 512):
            cw = min(512, wd - c)
            o_ref[:, c:c + cw] = _dot_nt(h, w_ref[off + c:off + c + cw, :])
        off += wd


def _proj(x_pair, m, g, w, layer, widths):
    x1, x2, n_first, second_off = x_pair
    d = x1.shape[1]
    n = w.shape[1]
    return pl.pallas_call(
        functools.partial(_proj_kernel, widths=widths, n_first=n_first),
        grid=(m // ROW_TILE,),
        in_specs=_split_specs(ROW_TILE, d, n_first, second_off) + [
            pl.BlockSpec((1, d), lambda i: (0, 0)),
            pl.BlockSpec((None, n, d), lambda i: (layer, 0, 0), pipeline_mode=pl.Buffered(1)),
        ],
        out_specs=[pl.BlockSpec((ROW_TILE, wd), lambda i: (i, 0)) for wd in widths],
        out_shape=[jax.ShapeDtypeStruct((m, wd), F32) for wd in widths],
        compiler_params=_cparams(("parallel",)),
        name="proj",
    )(x1, x2, g, w)


LANES = 128
TQ_PROMPT = 256
TQ_SAMPLE = 128


def _rows8(x):
    return [x[8 * j:8 * j + 8] for j in range(x.shape[0] // 8)]


def _tree(parts, op):
    while len(parts) > 1:
        nxt = [op(parts[2 * j], parts[2 * j + 1]) for j in range(len(parts) // 2)]
        if len(parts) % 2:
            nxt.append(parts[-1])
        parts = nxt
    return parts[0]


def _all8(x, op):
    for shift in (4, 2, 1):
        x = op(x, pltpu.roll(x, shift, 0))
    return x


def _fold(x, op):
    return _all8(_tree(_rows8(x), op), op)


def _per_rows(op, x, v8):
    return jnp.concatenate([op(part, v8) for part in _rows8(x)], axis=0)


def _head_slabs(x, n, dst):
    t = x.shape[0]
    keep = (lax.broadcasted_iota(jnp.int32, (t, LANES), 1) // A_HEAD_DIM) == dst
    parts = []
    for h in range(n):
        slab = x[:, (h // 2) * LANES:(h // 2 + 1) * LANES]
        if h % 2 != dst:
            slab = pltpu.roll(slab, A_HEAD_DIM, 1)
        parts.append(jnp.where(keep, slab, 0.0))
    return parts


def _to_cols(x):
    return jnp.concatenate([x[c:c + LANES, :].T for c in range(0, KEY_TILE, LANES)], axis=1)


def _to_rows(x_t):
    return jnp.concatenate([x_t[:, c:c + LANES].T for c in range(0, KEY_TILE, LANES)], axis=0)


def _fill_keys(kt, j, k, v_t, ik, kb_ref, vt_ref, ikb_ref):
    rows = pl.ds(kt * KEY_TILE, KEY_TILE)
    kb_ref[rows, j * LANES:(j + 1) * LANES] = k.astype(BF16)
    ikb_ref[rows, j * LANES:(j + 1) * LANES] = ik.astype(BF16)
    for g in range(A_KV_HEADS):
        vt_ref[kt, g, j * A_HEAD_DIM:(j + 1) * A_HEAD_DIM, :] = (
            v_t[g * A_HEAD_DIM:(g + 1) * A_HEAD_DIM, :].astype(BF16))


def _attend(qrow, za, kb_ref, vt_ref, ikb_ref, score_ref, lg_ref, p_ref, acc_ref, *, nt, q_pos0, n_keys, topk, nb=1):
    tk = KEY_TILE
    TQ = qrow.shape[0]
    tqb = TQ // nb

    def spread(x):
        if nb == 1:
            return x
        row_set = lax.broadcasted_iota(jnp.int32, x.shape, 0) // tqb
        return jnp.concatenate([jnp.where(row_set == j, x, 0.0) for j in range(nb)], axis=1)

    iq_all = jnp.concatenate([spread(s).astype(BF16) for s in _head_slabs(qrow[:, Q_IQ:Q_IW], IDX_HEADS, 0)],
                             axis=0)
    iw_t = qrow[:, Q_IW:Q_IW + LANES].T * IDX_W_SCALE
    iw_rows = [iw_t[h:h + 1, :] for h in range(IDX_HEADS)]

    k_row = lax.broadcasted_iota(jnp.int32, (tk, TQ), 0)
    q_pos = q_pos0 + lax.broadcasted_iota(jnp.int32, (tk, TQ), 1) % tqb
    q_chunk = q_pos // CHUNK

    def visible(kt):
        k_pos = kt * tk + k_row
        return ((k_pos // CHUNK) <= q_chunk) & (k_pos < n_keys), k_pos

    def score_body(kt, carry):
        mn, mx = carry
        ik_t = ikb_ref[pl.ds(pl.multiple_of(kt * tk, tk), tk), :]
        r = _dot_nt(ik_t, iq_all)
        s = jnp.zeros((tk, TQ), F32)
        for h in range(IDX_HEADS):
            s = s + jnp.maximum(r[:, h * TQ:(h + 1) * TQ], 0.0) * iw_rows[h]
        vis, _ = visible(kt)
        masked = jnp.where(vis, s, NEG_INF)
        score_ref[kt] = masked
        mn = jnp.minimum(mn, _tree(_rows8(jnp.where(vis, s, jnp.inf)), jnp.minimum))
        mx = jnp.maximum(mx, _tree(_rows8(masked), jnp.maximum))
        return mn, mx

    mn, mx = lax.fori_loop(0, nt, score_body,
                           (jnp.full((8, TQ), jnp.inf, F32), jnp.full((8, TQ), NEG_INF, F32)))
    mx = _all8(mx, jnp.maximum)

    q_pos8 = q_pos0 + lax.broadcasted_iota(jnp.int32, (8, TQ), 1) % tqb
    n_vis = jnp.minimum((q_pos8 // CHUNK + 1) * CHUNK, n_keys)
    k_eff = jnp.minimum(n_vis, topk).astype(F32)

    def count_ge(thr):
        def body(kt, acc):
            return acc + _tree([jnp.where(part >= thr, 1.0, 0.0) for part in _rows8(score_ref[kt])], jnp.add)
        return _all8(lax.fori_loop(0, nt, body, jnp.zeros((8, TQ), F32)), jnp.add)

    def bisect_step(_, state):
        lo, hi, c_lo, c_hi = state
        mid = lo + (hi - lo) * 0.5
        inside = (mid > lo) & (mid < hi)
        c = count_ge(mid)
        up = inside & (c >= k_eff)
        dn = inside & (c < k_eff)
        return (jnp.where(up, mid, lo), jnp.where(dn, mid, hi), jnp.where(up, c, c_lo), jnp.where(dn, c, c_hi))

    def band_max(lo, hi):
        def body(kt, acc):
            parts = [jnp.where((part >= lo) & (part < hi), part, NEG_INF) for part in _rows8(score_ref[kt])]
            return jnp.maximum(acc, _tree(parts, jnp.maximum))
        return _all8(lax.fori_loop(0, nt, body, jnp.full((8, TQ), NEG_INF, F32)), jnp.maximum)

    def peel_step(state):
        it, _, lo, hi, c_hi, done = state
        top = band_max(lo, hi)
        c_top = count_ge(top)
        hit = c_top >= k_eff
        cut = jnp.logical_not(hit) & (done == 0.0)
        lo = jnp.where(hit & (done == 0.0), top, lo)
        hi = jnp.where(cut, top, hi)
        c_hi = jnp.where(cut, c_top, c_hi)
        done = jnp.where(hit, 1.0, done)
        return it + 1, jnp.max(1.0 - done).astype(F32), lo, hi, c_hi, done

    lo0 = _all8(mn, jnp.minimum)
    hi0 = mx + jnp.maximum(jnp.abs(mx), 1e-30) * (2.0 ** -10)
    lo, hi, c_lo, c_hi = lax.fori_loop(
        0, BISECT_STEPS, bisect_step, (lo0, hi0, n_vis.astype(F32), jnp.zeros((8, TQ), F32)))
    done0 = jnp.where(c_lo == k_eff, 1.0, 0.0)
    init = (jnp.int32(0), jnp.max(1.0 - done0).astype(F32), lo, hi, c_hi, done0)
    _, _, lo, hi, c_hi, _ = lax.while_loop(lambda st: (st[0] < n_keys) & (st[1] > 0.0), peel_step, init)
    lo_r, hi_r = lo[0:1], hi[0:1]
    need_r = (k_eff - c_hi)[0:1]

    tril = (lax.broadcasted_iota(jnp.int32, (tk, tk), 1)
            <= lax.broadcasted_iota(jnp.int32, (tk, tk), 0)).astype(BF16)

    def select_body(kt, carry):
        s = score_ref[kt]
        inb = (s >= lo_r) & (s < hi_r)
        band = jnp.where(inb, 1.0, 0.0)
        rank = _dot(tril, band.astype(BF16)) + carry
        sel = (s >= hi_r) | (inb & (rank <= need_r))
        _, k_pos = visible(kt)
        dist = jnp.abs(q_pos - k_pos).astype(F32)
        score_ref[kt] = jnp.where(sel, -dist, NEG_INF)
        return carry + jnp.sum(band, axis=0, keepdims=True)

    lax.fori_loop(0, nt, select_body, jnp.zeros((1, TQ), F32))

    slopes = [2.0 ** (-8.0 * (h + 1) / A_HEADS) for h in range(A_HEADS)]
    gw = A_REP * A_HEAD_DIM
    q_gs = []
    for g in range(A_KV_HEADS):
        q_g = jnp.concatenate([spread(s) for s in _head_slabs(qrow[:, g * gw:(g + 1) * gw], A_REP, g)], axis=0)
        q_gs.append((q_g * (A_HEAD_DIM ** -0.5)).astype(BF16))
    acc_ref[...] = jnp.zeros(acc_ref.shape, F32)
    lane_set = (lax.broadcasted_iota(jnp.int32, (A_HEAD_DIM, A_REP * TQ), 1) % TQ) // tqb

    def attn_body(kt, carry):
        ms, ls = carry
        k_t = kb_ref[pl.ds(pl.multiple_of(kt * tk, tk), tk), :]
        nd = score_ref[kt]
        for g in range(A_KV_HEADS):
            lg_ref[g] = _dot_nt(k_t, q_gs[g])
        ms_new, ls_new = [], []
        for g in range(A_KV_HEADS):
            corrs = []
            for r in range(A_REP):
                h = g * A_REP + r
                lgr = lg_ref[g, :, r * TQ:(r + 1) * TQ] + slopes[h] * nd
                m_new = jnp.maximum(ms[h], _fold(lgr, jnp.maximum))
                m_safe = jnp.where(m_new == NEG_INF, 0.0, m_new)
                p = _per_rows(lambda a, m: jnp.exp(a - m), lgr, m_safe)
                corr = jnp.exp(ms[h] - m_safe)
                ls_new.append(ls[h] * corr + _fold(p, jnp.add))
                ms_new.append(m_new)
                corrs.append(corr)
                p_ref[g, :, r * TQ:(r + 1) * TQ] = p.astype(BF16)
            pv = _dot(vt_ref[kt, g], p_ref[g])
            if nb > 1:
                pv = sum(jnp.where(lane_set == j, pv[j * A_HEAD_DIM:(j + 1) * A_HEAD_DIM], 0.0) for j in range(nb))
            acc_ref[g] = _per_rows(jnp.multiply, acc_ref[g], jnp.concatenate(corrs, axis=1)) + pv
        return tuple(ms_new), tuple(ls_new)

    init = (tuple(jnp.full((8, TQ), NEG_INF, F32) for _ in range(A_HEADS)),
            tuple(jnp.zeros((8, TQ), F32) for _ in range(A_HEADS)))
    _, ls = lax.fori_loop(0, nt, attn_body, init)
    pieces = []
    for g in range(A_KV_HEADS):
        o_t = _per_rows(jnp.divide, acc_ref[g], jnp.concatenate(ls[g * A_REP:(g + 1) * A_REP], axis=1))
        for r in range(0, A_REP, 2):
            pair = jnp.concatenate([o_t[:, r * TQ:(r + 1) * TQ], o_t[:, (r + 1) * TQ:(r + 2) * TQ]], axis=0)
            pieces.append(pair.T)
    o_a = jnp.concatenate(pieces, axis=-1)
    return o_a * _silu(za)


def _attn_scratch(n_tiles, TQ, nb):
    return [
        pltpu.VMEM((n_tiles * KEY_TILE, nb * LANES), BF16),
        pltpu.VMEM((n_tiles, A_KV_HEADS, nb * A_HEAD_DIM, KEY_TILE), BF16),
        pltpu.VMEM((n_tiles * KEY_TILE, nb * LANES), BF16),
        pltpu.VMEM((n_tiles, KEY_TILE, TQ), F32),
        pltpu.VMEM((A_KV_HEADS, KEY_TILE, A_REP * TQ), F32),
        pltpu.VMEM((A_KV_HEADS, KEY_TILE, A_REP * TQ), BF16),
        pltpu.VMEM((A_KV_HEADS, A_HEAD_DIM, A_REP * TQ), F32),
    ]


def _attn_prompt_kernel(q_ref, kv_ref, za_ref, o_ref, kt_out, vt_out, ikt_out, kb_ref, vt_ref, ikb_ref, *work,
                        seq, topk):
    i = pl.program_id(1)

    @pl.when(i == 0)
    def _():
        for kt in range(seq // KEY_TILE):
            rows = slice(kt * KEY_TILE, (kt + 1) * KEY_TILE)
            k, ik = kv_ref[rows, 0:LANES], kv_ref[rows, KV_IK:KV_IK + LANES]
            v_t = _to_cols(kv_ref[rows, KV_V:KV_V + LANES])
            _fill_keys(kt, 0, k, v_t, ik, kb_ref, vt_ref, ikb_ref)
            kt_out[:, rows] = _to_cols(k)
            vt_out[:, rows] = v_t
            ikt_out[:, rows] = _to_cols(ik)[0:IDX_DIM, :]

    TQ = TQ_PROMPT
    nt = (i * TQ + TQ + KEY_TILE - 1) // KEY_TILE
    out = _attend(q_ref[...], za_ref[...], kb_ref, vt_ref, ikb_ref, *work,
                  nt=nt, q_pos0=i * TQ, n_keys=seq, topk=topk)
    o_ref[...] = out.astype(BF16)


def _attn_prompt(pq, pkv, za, batch, seq):
    TQ = TQ_PROMPT
    nq = seq // TQ
    topk = min(TOPK_MAX, seq // 4)
    return pl.pallas_call(
        functools.partial(_attn_prompt_kernel, seq=seq, topk=topk),
        grid=(batch, nq),
        in_specs=[
            pl.BlockSpec((TQ, QW), lambda b, i: (b * nq + i, 0)),
            pl.BlockSpec((seq, KVW), lambda b, i: (b, 0)),
            pl.BlockSpec((TQ, A_WIDTH), lambda b, i: (b * nq + i, 0)),
        ],
        out_specs=[
            pl.BlockSpec((TQ, A_WIDTH), lambda b, i: (b * nq + i, 0)),
            pl.BlockSpec((A_KV_WIDTH, seq), lambda b, i: (b, 0)),
            pl.BlockSpec((A_KV_WIDTH, seq), lambda b, i: (b, 0)),
            pl.BlockSpec((IDX_DIM, seq), lambda b, i: (b, 0)),
        ],
        out_shape=[
            jax.ShapeDtypeStruct((batch * seq, A_WIDTH), BF16),
            jax.ShapeDtypeStruct((batch * A_KV_WIDTH, seq), F32),
            jax.ShapeDtypeStruct((batch * A_KV_WIDTH, seq), F32),
            jax.ShapeDtypeStruct((batch * IDX_DIM, seq), F32),
        ],
        scratch_shapes=_attn_scratch(seq // KEY_TILE, TQ, 1),
        compiler_params=_cparams(("parallel", "arbitrary")),
        name="attn_prompt",
    )(pq, pkv, za)


def _attn_sample_kernel(q_ref, kv_ref, za_ref, ck_ref, cv_ref, cik_ref, o_ref,
                        kb_ref, vt_ref, ikb_ref, *work, t, past, topk, nt, nb):
    tk = KEY_TILE
    tail = nt * tk - past
    zeros = lambda n, w: jnp.zeros((n, w), F32)
    pad_rows = lambda x: jnp.concatenate([x, zeros(tail - t, LANES)], axis=0)
    for j in range(nb):
        for kt in range(past // tk):
            cols = slice(kt * tk, (kt + 1) * tk)
            ik_t = jnp.concatenate([cik_ref[j * IDX_DIM:(j + 1) * IDX_DIM, cols], zeros(LANES - IDX_DIM, tk)], axis=0)
            _fill_keys(kt, j, _to_rows(ck_ref[j * LANES:(j + 1) * LANES, cols]),
                       cv_ref[j * LANES:(j + 1) * LANES, cols], _to_rows(ik_t), kb_ref, vt_ref, ikb_ref)
        new = kv_ref[j * t:(j + 1) * t, :]
        _fill_keys(past // tk, j, pad_rows(new[:, 0:LANES]), _to_cols(pad_rows(new[:, KV_V:KV_V + LANES])),
                   pad_rows(new[:, KV_IK:KV_IK + LANES]), kb_ref, vt_ref, ikb_ref)
    out = _attend(q_ref[...], za_ref[...], kb_ref, vt_ref, ikb_ref, *work,
                  nt=nt, q_pos0=past, n_keys=past + t, topk=topk, nb=nb)
    o_ref[...] = out.astype(BF16)


def _attn_sample(pq, pkv, za, ck, cv, cik, layer, row0, dec_batch, t):
    past = ck.shape[1]
    nb = TQ_SAMPLE // t
    cblk0 = layer * (dec_batch // nb)
    assert past % KEY_TILE == 0 and t <= KEY_TILE and TQ_SAMPLE % t == 0 and dec_batch % nb == 0
    topk = min(TOPK_MAX, (past + t) // 4)
    nt = past // KEY_TILE + 1
    blk0 = row0 // TQ_SAMPLE
    return pl.pallas_call(
        functools.partial(_attn_sample_kernel, t=t, past=past, topk=topk, nt=nt, nb=nb),
        grid=(dec_batch // nb,),
        in_specs=[
            pl.BlockSpec((TQ_SAMPLE, QW), lambda b: (blk0 + b, 0)),
            pl.BlockSpec((TQ_SAMPLE, KVW), lambda b: (blk0 + b, 0)),
            pl.BlockSpec((TQ_SAMPLE, A_WIDTH), lambda b: (blk0 + b, 0)),
            pl.BlockSpec((nb * A_KV_WIDTH, past), lambda b: (cblk0 + b, 0)),
            pl.BlockSpec((nb * A_KV_WIDTH, past), lambda b: (cblk0 + b, 0)),
            pl.BlockSpec((nb * IDX_DIM, past), lambda b: (cblk0 + b, 0)),
        ],
        out_specs=pl.BlockSpec((TQ_SAMPLE, A_WIDTH), lambda b: (b, 0)),
        out_shape=jax.ShapeDtypeStruct((dec_batch * t, A_WIDTH), BF16),
        scratch_shapes=_attn_scratch(nt, TQ_SAMPLE, nb),
        compiler_params=_cparams(("parallel",)),
        name="attn_sample",
    )(pq, pkv, za, ck, cv, cik)


GMLP_ROWS = 512


def _gmlp_kernel(uv_ref, zb_ref, g_ref, b_ref, w_ref, bs_ref, o_ref, *v_out, width, mask_chunks):
    gw = width // B_GROUPS
    ws = []
    for g in range(B_GROUPS):
        w = w_ref[g]
        if mask_chunks:
            i = lax.broadcasted_iota(jnp.int32, (B_CHUNK, B_CHUNK), 0)
            j = lax.broadcasted_iota(jnp.int32, (B_CHUNK, B_CHUNK), 1)
            w = jnp.where((j // CHUNK) <= (i // CHUNK), w, 0.0)
        ws.append(w.astype(BF16))
    for c in range(uv_ref.shape[0] // B_CHUNK):
        rows = slice(c * B_CHUNK, (c + 1) * B_CHUNK)
        uv = uv_ref[rows, :]
        act = 0.5 * uv * (1.0 + lax.erf(uv * np.float32(1.0 / np.sqrt(2.0))))
        u = act[:, :width]
        v = _ln(act[:, width:], g_ref[...], b_ref[...])
        if v_out:
            v_out[0][rows, :] = v
        vb = v.astype(BF16)
        zs = _silu(zb_ref[rows, :])
        for g in range(B_GROUPS):
            cols = slice(g * gw, (g + 1) * gw)
            mixed = _dot(ws[g], vb[:, cols]) + bs_ref[g]
            o_ref[rows, cols] = (u[:, cols] * mixed * zs[:, cols]).astype(BF16)


def _gmlp(uv, zb, ln_g, ln_b, w, bs, row0, rows, mask_chunks, want_v):
    width = zb.shape[1]
    t = GMLP_ROWS
    assert row0 % t == 0 and rows % t == 0
    blk0 = row0 // t
    out_specs = [pl.BlockSpec((t, width), lambda i: (i, 0))]
    out_shape = [jax.ShapeDtypeStruct((rows, width), BF16)]
    if want_v:
        out_specs.append(pl.BlockSpec((t, width), lambda i: (i, 0)))
        out_shape.append(jax.ShapeDtypeStruct((rows, width), F32))
    return pl.pallas_call(
        functools.partial(_gmlp_kernel, width=width, mask_chunks=mask_chunks),
        grid=(rows // t,),
        in_specs=[
            pl.BlockSpec((t, 2 * width), lambda i: (blk0 + i, 0)),
            pl.BlockSpec((t, width), lambda i: (blk0 + i, 0)),
            pl.BlockSpec((1, width), lambda i: (0, 0)),
            pl.BlockSpec((1, width), lambda i: (0, 0)),
            pl.BlockSpec((B_GROUPS, B_CHUNK, B_CHUNK), lambda i: (0, 0, 0)),
            pl.BlockSpec((B_GROUPS, B_CHUNK, 1), lambda i: (0, 0, 0)),
        ],
        out_specs=out_specs,
        out_shape=out_shape,
        compiler_params=_cparams(("parallel",)),
        name="gmlp_sample" if want_v else "gmlp_prompt",
    )(uv, zb, ln_g, ln_b, w, bs)


CONV_PAD = 32


CONV_ROWS = 64
SUBLANES = 8


def _conv_tile(glu_ref, zc_ref, o_ref, ext_ref, y_ref, w_ref, b_ref, g_ref, beta_ref, t, width):
    ext_ref[0, CONV_PAD:CONV_PAD + t, :] = glu_ref[:, :width] * jax.nn.sigmoid(glu_ref[:, width:])
    n = t + CONV_PAD - SUBLANES
    for s in range(1, SUBLANES):
        ext_ref[s, 0:n, :] = ext_ref[0, s:s + n, :]
    base = CONV_PAD - (C_CONV - 1)

    rc = min(CONV_ROWS, t)

    def rows_step(ci, carry):
        r0 = pl.multiple_of(ci * rc, rc)
        y = jnp.zeros((rc, width), F32)
        for k in range(C_CONV):
            s, q = (base + k) % SUBLANES, (base + k) // SUBLANES
            tap = w_ref[SUBLANES * k:SUBLANES * (k + 1), :]
            y = y + _per_rows(jnp.multiply, ext_ref[s, pl.ds(r0 + SUBLANES * q, rc), :], tap)
        y_ref[pl.ds(r0, rc), :] = y
        return carry

    lax.fori_loop(0, t // rc, rows_step, 0)
    y = y_ref[...] + b_ref[...]
    o_ref[...] = (_silu(_ln(y, g_ref[...], beta_ref[...])) * _silu(zc_ref[...])).astype(BF16)


def _conv_prompt_kernel(glu_ref, zc_ref, w_ref, b_ref, g_ref, beta_ref, o_ref, tail_ref, ext_ref, y_ref,
                        *, t, width):
    i = pl.program_id(1)

    @pl.when(i == 0)
    def _():
        ext_ref[0, 0:CONV_PAD, :] = jnp.zeros((CONV_PAD, width), F32)

    _conv_tile(glu_ref, zc_ref, o_ref, ext_ref, y_ref, w_ref, b_ref, g_ref, beta_ref, t, width)
    tail = ext_ref[0, t:t + CONV_PAD, :]
    ext_ref[0, 0:CONV_PAD, :] = tail

    @pl.when(i == pl.num_programs(1) - 1)
    def _():
        tail_ref[...] = tail[CONV_PAD - (C_CONV - 1):, :]


def _conv_prompt(glu, zc, w, b, ln_g, ln_b, batch, seq):
    t = ROW_TILE
    width = zc.shape[1]
    nb = seq // t
    return pl.pallas_call(
        functools.partial(_conv_prompt_kernel, t=t, width=width),
        grid=(batch, nb),
        in_specs=[
            pl.BlockSpec((t, 2 * width), lambda b_, i: (b_ * nb + i, 0)),
            pl.BlockSpec((t, width), lambda b_, i: (b_ * nb + i, 0)),
            pl.BlockSpec((SUBLANES * C_CONV, width), lambda b_, i: (0, 0)),
            pl.BlockSpec((1, width), lambda b_, i: (0, 0)),
            pl.BlockSpec((1, width), lambda b_, i: (0, 0)),
            pl.BlockSpec((1, width), lambda b_, i: (0, 0)),
        ],
        out_specs=[
            pl.BlockSpec((t, width), lambda b_, i: (b_ * nb + i, 0)),
            pl.BlockSpec((None, C_CONV - 1, width), lambda b_, i: (b_, 0, 0)),
        ],
        out_shape=[
            jax.ShapeDtypeStruct((batch * seq, width), BF16),
            jax.ShapeDtypeStruct((batch, C_CONV - 1, width), F32),
        ],
        scratch_shapes=[pltpu.VMEM((SUBLANES, CONV_PAD + t, width), F32), pltpu.VMEM((t, width), F32)],
        compiler_params=_cparams(("parallel", "arbitrary")),
        name="conv_prompt",
    )(glu, zc, w, b, ln_g, ln_b)


def _conv_sample_kernel(glu_ref, zc_ref, st_ref, w_ref, b_ref, g_ref, beta_ref, o_ref, tail_ref, ext_ref, y_ref,
                        *, t, width):
    base = CONV_PAD - (C_CONV - 1)
    ext_ref[0, 0:base, :] = jnp.zeros((base, width), F32)
    ext_ref[0, base:CONV_PAD, :] = st_ref[...]
    _conv_tile(glu_ref, zc_ref, o_ref, ext_ref, y_ref, w_ref, b_ref, g_ref, beta_ref, t, width)
    tail_ref[...] = ext_ref[0, t + base:t + CONV_PAD, :]


def _conv_sample(glu, zc, state, w, b, ln_g, ln_b, row0, dec_batch, t):
    width = zc.shape[1]
    blk0 = row0 // t
    return pl.pallas_call(
        functools.partial(_conv_sample_kernel, t=t, width=width),
        grid=(dec_batch,),
        in_specs=[
            pl.BlockSpec((t, 2 * width), lambda b_: (blk0 + b_, 0)),
            pl.BlockSpec((t, width), lambda b_: (blk0 + b_, 0)),
            pl.BlockSpec((None, C_CONV - 1, width), lambda b_: (b_, 0, 0)),
            pl.BlockSpec((SUBLANES * C_CONV, width), lambda b_: (0, 0)),
            pl.BlockSpec((1, width), lambda b_: (0, 0)),
            pl.BlockSpec((1, width), lambda b_: (0, 0)),
            pl.BlockSpec((1, width), lambda b_: (0, 0)),
        ],
        out_specs=[
            pl.BlockSpec((t, width), lambda b_: (b_, 0)),
            pl.BlockSpec((None, C_CONV - 1, width), lambda b_: (b_, 0, 0)),
        ],
        out_shape=[
            jax.ShapeDtypeStruct((dec_batch * t, width), BF16),
            jax.ShapeDtypeStruct((dec_batch, C_CONV - 1, width), F32),
        ],
        scratch_shapes=[pltpu.VMEM((SUBLANES, CONV_PAD + t, width), F32), pltpu.VMEM((t, width), F32)],
        compiler_params=_cparams(("parallel",)),
        name="conv_sample",
    )(glu, zc, state, w, b, ln_g, ln_b)


def _merge_kernel(a1, a2, b1, b2, c1, c2, x1, x2, p1, p2, gates_ref, wa_ref, wb_ref, wc_ref, wo_ref,
                  pg_ref, wpg_ref, wple_ref, fg_ref, *o_refs, d, final, n_first):
    pick = functools.partial(_pick, n_first)
    merged = (jax.nn.sigmoid(gates_ref[:, 0:d]) * _dot(pick(a1, a2), wa_ref[...])
              + jax.nn.sigmoid(gates_ref[:, d:2 * d]) * _dot(pick(b1, b2), wb_ref[...])
              + jax.nn.sigmoid(gates_ref[:, 2 * d:3 * d]) * _dot(pick(c1, c2), wc_ref[...]))
    x = pick(x1, x2) + _dot(merged.astype(BF16), wo_ref[...])
    gate = jax.nn.sigmoid(_dot(_rms(x, pg_ref[...]).astype(BF16), wpg_ref[...]))
    x = x + gate * _dot(pick(p1, p2).astype(BF16), wple_ref[...])
    if not final:
        o_refs[0][...] = x
        return
    y = _rms(x, fg_ref[...])
    on_first = pl.program_id(0) < n_first

    @pl.when(on_first)
    def _():
        o_refs[0][...] = y

    @pl.when(jnp.logical_not(on_first))
    def _():
        o_refs[1][...] = y


def _merge(a, b, c, x_pair, p_quad, gates, wa, wb, wc, wo, pg, wpg, wple, fg, final):
    x1, x2, n_first, x_off = x_pair
    p = p_quad[:2]
    m, d = gates.shape[0], x1.shape[1]
    t = ROW_TILE
    pair = lambda arrs, off=0, off1=0: _split_specs(t, arrs[0].shape[1], n_first, off, off1)
    full = lambda arr: pl.BlockSpec(arr.shape, lambda i: (0, 0), pipeline_mode=pl.Buffered(1))
    if final:
        out_specs = _split_specs(t, d, n_first, 0)
        out_shape = [jax.ShapeDtypeStruct((n_first * t, d), F32), jax.ShapeDtypeStruct((m - n_first * t, d), F32)]
    else:
        out_specs = [pl.BlockSpec((t, d), lambda i: (i, 0))]
        out_shape = [jax.ShapeDtypeStruct((m, d), F32)]
    return pl.pallas_call(
        functools.partial(_merge_kernel, d=d, final=final, n_first=n_first),
        grid=(m // t,),
        in_specs=(pair(a) + pair(b) + pair(c) + pair((x1, x2), x_off) + pair(p, p_quad[3], p_quad[2])
                  + [pl.BlockSpec((t, 3 * d), lambda i: (i, 0))]
                  + [full(wa), full(wb), full(wc), full(wo), full(pg), full(wpg), full(wple), full(fg)]),
        out_specs=out_specs,
        out_shape=out_shape,
        compiler_params=_cparams(("arbitrary",)),
        name="merge",
    )(*a, *b, *c, x1, x2, *p, gates, wa, wb, wc, wo, pg, wpg, wple, fg)


def kernel(x_prompt, x_sample, cache_k, cache_v, cache_idx_k, state_conv, p_prompt, p_sample, norm_g, w_in, gmlp_ln_g, gmlp_ln_b, gmlp_ws, gmlp_bs, conv_w, conv_b, conv_ln_g, conv_ln_b, w_branch_a, w_branch_b, w_branch_c, w_out, ple_norm_g, w_ple_gate, w_ple, final_norm_g):
    batch, seq, d = x_prompt.shape
    dec_batch, dec_seq, _ = x_sample.shape
    depth = w_in.shape[0]
    past = cache_k.shape[2]
    bw = gmlp_ln_g.shape[1]
    cw = conv_b.shape[1]
    mp = batch * seq
    ms = dec_batch * dec_seq
    assert mp % ROW_TILE == 0 and ms % ROW_TILE == 0 and seq % ROW_TILE == 0
    assert dec_seq <= CHUNK and B_CHUNK % dec_seq == 0 and past % CHUNK == 0

    widths = (A_WIDTH, A_KV_WIDTH, A_KV_WIDTH, IDX_HEADS * IDX_DIM, IDX_DIM, IDX_HEADS, A_WIDTH,
              2 * bw, bw, 2 * cw, cw, N_BRANCH * d)
    cuts = np.concatenate([[0], np.cumsum(widths)])
    out_widths = (QW, KVW, A_WIDTH, 2 * bw, bw, 2 * cw, cw, N_BRANCH * d)

    tail0 = QW + KVW
    placed = [(0, 0), (3, Q_IQ), (5, Q_IW), (1, QW), (2, QW + KV_V), (4, QW + KV_IK),
              (6, tail0)]
    pieces = [(int(cuts[j]), int(widths[j]), dst) for j, dst in placed[:-1]]
    pieces.append((int(cuts[6]), int(cuts[12] - cuts[6]), tail0))
    w1 = _wprep(jnp.transpose(w_in, (2, 0, 1)), tuple(pieces), sum(out_widths))

    n_first = mp // ROW_TILE
    x_pair = (x_prompt.reshape(mp, d), x_sample.reshape(ms, d), n_first, 0)
    rep = B_CHUNK // dec_seq
    eye = jnp.eye(rep, dtype=F32)
    p_all_prompt = p_prompt.reshape(depth * mp, -1)
    p_all_sample = p_sample.reshape(depth * ms, -1)
    ck_t = jnp.transpose(cache_k, (0, 1, 3, 4, 2)).reshape(depth * dec_batch * A_KV_WIDTH, past)
    cv_t = jnp.transpose(cache_v, (0, 1, 3, 4, 2)).reshape(depth * dec_batch * A_KV_WIDTH, past)
    cik_t = jnp.transpose(cache_idx_k, (0, 1, 3, 2)).reshape(depth * dec_batch * IDX_DIM, past)

    kp, vp, ikp, cvp, ks, vs, iks, cvs, gvs = ([] for _ in range(9))
    for l in range(depth):
        pq, pkv, za, uv, zb, glu, zc, gates = _proj(x_pair, mp + ms, norm_g[l][None, :], w1, l, out_widths)

        a_p, k_t, v_t, ik_t = _attn_prompt(pq, pkv, za, batch, seq)
        a_s = _attn_sample(pq, pkv, za, ck_t, cv_t, cik_t, l, mp, dec_batch, dec_seq)

        ln_g, ln_b = gmlp_ln_g[l][None, :], gmlp_ln_b[l][None, :]
        (b_p,) = _gmlp(uv, zb, ln_g, ln_b, gmlp_ws[l], gmlp_bs[l][:, :, None], 0, mp, True, False)
        ws_s = jnp.einsum('ab,gij->gaibj', eye, gmlp_ws[l][:, :dec_seq, :dec_seq]).reshape(
            B_GROUPS, B_CHUNK, B_CHUNK)
        bs_s = jnp.tile(gmlp_bs[l][:, :dec_seq], (1, rep))[:, :, None]
        b_s, gv = _gmlp(uv, zb, ln_g, ln_b, ws_s, bs_s, mp, ms, False, True)

        cargs = (jnp.repeat(conv_w[l], SUBLANES, axis=0), conv_b[l][None, :],
                 conv_ln_g[l][None, :], conv_ln_b[l][None, :])
        c_p, tail_p = _conv_prompt(glu, zc, *cargs, batch, seq)
        c_s, tail_s = _conv_sample(glu, zc, state_conv[l], *cargs, mp, dec_batch, dec_seq)

        p = (p_all_prompt, p_all_sample, l * n_first, l * (ms // ROW_TILE))
        outs = _merge((a_p, a_s), (b_p, b_s), (c_p, c_s), x_pair, p, gates,
                      w_branch_a[l].astype(BF16), w_branch_b[l].astype(BF16), w_branch_c[l].astype(BF16),
                      w_out[l].astype(BF16), ple_norm_g[l][None, :], w_ple_gate[l].astype(BF16),
                      w_ple[l].astype(BF16), final_norm_g[None, :], l == depth - 1)
        x_pair = (outs[0], outs[0], n_first, n_first)

        heads_last = lambda x_t: jnp.transpose(x_t.reshape(batch, A_KV_HEADS, A_HEAD_DIM, seq), (0, 3, 1, 2))
        kp.append(heads_last(k_t))
        vp.append(heads_last(v_t))
        ikp.append(jnp.transpose(ik_t.reshape(batch, IDX_DIM, seq), (0, 2, 1)))
        cvp.append(tail_p)
        ks.append(pkv[mp:, 0:A_KV_WIDTH].reshape(dec_batch, dec_seq, A_KV_HEADS, A_HEAD_DIM))
        vs.append(pkv[mp:, KV_V:KV_V + A_KV_WIDTH].reshape(dec_batch, dec_seq, A_KV_HEADS, A_HEAD_DIM))
        iks.append(pkv[mp:, KV_IK:KV_IK + IDX_DIM].reshape(dec_batch, dec_seq, IDX_DIM))
        cvs.append(tail_s)
        gvs.append(gv.reshape(dec_batch, dec_seq, bw))

    return (outs[0].reshape(batch, seq, d), outs[1].reshape(dec_batch, dec_seq, d),
            jnp.stack(kp), jnp.stack(vp), jnp.stack(ikp), jnp.stack(cvp),
            jnp.stack(ks), jnp.stack(vs), jnp.stack(iks), jnp.stack(cvs), jnp.stack(gvs))
```

```python
import functools

import jax
import jax.numpy as jnp
import numpy as np
from jax import lax
from jax.experimental import pallas as pl
from jax.experimental.pallas import tpu as pltpu

F32 = jnp.float32
BF16 = jnp.bfloat16

CHUNK = 64
EPS = 1e-6
A_HEAD_DIM = 64
A_HEADS = 8
A_KV_HEADS = 2
A_REP = A_HEADS // A_KV_HEADS
A_WIDTH = A_HEADS * A_HEAD_DIM
A_KV_WIDTH = A_KV_HEADS * A_HEAD_DIM
IDX_HEADS = 4
IDX_DIM = 64
IDX_W_SCALE = (IDX_HEADS * IDX_DIM) ** -0.5
TOPK_MAX = 256
B_GROUPS = 4
B_CHUNK = 128
C_CONV = 31
N_BRANCH = 3

QW = 896
KVW = 384
Q_IQ = A_WIDTH
Q_IW = A_WIDTH + IDX_HEADS * IDX_DIM
KV_V = A_KV_WIDTH
KV_IK = 2 * A_KV_WIDTH

VMEM_LIMIT = 56 * 1024 * 1024
ROW_TILE = 256
KEY_TILE = 256
BISECT_STEPS = 16
NEG_INF = float("-inf")


def _cparams(sem):
    return pltpu.CompilerParams(dimension_semantics=sem, vmem_limit_bytes=VMEM_LIMIT)


def _rms(xf, g):
    return xf * lax.rsqrt(jnp.mean(jnp.square(xf), axis=-1, keepdims=True) + EPS) * g


def _ln(xf, g, b):
    xc = xf - jnp.mean(xf, axis=-1, keepdims=True)
    var = jnp.mean(jnp.square(xc), axis=-1, keepdims=True)
    return xc * lax.rsqrt(var + EPS) * g + b


def _silu(x):
    return x * jax.nn.sigmoid(x)


def _dot(a, b):
    return jnp.dot(a, b, preferred_element_type=F32)


def _dot_nt(a, b):
    return lax.dot_general(a, b, (((1,), (1,)), ((), ())), preferred_element_type=F32)


WPREP_ROWS = 128


def _wprep_kernel(src_ref, cnt_ref, w_ref, o_ref):
    i = pl.program_id(0)
    keep = lax.broadcasted_iota(jnp.int32, (WPREP_ROWS, w_ref.shape[2]), 0) < cnt_ref[i]
    for l in range(w_ref.shape[1]):
        o_ref[l] = jnp.where(keep, w_ref[:, l, :], 0.0).astype(BF16)


def _wprep(w_t, pieces, n_out):
    n, depth, d = w_t.shape
    src, cnt = [], []
    for dst0 in range(0, n_out, WPREP_ROWS):
        hit = [(s + dst0 - t, min(WPREP_ROWS, t + wd - dst0)) for s, wd, t in pieces if t <= dst0 < t + wd]
        assert len(hit) == 1 and hit[0][0] + WPREP_ROWS <= n
        src.append(hit[0][0])
        cnt.append(hit[0][1])
    grid_spec = pltpu.PrefetchScalarGridSpec(
        num_scalar_prefetch=2,
        grid=(n_out // WPREP_ROWS,),
        in_specs=[pl.BlockSpec((pl.Element(WPREP_ROWS), pl.Element(depth), pl.Element(d)),
                               lambda i, src_ref, cnt_ref: (src_ref[i], 0, 0))],
        out_specs=pl.BlockSpec((depth, WPREP_ROWS, d), lambda i, src_ref, cnt_ref: (0, i, 0)),
    )
    return pl.pallas_call(
        _wprep_kernel,
        grid_spec=grid_spec,
        out_shape=jax.ShapeDtypeStruct((depth, n_out, d), BF16),
        compiler_params=_cparams(("parallel",)),
        name="wprep",
    )(jnp.asarray(src, jnp.int32), jnp.asarray(cnt, jnp.int32), w_t)


def _split_specs(t, width, n_first, second_off, first_off=0):
    first = pl.BlockSpec((t, width), lambda i: (jnp.minimum(i, n_first - 1) + first_off, 0))
    second = pl.BlockSpec((t, width), lambda i: (jnp.maximum(i - n_first, 0) + second_off, 0))
    return [first, second]


def _pick(n_first, first_ref, second_ref):
    return jnp.where(pl.program_id(0) < n_first, first_ref[...], second_ref[...])


def _proj_kernel(x1_ref, x2_ref, g_ref, w_ref, *out_refs, widths, n_first):
    h = _rms(_pick(n_first, x1_ref, x2_ref), g_ref[...]).astype(BF16)
    off = 0
    for o_ref, wd in zip(out_refs, widths):
        for c in range(0, wd, 512):
            cw = min(512, wd - c)
            o_ref[:, c:c + cw] = _dot_nt(h, w_ref[off + c:off + c + cw, :])
        off += wd


def _proj(x_pair, m, g, w, layer, widths):
    x1, x2, n_first, second_off = x_pair
    d = x1.shape[1]
    n = w.shape[1]
    return pl.pallas_call(
        functools.partial(_proj_kernel, widths=widths, n_first=n_first),
        grid=(m // ROW_TILE,),
        in_specs=_split_specs(ROW_TILE, d, n_first, second_off) + [
            pl.BlockSpec((1, d), lambda i: (0, 0)),
            pl.BlockSpec((None, n, d), lambda i: (layer, 0, 0), pipeline_mode=pl.Buffered(1)),
        ],
        out_specs=[pl.BlockSpec((ROW_TILE, wd), lambda i: (i, 0)) for wd in widths],
        out_shape=[jax.ShapeDtypeStruct((m, wd), F32) for wd in widths],
        compiler_params=_cparams(("parallel",)),
        name="proj",
    )(x1, x2, g, w)


LANES = 128
TQ_PROMPT = 256
TQ_SAMPLE = 128


def _rows8(x):
    return [x[8 * j:8 * j + 8] for j in range(x.shape[0] // 8)]


def _tree(parts, op):
    while len(parts) > 1:
        nxt = [op(parts[2 * j], parts[2 * j + 1]) for j in range(len(parts) // 2)]
        if len(parts) % 2:
            nxt.append(parts[-1])
        parts = nxt
    return parts[0]


def _all8(x, op):
    for shift in (4, 2, 1):
        x = op(x, pltpu.roll(x, shift, 0))
    return x


def _fold(x, op):
    return _all8(_tree(_rows8(x), op), op)


def _per_rows(op, x, v8):
    return jnp.concatenate([op(part, v8) for part in _rows8(x)], axis=0)


def _head_slabs(x, n, dst):
    t = x.shape[0]
    keep = (lax.broadcasted_iota(jnp.int32, (t, LANES), 1) // A_HEAD_DIM) == dst
    parts = []
    for h in range(n):
        slab = x[:, (h // 2) * LANES:(h // 2 + 1) * LANES]
        if h % 2 != dst:
            slab = pltpu.roll(slab, A_HEAD_DIM, 1)
        parts.append(jnp.where(keep, slab, 0.0))
    return parts


def _to_cols(x):
    return jnp.concatenate([x[c:c + LANES, :].T for c in range(0, KEY_TILE, LANES)], axis=1)


def _to_rows(x_t):
    return jnp.concatenate([x_t[:, c:c + LANES].T for c in range(0, KEY_TILE, LANES)], axis=0)


def _fill_keys(kt, j, k, v_t, ik, kb_ref, vt_ref, ikb_ref):
    rows = pl.ds(kt * KEY_TILE, KEY_TILE)
    kb_ref[rows, j * LANES:(j + 1) * LANES] = k.astype(BF16)
    ikb_ref[rows, j * LANES:(j + 1) * LANES] = ik.astype(BF16)
    for g in range(A_KV_HEADS):
        vt_ref[kt, g, j * A_HEAD_DIM:(j + 1) * A_HEAD_DIM, :] = (
            v_t[g * A_HEAD_DIM:(g + 1) * A_HEAD_DIM, :].astype(BF16))


def _attend(qrow, za, kb_ref, vt_ref, ikb_ref, score_ref, lg_ref, p_ref, acc_ref, *, nt, q_pos0, n_keys, topk, nb=1):
    tk = KEY_TILE
    TQ = qrow.shape[0]
    tqb = TQ // nb

    def spread(x):
        if nb == 1:
            return x
        row_set = lax.broadcasted_iota(jnp.int32, x.shape, 0) // tqb
        return jnp.concatenate([jnp.where(row_set == j, x, 0.0) for j in range(nb)], axis=1)

    iq_all = jnp.concatenate([spread(s).astype(BF16) for s in _head_slabs(qrow[:, Q_IQ:Q_IW], IDX_HEADS, 0)],
                             axis=0)
    iw_t = qrow[:, Q_IW:Q_IW + LANES].T * IDX_W_SCALE
    iw_rows = [iw_t[h:h + 1, :] for h in range(IDX_HEADS)]

    k_row = lax.broadcasted_iota(jnp.int32, (tk, TQ), 0)
    q_pos = q_pos0 + lax.broadcasted_iota(jnp.int32, (tk, TQ), 1) % tqb
    n_vis_q = jnp.minimum((q_pos // CHUNK + 1) * CHUNK, n_keys)

    def visible(kt):
        k_pos = kt * tk + k_row
        return k_pos < n_vis_q, k_pos

    def score_body(kt, carry):
        mn, mx = carry
        ik_t = ikb_ref[pl.ds(pl.multiple_of(kt * tk, tk), tk), :]
        r = _dot_nt(ik_t, iq_all)
        s = jnp.zeros((tk, TQ), F32)
        for h in range(IDX_HEADS):
            s = s + jnp.maximum(r[:, h * TQ:(h + 1) * TQ], 0.0) * iw_rows[h]
        vis, _ = visible(kt)
        masked = jnp.where(vis, s, NEG_INF)
        score_ref[kt] = masked
        mn = jnp.minimum(mn, _tree(_rows8(jnp.where(vis, s, jnp.inf)), jnp.minimum))
        mx = jnp.maximum(mx, _tree(_rows8(masked), jnp.maximum))
        return mn, mx

    mn, mx = lax.fori_loop(0, nt, score_body,
                           (jnp.full((8, TQ), jnp.inf, F32), jnp.full((8, TQ), NEG_INF, F32)))
    mx = _all8(mx, jnp.maximum)

    q_pos8 = q_pos0 + lax.broadcasted_iota(jnp.int32, (8, TQ), 1) % tqb
    n_vis = jnp.minimum((q_pos8 // CHUNK + 1) * CHUNK, n_keys)
    k_eff = jnp.minimum(n_vis, topk).astype(F32)

    def count_ge(thr):
        def body(kt, acc):
            return acc + _tree([jnp.where(part >= thr, 1.0, 0.0) for part in _rows8(score_ref[kt])], jnp.add)
        return _all8(lax.fori_loop(0, nt, body, jnp.zeros((8, TQ), F32)), jnp.add)

    def bisect_step(_, state):
        lo, hi, c_lo, c_hi = state
        mid = lo + (hi - lo) * 0.5
        inside = (mid > lo) & (mid < hi)
        c = count_ge(mid)
        up = inside & (c >= k_eff)
        dn = inside & (c < k_eff)
        return (jnp.where(up, mid, lo), jnp.where(dn, mid, hi), jnp.where(up, c, c_lo), jnp.where(dn, c, c_hi))

    def band_max(lo, hi):
        def body(kt, acc):
            parts = [jnp.where((part >= lo) & (part < hi), part, NEG_INF) for part in _rows8(score_ref[kt])]
            return jnp.maximum(acc, _tree(parts, jnp.maximum))
        return _all8(lax.fori_loop(0, nt, body, jnp.full((8, TQ), NEG_INF, F32)), jnp.maximum)

    def peel_step(state):
        it, _, lo, hi, c_hi, done = state
        top = band_max(lo, hi)
        c_top = count_ge(top)
        hit = c_top >= k_eff
        cut = jnp.logical_not(hit) & (done == 0.0)
        lo = jnp.where(hit & (done == 0.0), top, lo)
        hi = jnp.where(cut, top, hi)
        c_hi = jnp.where(cut, c_top, c_hi)
        done = jnp.where(hit, 1.0, done)
        return it + 1, jnp.max(1.0 - done).astype(F32), lo, hi, c_hi, done

    lo0 = _all8(mn, jnp.minimum)
    hi0 = mx + jnp.maximum(jnp.abs(mx), 1e-30) * (2.0 ** -10)
    lo, hi, c_lo, c_hi = lax.fori_loop(
        0, BISECT_STEPS, bisect_step, (lo0, hi0, n_vis.astype(F32), jnp.zeros((8, TQ), F32)))
    done0 = jnp.where(c_lo == k_eff, 1.0, 0.0)
    init = (jnp.int32(0), jnp.max(1.0 - done0).astype(F32), lo, hi, c_hi, done0)
    _, _, lo, hi, c_hi, _ = lax.while_loop(lambda st: (st[0] < n_keys) & (st[1] > 0.0), peel_step, init)
    lo_r, hi_r = lo[0:1], hi[0:1]
    need_r = (k_eff - c_hi)[0:1]

    tril = (lax.broadcasted_iota(jnp.int32, (tk, tk), 1)
            <= lax.broadcasted_iota(jnp.int32, (tk, tk), 0)).astype(BF16)

    def select_body(kt, carry):
        s = score_ref[kt]
        inb = (s >= lo_r) & (s < hi_r)
        band = jnp.where(inb, 1.0, 0.0)
        rank = _dot(tril, band.astype(BF16)) + carry
        sel = (s >= hi_r) | (inb & (rank <= need_r))
        _, k_pos = visible(kt)
        dist = jnp.abs(q_pos - k_pos).astype(F32)
        score_ref[kt] = jnp.where(sel, -dist, NEG_INF)
        return carry + jnp.sum(band, axis=0, keepdims=True)

    lax.fori_loop(0, nt, select_body, jnp.zeros((1, TQ), F32))

    slopes = [2.0 ** (-8.0 * (h + 1) / A_HEADS) for h in range(A_HEADS)]
    gw = A_REP * A_HEAD_DIM
    q_gs = []
    for g in range(A_KV_HEADS):
        q_g = jnp.concatenate([spread(s) for s in _head_slabs(qrow[:, g * gw:(g + 1) * gw], A_REP, g)], axis=0)
        q_gs.append((q_g * (A_HEAD_DIM ** -0.5)).astype(BF16))
    acc_ref[...] = jnp.zeros(acc_ref.shape, F32)
    lane_set = (lax.broadcasted_iota(jnp.int32, (A_HEAD_DIM, A_REP * TQ), 1) % TQ) // tqb

    def attn_body(kt, carry):
        ms, ls = carry
        k_t = kb_ref[pl.ds(pl.multiple_of(kt * tk, tk), tk), :]
        nd = score_ref[kt]
        for g in range(A_KV_HEADS):
            lg_ref[g] = _dot_nt(k_t, q_gs[g])
        ms_new, ls_new = [], []
        for g in range(A_KV_HEADS):
            corrs = []
            for r in range(A_REP):
                h = g * A_REP + r
                lgr = lg_ref[g, :, r * TQ:(r + 1) * TQ] + slopes[h] * nd
                m_new = jnp.maximum(ms[h], _fold(lgr, jnp.maximum))
                m_safe = jnp.where(m_new == NEG_INF, 0.0, m_new)
                p = _per_rows(lambda a, m: jnp.exp(a - m), lgr, m_safe)
                corr = jnp.exp(ms[h] - m_safe)
                ls_new.append(ls[h] * corr + _fold(p, jnp.add))
                ms_new.append(m_new)
                corrs.append(corr)
                p_ref[g, :, r * TQ:(r + 1) * TQ] = p.astype(BF16)
            pv = _dot(vt_ref[kt, g], p_ref[g])
            if nb > 1:
                pv = sum(jnp.where(lane_set == j, pv[j * A_HEAD_DIM:(j + 1) * A_HEAD_DIM], 0.0) for j in range(nb))
            acc_ref[g] = _per_rows(jnp.multiply, acc_ref[g], jnp.concatenate(corrs, axis=1)) + pv
        return tuple(ms_new), tuple(ls_new)

    init = (tuple(jnp.full((8, TQ), NEG_INF, F32) for _ in range(A_HEADS)),
            tuple(jnp.zeros((8, TQ), F32) for _ in range(A_HEADS)))
    _, ls = lax.fori_loop(0, nt, attn_body, init)
    pieces = []
    for g in range(A_KV_HEADS):
        o_t = _per_rows(jnp.divide, acc_ref[g], jnp.concatenate(ls[g * A_REP:(g + 1) * A_REP], axis=1))
        for r in range(0, A_REP, 2):
            pair = jnp.concatenate([o_t[:, r * TQ:(r + 1) * TQ], o_t[:, (r + 1) * TQ:(r + 2) * TQ]], axis=0)
            pieces.append(pair.T)
    o_a = jnp.concatenate(pieces, axis=-1)
    return o_a * _silu(za)


def _attn_scratch(n_tiles, TQ, nb):
    return [
        pltpu.VMEM((n_tiles * KEY_TILE, nb * LANES), BF16),
        pltpu.VMEM((n_tiles, A_KV_HEADS, nb * A_HEAD_DIM, KEY_TILE), BF16),
        pltpu.VMEM((n_tiles * KEY_TILE, nb * LANES), BF16),
        pltpu.VMEM((n_tiles, KEY_TILE, TQ), F32),
        pltpu.VMEM((A_KV_HEADS, KEY_TILE, A_REP * TQ), F32),
        pltpu.VMEM((A_KV_HEADS, KEY_TILE, A_REP * TQ), BF16),
        pltpu.VMEM((A_KV_HEADS, A_HEAD_DIM, A_REP * TQ), F32),
    ]


def _attn_prompt_kernel(q_ref, kv_ref, za_ref, o_ref, kt_out, vt_out, ikt_out, kb_ref, vt_ref, ikb_ref, *work,
                        seq, topk):
    i = pl.program_id(1)

    @pl.when(i == 0)
    def _():
        for kt in range(seq // KEY_TILE):
            rows = slice(kt * KEY_TILE, (kt + 1) * KEY_TILE)
            k, ik = kv_ref[rows, 0:LANES], kv_ref[rows, KV_IK:KV_IK + LANES]
            v_t = _to_cols(kv_ref[rows, KV_V:KV_V + LANES])
            _fill_keys(kt, 0, k, v_t, ik, kb_ref, vt_ref, ikb_ref)
            kt_out[:, rows] = _to_cols(k)
            vt_out[:, rows] = v_t
            ikt_out[:, rows] = _to_cols(ik)[0:IDX_DIM, :]

    TQ = TQ_PROMPT
    nt = (i * TQ + TQ + KEY_TILE - 1) // KEY_TILE
    out = _attend(q_ref[...], za_ref[...], kb_ref, vt_ref, ikb_ref, *work,
                  nt=nt, q_pos0=i * TQ, n_keys=seq, topk=topk)
    o_ref[...] = out.astype(BF16)


def _attn_prompt(pq, pkv, za, batch, seq):
    TQ = TQ_PROMPT
    nq = seq // TQ
    topk = min(TOPK_MAX, seq // 4)
    return pl.pallas_call(
        functools.partial(_attn_prompt_kernel, seq=seq, topk=topk),
        grid=(batch, nq),
        in_specs=[
            pl.BlockSpec((TQ, QW), lambda b, i: (b * nq + i, 0)),
            pl.BlockSpec((seq, KVW), lambda b, i: (b, 0)),
            pl.BlockSpec((TQ, A_WIDTH), lambda b, i: (b * nq + i, 0)),
        ],
        out_specs=[
            pl.BlockSpec((TQ, A_WIDTH), lambda b, i: (b * nq + i, 0)),
            pl.BlockSpec((A_KV_WIDTH, seq), lambda b, i: (b, 0)),
            pl.BlockSpec((A_KV_WIDTH, seq), lambda b, i: (b, 0)),
            pl.BlockSpec((IDX_DIM, seq), lambda b, i: (b, 0)),
        ],
        out_shape=[
            jax.ShapeDtypeStruct((batch * seq, A_WIDTH), BF16),
            jax.ShapeDtypeStruct((batch * A_KV_WIDTH, seq), F32),
            jax.ShapeDtypeStruct((batch * A_KV_WIDTH, seq), F32),
            jax.ShapeDtypeStruct((batch * IDX_DIM, seq), F32),
        ],
        scratch_shapes=_attn_scratch(seq // KEY_TILE, TQ, 1),
        compiler_params=_cparams(("parallel", "arbitrary")),
        name="attn_prompt",
    )(pq, pkv, za)


def _attn_sample_kernel(q_ref, kv_ref, za_ref, ck_ref, cv_ref, cik_ref, o_ref,
                        kb_ref, vt_ref, ikb_ref, *work, t, past, topk, nt, nb):
    tk = KEY_TILE
    tail = nt * tk - past
    zeros = lambda n, w: jnp.zeros((n, w), F32)
    pad_rows = lambda x: jnp.concatenate([x, zeros(tail - t, LANES)], axis=0)
    for j in range(nb):
        for kt in range(past // tk):
            cols = slice(kt * tk, (kt + 1) * tk)
            ik_t = jnp.concatenate([cik_ref[j * IDX_DIM:(j + 1) * IDX_DIM, cols], zeros(LANES - IDX_DIM, tk)], axis=0)
            _fill_keys(kt, j, _to_rows(ck_ref[j * LANES:(j + 1) * LANES, cols]),
                       cv_ref[j * LANES:(j + 1) * LANES, cols], _to_rows(ik_t), kb_ref, vt_ref, ikb_ref)
        new = kv_ref[j * t:(j + 1) * t, :]
        _fill_keys(past // tk, j, pad_rows(new[:, 0:LANES]), _to_cols(pad_rows(new[:, KV_V:KV_V + LANES])),
                   pad_rows(new[:, KV_IK:KV_IK + LANES]), kb_ref, vt_ref, ikb_ref)
    out = _attend(q_ref[...], za_ref[...], kb_ref, vt_ref, ikb_ref, *work,
                  nt=nt, q_pos0=past, n_keys=past + t, topk=topk, nb=nb)
    o_ref[...] = out.astype(BF16)


def _attn_sample(pq, pkv, za, ck, cv, cik, layer, row0, dec_batch, t):
    past = ck.shape[1]
    nb = TQ_SAMPLE // t
    cblk0 = layer * (dec_batch // nb)
    assert past % KEY_TILE == 0 and t <= KEY_TILE and TQ_SAMPLE % t == 0 and dec_batch % nb == 0
    topk = min(TOPK_MAX, (past + t) // 4)
    nt = past // KEY_TILE + 1
    blk0 = row0 // TQ_SAMPLE
    return pl.pallas_call(
        functools.partial(_attn_sample_kernel, t=t, past=past, topk=topk, nt=nt, nb=nb),
        grid=(dec_batch // nb,),
        in_specs=[
            pl.BlockSpec((TQ_SAMPLE, QW), lambda b: (blk0 + b, 0)),
            pl.BlockSpec((TQ_SAMPLE, KVW), lambda b: (blk0 + b, 0)),
            pl.BlockSpec((TQ_SAMPLE, A_WIDTH), lambda b: (blk0 + b, 0)),
            pl.BlockSpec((nb * A_KV_WIDTH, past), lambda b: (cblk0 + b, 0)),
            pl.BlockSpec((nb * A_KV_WIDTH, past), lambda b: (cblk0 + b, 0)),
            pl.BlockSpec((nb * IDX_DIM, past), lambda b: (cblk0 + b, 0)),
        ],
        out_specs=pl.BlockSpec((TQ_SAMPLE, A_WIDTH), lambda b: (b, 0)),
        out_shape=jax.ShapeDtypeStruct((dec_batch * t, A_WIDTH), BF16),
        scratch_shapes=_attn_scratch(nt, TQ_SAMPLE, nb),
        compiler_params=_cparams(("parallel",)),
        name="attn_sample",
    )(pq, pkv, za, ck, cv, cik)


GMLP_ROWS = 512


def _gmlp_kernel(uv_ref, zb_ref, g_ref, b_ref, w_ref, bs_ref, o_ref, *v_out, width, mask_chunks):
    gw = width // B_GROUPS
    ws = []
    for g in range(B_GROUPS):
        w = w_ref[g]
        if mask_chunks:
            i = lax.broadcasted_iota(jnp.int32, (B_CHUNK, B_CHUNK), 0)
            j = lax.broadcasted_iota(jnp.int32, (B_CHUNK, B_CHUNK), 1)
            w = jnp.where((j // CHUNK) <= (i // CHUNK), w, 0.0)
        ws.append(w.astype(BF16))
    for c in range(uv_ref.shape[0] // B_CHUNK):
        rows = slice(c * B_CHUNK, (c + 1) * B_CHUNK)
        uv = uv_ref[rows, :]
        act = 0.5 * uv * (1.0 + lax.erf(uv * np.float32(1.0 / np.sqrt(2.0))))
        u = act[:, :width]
        v = _ln(act[:, width:], g_ref[...], b_ref[...])
        if v_out:
            v_out[0][rows, :] = v
        vb = v.astype(BF16)
        zs = _silu(zb_ref[rows, :])
        for g in range(B_GROUPS):
            cols = slice(g * gw, (g + 1) * gw)
            mixed = _dot(ws[g], vb[:, cols]) + bs_ref[g]
            o_ref[rows, cols] = (u[:, cols] * mixed * zs[:, cols]).astype(BF16)


def _gmlp(uv, zb, ln_g, ln_b, w, bs, row0, rows, mask_chunks, want_v):
    width = zb.shape[1]
    t = GMLP_ROWS
    assert row0 % t == 0 and rows % t == 0
    blk0 = row0 // t
    out_specs = [pl.BlockSpec((t, width), lambda i: (i, 0))]
    out_shape = [jax.ShapeDtypeStruct((rows, width), BF16)]
    if want_v:
        out_specs.append(pl.BlockSpec((t, width), lambda i: (i, 0)))
        out_shape.append(jax.ShapeDtypeStruct((rows, width), F32))
    return pl.pallas_call(
        functools.partial(_gmlp_kernel, width=width, mask_chunks=mask_chunks),
        grid=(rows // t,),
        in_specs=[
            pl.BlockSpec((t, 2 * width), lambda i: (blk0 + i, 0)),
            pl.BlockSpec((t, width), lambda i: (blk0 + i, 0)),
            pl.BlockSpec((1, width), lambda i: (0, 0)),
            pl.BlockSpec((1, width), lambda i: (0, 0)),
            pl.BlockSpec((B_GROUPS, B_CHUNK, B_CHUNK), lambda i: (0, 0, 0)),
            pl.BlockSpec((B_GROUPS, B_CHUNK, 1), lambda i: (0, 0, 0)),
        ],
        out_specs=out_specs,
        out_shape=out_shape,
        compiler_params=_cparams(("parallel",)),
        name="gmlp_sample" if want_v else "gmlp_prompt",
    )(uv, zb, ln_g, ln_b, w, bs)


CONV_PAD = 32


CONV_ROWS = 64
SUBLANES = 8


def _conv_tile(glu_ref, zc_ref, o_ref, ext_ref, y_ref, w_ref, b_ref, g_ref, beta_ref, t, width):
    ext_ref[0, CONV_PAD:CONV_PAD + t, :] = glu_ref[:, :width] * jax.nn.sigmoid(glu_ref[:, width:])
    n = t + CONV_PAD - SUBLANES
    for s in range(1, SUBLANES):
        ext_ref[s, 0:n, :] = ext_ref[0, s:s + n, :]
    base = CONV_PAD - (C_CONV - 1)

    rc = min(CONV_ROWS, t)

    def rows_step(ci, carry):
        r0 = pl.multiple_of(ci * rc, rc)
        y = jnp.zeros((rc, width), F32)
        for k in range(C_CONV):
            s, q = (base + k) % SUBLANES, (base + k) // SUBLANES
            tap = w_ref[SUBLANES * k:SUBLANES * (k + 1), :]
            y = y + _per_rows(jnp.multiply, ext_ref[s, pl.ds(r0 + SUBLANES * q, rc), :], tap)
        y_ref[pl.ds(r0, rc), :] = y
        return carry

    lax.fori_loop(0, t // rc, rows_step, 0)
    y = y_ref[...] + b_ref[...]
    o_ref[...] = (_silu(_ln(y, g_ref[...], beta_ref[...])) * _silu(zc_ref[...])).astype(BF16)


def _conv_prompt_kernel(glu_ref, zc_ref, w_ref, b_ref, g_ref, beta_ref, o_ref, tail_ref, ext_ref, y_ref,
                        *, t, width):
    i = pl.program_id(1)

    @pl.when(i == 0)
    def _():
        ext_ref[0, 0:CONV_PAD, :] = jnp.zeros((CONV_PAD, width), F32)

    _conv_tile(glu_ref, zc_ref, o_ref, ext_ref, y_ref, w_ref, b_ref, g_ref, beta_ref, t, width)
    tail = ext_ref[0, t:t + CONV_PAD, :]
    ext_ref[0, 0:CONV_PAD, :] = tail

    @pl.when(i == pl.num_programs(1) - 1)
    def _():
        tail_ref[...] = tail[CONV_PAD - (C_CONV - 1):, :]


def _conv_prompt(glu, zc, w, b, ln_g, ln_b, batch, seq):
    t = ROW_TILE
    width = zc.shape[1]
    nb = seq // t
    return pl.pallas_call(
        functools.partial(_conv_prompt_kernel, t=t, width=width),
        grid=(batch, nb),
        in_specs=[
            pl.BlockSpec((t, 2 * width), lambda b_, i: (b_ * nb + i, 0)),
            pl.BlockSpec((t, width), lambda b_, i: (b_ * nb + i, 0)),
            pl.BlockSpec((SUBLANES * C_CONV, width), lambda b_, i: (0, 0)),
            pl.BlockSpec((1, width), lambda b_, i: (0, 0)),
            pl.BlockSpec((1, width), lambda b_, i: (0, 0)),
            pl.BlockSpec((1, width), lambda b_, i: (0, 0)),
        ],
        out_specs=[
            pl.BlockSpec((t, width), lambda b_, i: (b_ * nb + i, 0)),
            pl.BlockSpec((None, C_CONV - 1, width), lambda b_, i: (b_, 0, 0)),
        ],
        out_shape=[
            jax.ShapeDtypeStruct((batch * seq, width), BF16),
            jax.ShapeDtypeStruct((batch, C_CONV - 1, width), F32),
        ],
        scratch_shapes=[pltpu.VMEM((SUBLANES, CONV_PAD + t, width), F32), pltpu.VMEM((t, width), F32)],
        compiler_params=_cparams(("parallel", "arbitrary")),
        name="conv_prompt",
    )(glu, zc, w, b, ln_g, ln_b)


def _conv_sample_kernel(glu_ref, zc_ref, st_ref, w_ref, b_ref, g_ref, beta_ref, o_ref, tail_ref, ext_ref, y_ref,
                        *, t, width):
    base = CONV_PAD - (C_CONV - 1)
    ext_ref[0, 0:base, :] = jnp.zeros((base, width), F32)
    ext_ref[0, base:CONV_PAD, :] = st_ref[...]
    _conv_tile(glu_ref, zc_ref, o_ref, ext_ref, y_ref, w_ref, b_ref, g_ref, beta_ref, t, width)
    tail_ref[...] = ext_ref[0, t + base:t + CONV_PAD, :]


def _conv_sample(glu, zc, state, w, b, ln_g, ln_b, row0, dec_batch, t):
    width = zc.shape[1]
    blk0 = row0 // t
    return pl.pallas_call(
        functools.partial(_conv_sample_kernel, t=t, width=width),
        grid=(dec_batch,),
        in_specs=[
            pl.BlockSpec((t, 2 * width), lambda b_: (blk0 + b_, 0)),
            pl.BlockSpec((t, width), lambda b_: (blk0 + b_, 0)),
            pl.BlockSpec((None, C_CONV - 1, width), lambda b_: (b_, 0, 0)),
            pl.BlockSpec((SUBLANES * C_CONV, width), lambda b_: (0, 0)),
            pl.BlockSpec((1, width), lambda b_: (0, 0)),
            pl.BlockSpec((1, width), lambda b_: (0, 0)),
            pl.BlockSpec((1, width), lambda b_: (0, 0)),
        ],
        out_specs=[
            pl.BlockSpec((t, width), lambda b_: (b_, 0)),
            pl.BlockSpec((None, C_CONV - 1, width), lambda b_: (b_, 0, 0)),
        ],
        out_shape=[
            jax.ShapeDtypeStruct((dec_batch * t, width), BF16),
            jax.ShapeDtypeStruct((dec_batch, C_CONV - 1, width), F32),
        ],
        scratch_shapes=[pltpu.VMEM((SUBLANES, CONV_PAD + t, width), F32), pltpu.VMEM((t, width), F32)],
        compiler_params=_cparams(("parallel",)),
        name="conv_sample",
    )(glu, zc, state, w, b, ln_g, ln_b)


def _merge_kernel(a1, a2, b1, b2, c1, c2, x1, x2, p1, p2, gates_ref, wa_ref, wb_ref, wc_ref, wo_ref,
                  pg_ref, wpg_ref, wple_ref, fg_ref, *o_refs, d, final, n_first):
    pick = functools.partial(_pick, n_first)
    merged = (jax.nn.sigmoid(gates_ref[:, 0:d]) * _dot(pick(a1, a2), wa_ref[...])
              + jax.nn.sigmoid(gates_ref[:, d:2 * d]) * _dot(pick(b1, b2), wb_ref[...])
              + jax.nn.sigmoid(gates_ref[:, 2 * d:3 * d]) * _dot(pick(c1, c2), wc_ref[...]))
    x = pick(x1, x2) + _dot(merged.astype(BF16), wo_ref[...])
    gate = jax.nn.sigmoid(_dot(_rms(x, pg_ref[...]).astype(BF16), wpg_ref[...]))
    x = x + gate * _dot(pick(p1, p2).astype(BF16), wple_ref[...])
    if not final:
        o_refs[0][...] = x
        return
    y = _rms(x, fg_ref[...])
    on_first = pl.program_id(0) < n_first

    @pl.when(on_first)
    def _():
        o_refs[0][...] = y

    @pl.when(jnp.logical_not(on_first))
    def _():
        o_refs[1][...] = y


def _merge(a, b, c, x_pair, p_quad, gates, wa, wb, wc, wo, pg, wpg, wple, fg, final):
    x1, x2, n_first, x_off = x_pair
    p = p_quad[:2]
    m, d = gates.shape[0], x1.shape[1]
    t = ROW_TILE
    pair = lambda arrs, off=0, off1=0: _split_specs(t, arrs[0].shape[1], n_first, off, off1)
    full = lambda arr: pl.BlockSpec(arr.shape, lambda i: (0, 0), pipeline_mode=pl.Buffered(1))
    if final:
        out_specs = _split_specs(t, d, n_first, 0)
        out_shape = [jax.ShapeDtypeStruct((n_first * t, d), F32), jax.ShapeDtypeStruct((m - n_first * t, d), F32)]
    else:
        out_specs = [pl.BlockSpec((t, d), lambda i: (i, 0))]
        out_shape = [jax.ShapeDtypeStruct((m, d), F32)]
    return pl.pallas_call(
        functools.partial(_merge_kernel, d=d, final=final, n_first=n_first),
        grid=(m // t,),
        in_specs=(pair(a) + pair(b) + pair(c) + pair((x1, x2), x_off) + pair(p, p_quad[3], p_quad[2])
                  + [pl.BlockSpec((t, 3 * d), lambda i: (i, 0))]
                  + [full(wa), full(wb), full(wc), full(wo), full(pg), full(wpg), full(wple), full(fg)]),
        out_specs=out_specs,
        out_shape=out_shape,
        compiler_params=_cparams(("arbitrary",)),
        name="merge",
    )(*a, *b, *c, x1, x2, *p, gates, wa, wb, wc, wo, pg, wpg, wple, fg)


def kernel(x_prompt, x_sample, cache_k, cache_v, cache_idx_k, state_conv, p_prompt, p_sample, norm_g, w_in, gmlp_ln_g, gmlp_ln_b, gmlp_ws, gmlp_bs, conv_w, conv_b, conv_ln_g, conv_ln_b, w_branch_a, w_branch_b, w_branch_c, w_out, ple_norm_g, w_ple_gate, w_ple, final_norm_g):
    batch, seq, d = x_prompt.shape
    dec_batch, dec_seq, _ = x_sample.shape
    depth = w_in.shape[0]
    past = cache_k.shape[2]
    bw = gmlp_ln_g.shape[1]
    cw = conv_b.shape[1]
    mp = batch * seq
    ms = dec_batch * dec_seq
    assert mp % ROW_TILE == 0 and ms % ROW_TILE == 0 and seq % ROW_TILE == 0
    assert dec_seq <= CHUNK and B_CHUNK % dec_seq == 0 and past % CHUNK == 0

    widths = (A_WIDTH, A_KV_WIDTH, A_KV_WIDTH, IDX_HEADS * IDX_DIM, IDX_DIM, IDX_HEADS, A_WIDTH,
              2 * bw, bw, 2 * cw, cw, N_BRANCH * d)
    cuts = np.concatenate([[0], np.cumsum(widths)])
    out_widths = (QW, KVW, A_WIDTH, 2 * bw, bw, 2 * cw, cw, N_BRANCH * d)

    tail0 = QW + KVW
    placed = [(0, 0), (3, Q_IQ), (5, Q_IW), (1, QW), (2, QW + KV_V), (4, QW + KV_IK),
              (6, tail0)]
    pieces = [(int(cuts[j]), int(widths[j]), dst) for j, dst in placed[:-1]]
    pieces.append((int(cuts[6]), int(cuts[12] - cuts[6]), tail0))
    w1 = _wprep(jnp.transpose(w_in, (2, 0, 1)), tuple(pieces), sum(out_widths))

    n_first = mp // ROW_TILE
    x_pair = (x_prompt.reshape(mp, d), x_sample.reshape(ms, d), n_first, 0)
    rep = B_CHUNK // dec_seq
    eye = jnp.eye(rep, dtype=F32)
    p_all_prompt = p_prompt.reshape(depth * mp, -1)
    p_all_sample = p_sample.reshape(depth * ms, -1)
    ck_t = jnp.transpose(cache_k, (0, 1, 3, 4, 2)).reshape(depth * dec_batch * A_KV_WIDTH, past)
    cv_t = jnp.transpose(cache_v, (0, 1, 3, 4, 2)).reshape(depth * dec_batch * A_KV_WIDTH, past)
    cik_t = jnp.transpose(cache_idx_k, (0, 1, 3, 2)).reshape(depth * dec_batch * IDX_DIM, past)

    kp, vp, ikp, cvp, ks, vs, iks, cvs, gvs = ([] for _ in range(9))
    for l in range(depth):
        pq, pkv, za, uv, zb, glu, zc, gates = _proj(x_pair, mp + ms, norm_g[l][None, :], w1, l, out_widths)

        a_p, k_t, v_t, ik_t = _attn_prompt(pq, pkv, za, batch, seq)
        a_s = _attn_sample(pq, pkv, za, ck_t, cv_t, cik_t, l, mp, dec_batch, dec_seq)

        ln_g, ln_b = gmlp_ln_g[l][None, :], gmlp_ln_b[l][None, :]
        (b_p,) = _gmlp(uv, zb, ln_g, ln_b, gmlp_ws[l], gmlp_bs[l][:, :, None], 0, mp, True, False)
        ws_s = jnp.einsum('ab,gij->gaibj', eye, gmlp_ws[l][:, :dec_seq, :dec_seq]).reshape(
            B_GROUPS, B_CHUNK, B_CHUNK)
        bs_s = jnp.tile(gmlp_bs[l][:, :dec_seq], (1, rep))[:, :, None]
        b_s, gv = _gmlp(uv, zb, ln_g, ln_b, ws_s, bs_s, mp, ms, False, True)

        cargs = (jnp.repeat(conv_w[l], SUBLANES, axis=0), conv_b[l][None, :],
                 conv_ln_g[l][None, :], conv_ln_b[l][None, :])
        c_p, tail_p = _conv_prompt(glu, zc, *cargs, batch, seq)
        c_s, tail_s = _conv_sample(glu, zc, state_conv[l], *cargs, mp, dec_batch, dec_seq)

        p = (p_all_prompt, p_all_sample, l * n_first, l * (ms // ROW_TILE))
        outs = _merge((a_p, a_s), (b_p, b_s), (c_p, c_s), x_pair, p, gates,
                      w_branch_a[l].astype(BF16), w_branch_b[l].astype(BF16), w_branch_c[l].astype(BF16),
                      w_out[l].astype(BF16), ple_norm_g[l][None, :], w_ple_gate[l].astype(BF16),
                      w_ple[l].astype(BF16), final_norm_g[None, :], l == depth - 1)
        x_pair = (outs[0], outs[0], n_first, n_first)

        heads_last = lambda x_t: jnp.transpose(x_t.reshape(batch, A_KV_HEADS, A_HEAD_DIM, seq), (0, 3, 1, 2))
        kp.append(heads_last(k_t))
        vp.append(heads_last(v_t))
        ikp.append(jnp.transpose(ik_t.reshape(batch, IDX_DIM, seq), (0, 2, 1)))
        cvp.append(tail_p)
        ks.append(pkv[mp:, 0:A_KV_WIDTH].reshape(dec_batch, dec_seq, A_KV_HEADS, A_HEAD_DIM))
        vs.append(pkv[mp:, KV_V:KV_V + A_KV_WIDTH].reshape(dec_batch, dec_seq, A_KV_HEADS, A_HEAD_DIM))
        iks.append(pkv[mp:, KV_IK:KV_IK + IDX_DIM].reshape(dec_batch, dec_seq, IDX_DIM))
        cvs.append(tail_s)
        gvs.append(gv.reshape(dec_batch, dec_seq, bw))

    return (outs[0].reshape(batch, seq, d), outs[1].reshape(dec_batch, dec_seq, d),
            jnp.stack(kp), jnp.stack(vp), jnp.stack(ikp), jnp.stack(cvp),
            jnp.stack(ks), jnp.stack(vs), jnp.stack(iks), jnp.stack(cvs), jnp.stack(gvs))
```

```python
import functools

import jax
import jax.numpy as jnp
import numpy as np
from jax import lax
from jax.experimental import pallas as pl
from jax.experimental.pallas import tpu as pltpu

F32 = jnp.float32
BF16 = jnp.bfloat16

CHUNK = 64
EPS = 1e-6
A_HEAD_DIM = 64
A_HEADS = 8
A_KV_HEADS = 2
A_REP = A_HEADS // A_KV_HEADS
A_WIDTH = A_HEADS * A_HEAD_DIM
A_KV_WIDTH = A_KV_HEADS * A_HEAD_DIM
IDX_HEADS = 4
IDX_DIM = 64
IDX_W_SCALE = (IDX_HEADS * IDX_DIM) ** -0.5
TOPK_MAX = 256
B_GROUPS = 4
B_CHUNK = 128
C_CONV = 31
N_BRANCH = 3

QW = 896
KVW = 384
Q_IQ = A_WIDTH
Q_IW = A_WIDTH + IDX_HEADS * IDX_DIM
KV_V = A_KV_WIDTH
KV_IK = 2 * A_KV_WIDTH

VMEM_LIMIT = 56 * 1024 * 1024
ROW_TILE = 256
KEY_TILE = 256
BISECT_STEPS = 16
NEG_INF = float("-inf")


def _cparams(sem):
    return pltpu.CompilerParams(dimension_semantics=sem, vmem_limit_bytes=VMEM_LIMIT)


def _rms(xf, g):
    return xf * lax.rsqrt(jnp.mean(jnp.square(xf), axis=-1, keepdims=True) + EPS) * g


def _ln(xf, g, b):
    xc = xf - jnp.mean(xf, axis=-1, keepdims=True)
    var = jnp.mean(jnp.square(xc), axis=-1, keepdims=True)
    return xc * lax.rsqrt(var + EPS) * g + b


def _silu(x):
    return x * jax.nn.sigmoid(x)


def _dot(a, b):
    return jnp.dot(a, b, preferred_element_type=F32)


def _dot_nt(a, b):
    return lax.dot_general(a, b, (((1,), (1,)), ((), ())), preferred_element_type=F32)


WPREP_ROWS = 128


def _wprep_kernel(src_ref, cnt_ref, w_ref, o_ref):
    i = pl.program_id(0)
    keep = lax.broadcasted_iota(jnp.int32, (WPREP_ROWS, w_ref.shape[2]), 0) < cnt_ref[i]
    for l in range(w_ref.shape[1]):
        o_ref[l] = jnp.where(keep, w_ref[:, l, :], 0.0).astype(BF16)


def _wprep(w_t, pieces, n_out):
    n, depth, d = w_t.shape
    src, cnt = [], []
    for dst0 in range(0, n_out, WPREP_ROWS):
        hit = [(s + dst0 - t, min(WPREP_ROWS, t + wd - dst0)) for s, wd, t in pieces if t <= dst0 < t + wd]
        assert len(hit) == 1 and hit[0][0] + WPREP_ROWS <= n
        src.append(hit[0][0])
        cnt.append(hit[0][1])
    grid_spec = pltpu.PrefetchScalarGridSpec(
        num_scalar_prefetch=2,
        grid=(n_out // WPREP_ROWS,),
        in_specs=[pl.BlockSpec((pl.Element(WPREP_ROWS), pl.Element(depth), pl.Element(d)),
                               lambda i, src_ref, cnt_ref: (src_ref[i], 0, 0))],
        out_specs=pl.BlockSpec((depth, WPREP_ROWS, d), lambda i, src_ref, cnt_ref: (0, i, 0)),
    )
    return pl.pallas_call(
        _wprep_kernel,
        grid_spec=grid_spec,
        out_shape=jax.ShapeDtypeStruct((depth, n_out, d), BF16),
        compiler_params=_cparams(("parallel",)),
        name="wprep",
    )(jnp.asarray(src, jnp.int32), jnp.asarray(cnt, jnp.int32), w_t)


def _split_specs(t, width, n_first, second_off, first_off=0):
    first = pl.BlockSpec((t, width), lambda i: (jnp.minimum(i, n_first - 1) + first_off, 0))
    second = pl.BlockSpec((t, width), lambda i: (jnp.maximum(i - n_first, 0) + second_off, 0))
    return [first, second]


def _pick(n_first, first_ref, second_ref):
    return jnp.where(pl.program_id(0) < n_first, first_ref[...], second_ref[...])


def _proj_kernel(x1_ref, x2_ref, g_ref, w_ref, *out_refs, widths, n_first):
    h = _rms(_pick(n_first, x1_ref, x2_ref), g_ref[...]).astype(BF16)
    off = 0
    for o_ref, wd in zip(out_refs, widths):
        for c in range(0, wd, 512):
            cw = min(512, wd - c)
            o_ref[:, c:c + cw] = _dot_nt(h, w_ref[off + c:off + c + cw, :])
        off += wd


def _proj(x_pair, m, g, w, layer, widths):
    x1, x2, n_first, second_off = x_pair
    d = x1.shape[1]
    n = w.shape[1]
    return pl.pallas_call(
        functools.partial(_proj_kernel, widths=widths, n_first=n_first),
        grid=(m // ROW_TILE,),
        in_specs=_split_specs(ROW_TILE, d, n_first, second_off) + [
            pl.BlockSpec((1, d), lambda i: (0, 0)),
            pl.BlockSpec((None, n, d), lambda i: (layer, 0, 0), pipeline_mode=pl.Buffered(1)),
        ],
        out_specs=[pl.BlockSpec((ROW_TILE, wd), lambda i: (i, 0)) for wd in widths],
        out_shape=[jax.ShapeDtypeStruct((m, wd), F32) for wd in widths],
        compiler_params=_cparams(("parallel",)),
        name="proj",
    )(x1, x2, g, w)


LANES = 128
TQ_PROMPT = 256
TQ_SAMPLE = 128


def _rows8(x):
    return [x[8 * j:8 * j + 8] for j in range(x.shape[0] // 8)]


def _tree(parts, op):
    while len(parts) > 1:
        nxt = [op(parts[2 * j], parts[2 * j + 1]) for j in range(len(parts) // 2)]
        if len(parts) % 2:
            nxt.append(parts[-1])
        parts = nxt
    return parts[0]


def _all8(x, op):
    for shift in (4, 2, 1):
        x = op(x, pltpu.roll(x, shift, 0))
    return x


def _fold(x, op):
    return _all8(_tree(_rows8(x), op), op)


def _per_rows(op, x, v8):
    return jnp.concatenate([op(part, v8) for part in _rows8(x)], axis=0)


def _head_slabs(x, n, dst):
    t = x.shape[0]
    keep = (lax.broadcasted_iota(jnp.int32, (t, LANES), 1) // A_HEAD_DIM) == dst
    parts = []
    for h in range(n):
        slab = x[:, (h // 2) * LANES:(h // 2 + 1) * LANES]
        if h % 2 != dst:
            slab = pltpu.roll(slab, A_HEAD_DIM, 1)
        parts.append(jnp.where(keep, slab, 0.0))
    return parts


def _to_cols(x):
    return jnp.concatenate([x[c:c + LANES, :].T for c in range(0, KEY_TILE, LANES)], axis=1)


def _to_rows(x_t):
    return jnp.concatenate([x_t[:, c:c + LANES].T for c in range(0, KEY_TILE, LANES)], axis=0)


def _fill_keys(kt, j, k, v_t, ik, kb_ref, vt_ref, ikb_ref):
    rows = pl.ds(kt * KEY_TILE, KEY_TILE)
    kb_ref[rows, j * LANES:(j + 1) * LANES] = k.astype(BF16)
    ikb_ref[rows, j * LANES:(j + 1) * LANES] = ik.astype(BF16)
    for g in range(A_KV_HEADS):
        vt_ref[kt, g, j * A_HEAD_DIM:(j + 1) * A_HEAD_DIM, :] = (
            v_t[g * A_HEAD_DIM:(g + 1) * A_HEAD_DIM, :].astype(BF16))


def _attend(qrow, za, kb_ref, vt_ref, ikb_ref, score_ref, lg_ref, p_ref, acc_ref, *, nt, q_pos0, n_keys, topk, nb=1):
    tk = KEY_TILE
    TQ = qrow.shape[0]
    tqb = TQ // nb

    def spread(x):
        if nb == 1:
            return x
        row_set = lax.broadcasted_iota(jnp.int32, x.shape, 0) // tqb
        return jnp.concatenate([jnp.where(row_set == j, x, 0.0) for j in range(nb)], axis=1)

    iq_all = jnp.concatenate([spread(s).astype(BF16) for s in _head_slabs(qrow[:, Q_IQ:Q_IW], IDX_HEADS, 0)],
                             axis=0)
    iw_t = qrow[:, Q_IW:Q_IW + LANES].T * IDX_W_SCALE
    iw_rows = [iw_t[h:h + 1, :] for h in range(IDX_HEADS)]

    k_row = lax.broadcasted_iota(jnp.int32, (tk, TQ), 0)
    q_pos = q_pos0 + lax.broadcasted_iota(jnp.int32, (tk, TQ), 1) % tqb
    n_vis_q = jnp.minimum((q_pos // CHUNK + 1) * CHUNK, n_keys)

    def visible(kt):
        k_pos = kt * tk + k_row
        return k_pos < n_vis_q, k_pos

    def score_body(kt, carry):
        mn, mx = carry
        ik_t = ikb_ref[pl.ds(pl.multiple_of(kt * tk, tk), tk), :]
        r = _dot_nt(ik_t, iq_all)
        s = jnp.zeros((tk, TQ), F32)
        for h in range(IDX_HEADS):
            s = s + jnp.maximum(r[:, h * TQ:(h + 1) * TQ], 0.0) * iw_rows[h]
        vis, _ = visible(kt)
        masked = jnp.where(vis, s, NEG_INF)
        score_ref[kt] = masked
        mn = jnp.minimum(mn, _tree(_rows8(jnp.where(vis, s, jnp.inf)), jnp.minimum))
        mx = jnp.maximum(mx, _tree(_rows8(masked), jnp.maximum))
        return mn, mx

    mn, mx = lax.fori_loop(0, nt, score_body,
                           (jnp.full((8, TQ), jnp.inf, F32), jnp.full((8, TQ), NEG_INF, F32)))
    mx = _all8(mx, jnp.maximum)

    q_pos8 = q_pos0 + lax.broadcasted_iota(jnp.int32, (8, TQ), 1) % tqb
    n_vis = jnp.minimum((q_pos8 // CHUNK + 1) * CHUNK, n_keys)
    k_eff = jnp.minimum(n_vis, topk).astype(F32)

    def count_ge(thr):
        def body(kt, acc):
            return acc + _tree([jnp.where(part >= thr, 1.0, 0.0) for part in _rows8(score_ref[kt])], jnp.add)
        return _all8(lax.fori_loop(0, nt, body, jnp.zeros((8, TQ), F32)), jnp.add)

    def bisect_step(_, state):
        lo, hi, c_lo, c_hi = state
        mid = lo + (hi - lo) * 0.5
        inside = (mid > lo) & (mid < hi)
        c = count_ge(mid)
        up = inside & (c >= k_eff)
        dn = inside & (c < k_eff)
        return (jnp.where(up, mid, lo), jnp.where(dn, mid, hi), jnp.where(up, c, c_lo), jnp.where(dn, c, c_hi))

    def band_max(lo, hi):
        def body(kt, acc):
            parts = [jnp.where((part >= lo) & (part < hi), part, NEG_INF) for part in _rows8(score_ref[kt])]
            return jnp.maximum(acc, _tree(parts, jnp.maximum))
        return _all8(lax.fori_loop(0, nt, body, jnp.full((8, TQ), NEG_INF, F32)), jnp.maximum)

    def peel_step(state):
        it, _, lo, hi, c_hi, done = state
        top = band_max(lo, hi)
        c_top = count_ge(top)
        hit = c_top >= k_eff
        cut = jnp.logical_not(hit) & (done == 0.0)
        lo = jnp.where(hit & (done == 0.0), top, lo)
        hi = jnp.where(cut, top, hi)
        c_hi = jnp.where(cut, c_top, c_hi)
        done = jnp.where(hit, 1.0, done)
        return it + 1, jnp.max(1.0 - done).astype(F32), lo, hi, c_hi, done

    lo0 = _all8(mn, jnp.minimum)
    hi0 = mx + jnp.maximum(jnp.abs(mx), 1e-30) * (2.0 ** -10)
    lo, hi, c_lo, c_hi = lax.fori_loop(
        0, BISECT_STEPS, bisect_step, (lo0, hi0, n_vis.astype(F32), jnp.zeros((8, TQ), F32)))
    done0 = jnp.where(c_lo == k_eff, 1.0, 0.0)
    init = (jnp.int32(0), jnp.max(1.0 - done0).astype(F32), lo, hi, c_hi, done0)
    _, _, lo, hi, c_hi, _ = lax.while_loop(lambda st: (st[0] < n_keys) & (st[1] > 0.0), peel_step, init)
    lo_r, hi_r = lo[0:1], hi[0:1]
    need_r = (k_eff - c_hi)[0:1]

    tril = (lax.broadcasted_iota(jnp.int32, (tk, tk), 1)
            <= lax.broadcasted_iota(jnp.int32, (tk, tk), 0)).astype(BF16)

    def select_body(kt, carry):
        s = score_ref[kt]
        inb = (s >= lo_r) & (s < hi_r)
        band = jnp.where(inb, 1.0, 0.0)
        rank = _dot(tril, band.astype(BF16)) + carry
        sel = (s >= hi_r) | (inb & (rank <= need_r))
        _, k_pos = visible(kt)
        dist = jnp.abs(q_pos - k_pos).astype(F32)
        score_ref[kt] = jnp.where(sel, -dist, NEG_INF)
        return carry + jnp.sum(band, axis=0, keepdims=True)

    lax.fori_loop(0, nt, select_body, jnp.zeros((1, TQ), F32))

    slopes = [2.0 ** (-8.0 * (h + 1) / A_HEADS) for h in range(A_HEADS)]
    gw = A_REP * A_HEAD_DIM
    q_gs = []
    for g in range(A_KV_HEADS):
        q_g = jnp.concatenate([spread(s) for s in _head_slabs(qrow[:, g * gw:(g + 1) * gw], A_REP, g)], axis=0)
        q_gs.append((q_g * (A_HEAD_DIM ** -0.5)).astype(BF16))
    acc_ref[...] = jnp.zeros(acc_ref.shape, F32)
    lane_set = (lax.broadcasted_iota(jnp.int32, (A_HEAD_DIM, A_REP * TQ), 1) % TQ) // tqb

    last = A_KV_HEADS - 1
    p_ref[last] = jnp.zeros(p_ref.shape[1:], BF16)

    def add_values(kt, g, corr_all):
        pv = _dot(vt_ref[kt, g], p_ref[g])
        if nb > 1:
            pv = sum(jnp.where(lane_set == j, pv[j * A_HEAD_DIM:(j + 1) * A_HEAD_DIM], 0.0) for j in range(nb))
        acc_ref[g] = _per_rows(jnp.multiply, acc_ref[g], corr_all) + pv

    def attn_body(kt, carry):
        ms, ls, corr_last = carry
        k_t = kb_ref[pl.ds(pl.multiple_of(kt * tk, tk), tk), :]
        nd = score_ref[kt]
        for g in range(A_KV_HEADS):
            lg_ref[g] = _dot_nt(k_t, q_gs[g])
        add_values(jnp.maximum(kt - 1, 0), last, corr_last)
        ms_new, ls_new = [], []
        for g in range(A_KV_HEADS):
            corrs = []
            for r in range(A_REP):
                h = g * A_REP + r
                lgr = lg_ref[g, :, r * TQ:(r + 1) * TQ] + slopes[h] * nd
                m_new = jnp.maximum(ms[h], _fold(lgr, jnp.maximum))
                m_safe = jnp.where(m_new == NEG_INF, 0.0, m_new)
                p = _per_rows(lambda a, m: jnp.exp(a - m), lgr, m_safe)
                corr = jnp.exp(ms[h] - m_safe)
                ls_new.append(ls[h] * corr + _fold(p, jnp.add))
                ms_new.append(m_new)
                corrs.append(corr)
                p_ref[g, :, r * TQ:(r + 1) * TQ] = p.astype(BF16)
            corr_all = jnp.concatenate(corrs, axis=1)
            if g < last:
                add_values(kt, g, corr_all)
        return tuple(ms_new), tuple(ls_new), corr_all

    init = (tuple(jnp.full((8, TQ), NEG_INF, F32) for _ in range(A_HEADS)),
            tuple(jnp.zeros((8, TQ), F32) for _ in range(A_HEADS)),
            jnp.ones((8, A_REP * TQ), F32))
    _, ls, corr_last = lax.fori_loop(0, nt, attn_body, init)
    add_values(nt - 1, last, corr_last)
    pieces = []
    for g in range(A_KV_HEADS):
        o_t = _per_rows(jnp.divide, acc_ref[g], jnp.concatenate(ls[g * A_REP:(g + 1) * A_REP], axis=1))
        for r in range(0, A_REP, 2):
            pair = jnp.concatenate([o_t[:, r * TQ:(r + 1) * TQ], o_t[:, (r + 1) * TQ:(r + 2) * TQ]], axis=0)
            pieces.append(pair.T)
    o_a = jnp.concatenate(pieces, axis=-1)
    return o_a * _silu(za)


def _attn_scratch(n_tiles, TQ, nb):
    return [
        pltpu.VMEM((n_tiles * KEY_TILE, nb * LANES), BF16),
        pltpu.VMEM((n_tiles, A_KV_HEADS, nb * A_HEAD_DIM, KEY_TILE), BF16),
        pltpu.VMEM((n_tiles * KEY_TILE, nb * LANES), BF16),
        pltpu.VMEM((n_tiles, KEY_TILE, TQ), F32),
        pltpu.VMEM((A_KV_HEADS, KEY_TILE, A_REP * TQ), F32),
        pltpu.VMEM((A_KV_HEADS, KEY_TILE, A_REP * TQ), BF16),
        pltpu.VMEM((A_KV_HEADS, A_HEAD_DIM, A_REP * TQ), F32),
    ]


def _attn_prompt_kernel(q_ref, kv_ref, za_ref, o_ref, kt_out, vt_out, ikt_out, kb_ref, vt_ref, ikb_ref, *work,
                        seq, topk):
    i = pl.program_id(1)

    @pl.when(i == 0)
    def _():
        for kt in range(seq // KEY_TILE):
            rows = slice(kt * KEY_TILE, (kt + 1) * KEY_TILE)
            k, ik = kv_ref[rows, 0:LANES], kv_ref[rows, KV_IK:KV_IK + LANES]
            v_t = _to_cols(kv_ref[rows, KV_V:KV_V + LANES])
            _fill_keys(kt, 0, k, v_t, ik, kb_ref, vt_ref, ikb_ref)
            kt_out[:, rows] = _to_cols(k)
            vt_out[:, rows] = v_t
            ikt_out[:, rows] = _to_cols(ik)[0:IDX_DIM, :]

    TQ = TQ_PROMPT
    nt = (i * TQ + TQ + KEY_TILE - 1) // KEY_TILE
    out = _attend(q_ref[...], za_ref[...], kb_ref, vt_ref, ikb_ref, *work,
                  nt=nt, q_pos0=i * TQ, n_keys=seq, topk=topk)
    o_ref[...] = out.astype(BF16)


def _attn_prompt(pq, pkv, za, batch, seq):
    TQ = TQ_PROMPT
    nq = seq // TQ
    topk = min(TOPK_MAX, seq // 4)
    return pl.pallas_call(
        functools.partial(_attn_prompt_kernel, seq=seq, topk=topk),
        grid=(batch, nq),
        in_specs=[
            pl.BlockSpec((TQ, QW), lambda b, i: (b * nq + i, 0)),
            pl.BlockSpec((seq, KVW), lambda b, i: (b, 0)),
            pl.BlockSpec((TQ, A_WIDTH), lambda b, i: (b * nq + i, 0)),
        ],
        out_specs=[
            pl.BlockSpec((TQ, A_WIDTH), lambda b, i: (b * nq + i, 0)),
            pl.BlockSpec((A_KV_WIDTH, seq), lambda b, i: (b, 0)),
            pl.BlockSpec((A_KV_WIDTH, seq), lambda b, i: (b, 0)),
            pl.BlockSpec((IDX_DIM, seq), lambda b, i: (b, 0)),
        ],
        out_shape=[
            jax.ShapeDtypeStruct((batch * seq, A_WIDTH), BF16),
            jax.ShapeDtypeStruct((batch * A_KV_WIDTH, seq), F32),
            jax.ShapeDtypeStruct((batch * A_KV_WIDTH, seq), F32),
            jax.ShapeDtypeStruct((batch * IDX_DIM, seq), F32),
        ],
        scratch_shapes=_attn_scratch(seq // KEY_TILE, TQ, 1),
        compiler_params=_cparams(("parallel", "arbitrary")),
        name="attn_prompt",
    )(pq, pkv, za)


def _attn_sample_kernel(q_ref, kv_ref, za_ref, ck_ref, cv_ref, cik_ref, o_ref,
                        kb_ref, vt_ref, ikb_ref, *work, t, past, topk, nt, nb):
    tk = KEY_TILE
    tail = nt * tk - past
    zeros = lambda n, w: jnp.zeros((n, w), F32)
    pad_rows = lambda x: jnp.concatenate([x, zeros(tail - t, LANES)], axis=0)
    for j in range(nb):
        for kt in range(past // tk):
            cols = slice(kt * tk, (kt + 1) * tk)
            ik_t = jnp.concatenate([cik_ref[j * IDX_DIM:(j + 1) * IDX_DIM, cols], zeros(LANES - IDX_DIM, tk)], axis=0)
            _fill_keys(kt, j, _to_rows(ck_ref[j * LANES:(j + 1) * LANES, cols]),
                       cv_ref[j * LANES:(j + 1) * LANES, cols], _to_rows(ik_t), kb_ref, vt_ref, ikb_ref)
        new = kv_ref[j * t:(j + 1) * t, :]
        _fill_keys(past // tk, j, pad_rows(new[:, 0:LANES]), _to_cols(pad_rows(new[:, KV_V:KV_V + LANES])),
                   pad_rows(new[:, KV_IK:KV_IK + LANES]), kb_ref, vt_ref, ikb_ref)
    out = _attend(q_ref[...], za_ref[...], kb_ref, vt_ref, ikb_ref, *work,
                  nt=nt, q_pos0=past, n_keys=past + t, topk=topk, nb=nb)
    o_ref[...] = out.astype(BF16)


def _attn_sample(pq, pkv, za, ck, cv, cik, layer, row0, dec_batch, t):
    past = ck.shape[1]
    nb = TQ_SAMPLE // t
    cblk0 = layer * (dec_batch // nb)
    assert past % KEY_TILE == 0 and t <= KEY_TILE and TQ_SAMPLE % t == 0 and dec_batch % nb == 0
    topk = min(TOPK_MAX, (past + t) // 4)
    nt = past // KEY_TILE + 1
    blk0 = row0 // TQ_SAMPLE
    return pl.pallas_call(
        functools.partial(_attn_sample_kernel, t=t, past=past, topk=topk, nt=nt, nb=nb),
        grid=(dec_batch // nb,),
        in_specs=[
            pl.BlockSpec((TQ_SAMPLE, QW), lambda b: (blk0 + b, 0)),
            pl.BlockSpec((TQ_SAMPLE, KVW), lambda b: (blk0 + b, 0)),
            pl.BlockSpec((TQ_SAMPLE, A_WIDTH), lambda b: (blk0 + b, 0)),
            pl.BlockSpec((nb * A_KV_WIDTH, past), lambda b: (cblk0 + b, 0)),
            pl.BlockSpec((nb * A_KV_WIDTH, past), lambda b: (cblk0 + b, 0)),
            pl.BlockSpec((nb * IDX_DIM, past), lambda b: (cblk0 + b, 0)),
        ],
        out_specs=pl.BlockSpec((TQ_SAMPLE, A_WIDTH), lambda b: (b, 0)),
        out_shape=jax.ShapeDtypeStruct((dec_batch * t, A_WIDTH), BF16),
        scratch_shapes=_attn_scratch(nt, TQ_SAMPLE, nb),
        compiler_params=_cparams(("parallel",)),
        name="attn_sample",
    )(pq, pkv, za, ck, cv, cik)


GMLP_ROWS = 512


def _gmlp_kernel(uv_ref, zb_ref, g_ref, b_ref, w_ref, bs_ref, o_ref, *v_out, width, mask_chunks):
    gw = width // B_GROUPS
    ws = []
    for g in range(B_GROUPS):
        w = w_ref[g]
        if mask_chunks:
            i = lax.broadcasted_iota(jnp.int32, (B_CHUNK, B_CHUNK), 0)
            j = lax.broadcasted_iota(jnp.int32, (B_CHUNK, B_CHUNK), 1)
            w = jnp.where((j // CHUNK) <= (i // CHUNK), w, 0.0)
        ws.append(w.astype(BF16))
    for c in range(uv_ref.shape[0] // B_CHUNK):
        rows = slice(c * B_CHUNK, (c + 1) * B_CHUNK)
        uv = uv_ref[rows, :]
        act = 0.5 * uv * (1.0 + lax.erf(uv * np.float32(1.0 / np.sqrt(2.0))))
        u = act[:, :width]
        v = _ln(act[:, width:], g_ref[...], b_ref[...])
        if v_out:
            v_out[0][rows, :] = v
        vb = v.astype(BF16)
        zs = _silu(zb_ref[rows, :])
        for g in range(B_GROUPS):
            cols = slice(g * gw, (g + 1) * gw)
            mixed = _dot(ws[g], vb[:, cols]) + bs_ref[g]
            o_ref[rows, cols] = (u[:, cols] * mixed * zs[:, cols]).astype(BF16)


def _gmlp(uv, zb, ln_g, ln_b, w, bs, row0, rows, mask_chunks, want_v):
    width = zb.shape[1]
    t = GMLP_ROWS
    assert row0 % t == 0 and rows % t == 0
    blk0 = row0 // t
    out_specs = [pl.BlockSpec((t, width), lambda i: (i, 0))]
    out_shape = [jax.ShapeDtypeStruct((rows, width), BF16)]
    if want_v:
        out_specs.append(pl.BlockSpec((t, width), lambda i: (i, 0)))
        out_shape.append(jax.ShapeDtypeStruct((rows, width), F32))
    return pl.pallas_call(
        functools.partial(_gmlp_kernel, width=width, mask_chunks=mask_chunks),
        grid=(rows // t,),
        in_specs=[
            pl.BlockSpec((t, 2 * width), lambda i: (blk0 + i, 0)),
            pl.BlockSpec((t, width), lambda i: (blk0 + i, 0)),
            pl.BlockSpec((1, width), lambda i: (0, 0)),
            pl.BlockSpec((1, width), lambda i: (0, 0)),
            pl.BlockSpec((B_GROUPS, B_CHUNK, B_CHUNK), lambda i: (0, 0, 0)),
            pl.BlockSpec((B_GROUPS, B_CHUNK, 1), lambda i: (0, 0, 0)),
        ],
        out_specs=out_specs,
        out_shape=out_shape,
        compiler_params=_cparams(("parallel",)),
        name="gmlp_sample" if want_v else "gmlp_prompt",
    )(uv, zb, ln_g, ln_b, w, bs)


CONV_PAD = 32


CONV_ROWS = 64
SUBLANES = 8


def _conv_tile(glu_ref, zc_ref, o_ref, ext_ref, y_ref, w_ref, b_ref, g_ref, beta_ref, t, width):
    ext_ref[0, CONV_PAD:CONV_PAD + t, :] = glu_ref[:, :width] * jax.nn.sigmoid(glu_ref[:, width:])
    n = t + CONV_PAD - SUBLANES
    for s in range(1, SUBLANES):
        ext_ref[s, 0:n, :] = ext_ref[0, s:s + n, :]
    base = CONV_PAD - (C_CONV - 1)

    rc = min(CONV_ROWS, t)

    def rows_step(ci, carry):
        r0 = pl.multiple_of(ci * rc, rc)
        y = jnp.zeros((rc, width), F32)
        for k in range(C_CONV):
            s, q = (base + k) % SUBLANES, (base + k) // SUBLANES
            tap = w_ref[SUBLANES * k:SUBLANES * (k + 1), :]
            y = y + _per_rows(jnp.multiply, ext_ref[s, pl.ds(r0 + SUBLANES * q, rc), :], tap)
        y_ref[pl.ds(r0, rc), :] = y
        return carry

    lax.fori_loop(0, t // rc, rows_step, 0)
    y = y_ref[...] + b_ref[...]
    o_ref[...] = (_silu(_ln(y, g_ref[...], beta_ref[...])) * _silu(zc_ref[...])).astype(BF16)


def _conv_prompt_kernel(glu_ref, zc_ref, w_ref, b_ref, g_ref, beta_ref, o_ref, tail_ref, ext_ref, y_ref,
                        *, t, width):
    i = pl.program_id(1)

    @pl.when(i == 0)
    def _():
        ext_ref[0, 0:CONV_PAD, :] = jnp.zeros((CONV_PAD, width), F32)

    _conv_tile(glu_ref, zc_ref, o_ref, ext_ref, y_ref, w_ref, b_ref, g_ref, beta_ref, t, width)
    tail = ext_ref[0, t:t + CONV_PAD, :]
    ext_ref[0, 0:CONV_PAD, :] = tail

    @pl.when(i == pl.num_programs(1) - 1)
    def _():
        tail_ref[...] = tail[CONV_PAD - (C_CONV - 1):, :]


def _conv_prompt(glu, zc, w, b, ln_g, ln_b, batch, seq):
    t = ROW_TILE
    width = zc.shape[1]
    nb = seq // t
    return pl.pallas_call(
        functools.partial(_conv_prompt_kernel, t=t, width=width),
        grid=(batch, nb),
        in_specs=[
            pl.BlockSpec((t, 2 * width), lambda b_, i: (b_ * nb + i, 0)),
            pl.BlockSpec((t, width), lambda b_, i: (b_ * nb + i, 0)),
            pl.BlockSpec((SUBLANES * C_CONV, width), lambda b_, i: (0, 0)),
            pl.BlockSpec((1, width), lambda b_, i: (0, 0)),
            pl.BlockSpec((1, width), lambda b_, i: (0, 0)),
            pl.BlockSpec((1, width), lambda b_, i: (0, 0)),
        ],
        out_specs=[
            pl.BlockSpec((t, width), lambda b_, i: (b_ * nb + i, 0)),
            pl.BlockSpec((None, C_CONV - 1, width), lambda b_, i: (b_, 0, 0)),
        ],
        out_shape=[
            jax.ShapeDtypeStruct((batch * seq, width), BF16),
            jax.ShapeDtypeStruct((batch, C_CONV - 1, width), F32),
        ],
        scratch_shapes=[pltpu.VMEM((SUBLANES, CONV_PAD + t, width), F32), pltpu.VMEM((t, width), F32)],
        compiler_params=_cparams(("parallel", "arbitrary")),
        name="conv_prompt",
    )(glu, zc, w, b, ln_g, ln_b)


def _conv_sample_kernel(glu_ref, zc_ref, st_ref, w_ref, b_ref, g_ref, beta_ref, o_ref, tail_ref, ext_ref, y_ref,
                        *, t, width):
    base = CONV_PAD - (C_CONV - 1)
    ext_ref[0, 0:base, :] = jnp.zeros((base, width), F32)
    ext_ref[0, base:CONV_PAD, :] = st_ref[...]
    _conv_tile(glu_ref, zc_ref, o_ref, ext_ref, y_ref, w_ref, b_ref, g_ref, beta_ref, t, width)
    tail_ref[...] = ext_ref[0, t + base:t + CONV_PAD, :]


def _conv_sample(glu, zc, state, w, b, ln_g, ln_b, row0, dec_batch, t):
    width = zc.shape[1]
    blk0 = row0 // t
    return pl.pallas_call(
        functools.partial(_conv_sample_kernel, t=t, width=width),
        grid=(dec_batch,),
        in_specs=[
            pl.BlockSpec((t, 2 * width), lambda b_: (blk0 + b_, 0)),
            pl.BlockSpec((t, width), lambda b_: (blk0 + b_, 0)),
            pl.BlockSpec((None, C_CONV - 1, width), lambda b_: (b_, 0, 0)),
            pl.BlockSpec((SUBLANES * C_CONV, width), lambda b_: (0, 0)),
            pl.BlockSpec((1, width), lambda b_: (0, 0)),
            pl.BlockSpec((1, width), lambda b_: (0, 0)),
            pl.BlockSpec((1, width), lambda b_: (0, 0)),
        ],
        out_specs=[
            pl.BlockSpec((t, width), lambda b_: (b_, 0)),
            pl.BlockSpec((None, C_CONV - 1, width), lambda b_: (b_, 0, 0)),
        ],
        out_shape=[
            jax.ShapeDtypeStruct((dec_batch * t, width), BF16),
            jax.ShapeDtypeStruct((dec_batch, C_CONV - 1, width), F32),
        ],
        scratch_shapes=[pltpu.VMEM((SUBLANES, CONV_PAD + t, width), F32), pltpu.VMEM((t, width), F32)],
        compiler_params=_cparams(("parallel",)),
        name="conv_sample",
    )(glu, zc, state, w, b, ln_g, ln_b)


def _merge_kernel(a1, a2, b1, b2, c1, c2, x1, x2, p1, p2, gates_ref, wa_ref, wb_ref, wc_ref, wo_ref,
                  pg_ref, wpg_ref, wple_ref, fg_ref, *o_refs, d, final, n_first):
    pick = functools.partial(_pick, n_first)
    merged = (jax.nn.sigmoid(gates_ref[:, 0:d]) * _dot(pick(a1, a2), wa_ref[...])
              + jax.nn.sigmoid(gates_ref[:, d:2 * d]) * _dot(pick(b1, b2), wb_ref[...])
              + jax.nn.sigmoid(gates_ref[:, 2 * d:3 * d]) * _dot(pick(c1, c2), wc_ref[...]))
    x = pick(x1, x2) + _dot(merged.astype(BF16), wo_ref[...])
    gate = jax.nn.sigmoid(_dot(_rms(x, pg_ref[...]).astype(BF16), wpg_ref[...]))
    x = x + gate * _dot(pick(p1, p2).astype(BF16), wple_ref[...])
    if not final:
        o_refs[0][...] = x
        return
    y = _rms(x, fg_ref[...])
    on_first = pl.program_id(0) < n_first

    @pl.when(on_first)
    def _():
        o_refs[0][...] = y

    @pl.when(jnp.logical_not(on_first))
    def _():
        o_refs[1][...] = y


def _merge(a, b, c, x_pair, p_quad, gates, wa, wb, wc, wo, pg, wpg, wple, fg, final):
    x1, x2, n_first, x_off = x_pair
    p = p_quad[:2]
    m, d = gates.shape[0], x1.shape[1]
    t = ROW_TILE
    pair = lambda arrs, off=0, off1=0: _split_specs(t, arrs[0].shape[1], n_first, off, off1)
    full = lambda arr: pl.BlockSpec(arr.shape, lambda i: (0, 0), pipeline_mode=pl.Buffered(1))
    if final:
        out_specs = _split_specs(t, d, n_first, 0)
        out_shape = [jax.ShapeDtypeStruct((n_first * t, d), F32), jax.ShapeDtypeStruct((m - n_first * t, d), F32)]
    else:
        out_specs = [pl.BlockSpec((t, d), lambda i: (i, 0))]
        out_shape = [jax.ShapeDtypeStruct((m, d), F32)]
    return pl.pallas_call(
        functools.partial(_merge_kernel, d=d, final=final, n_first=n_first),
        grid=(m // t,),
        in_specs=(pair(a) + pair(b) + pair(c) + pair((x1, x2), x_off) + pair(p, p_quad[3], p_quad[2])
                  + [pl.BlockSpec((t, 3 * d), lambda i: (i, 0))]
                  + [full(wa), full(wb), full(wc), full(wo), full(pg), full(wpg), full(wple), full(fg)]),
        out_specs=out_specs,
        out_shape=out_shape,
        compiler_params=_cparams(("arbitrary",)),
        name="merge",
    )(*a, *b, *c, x1, x2, *p, gates, wa, wb, wc, wo, pg, wpg, wple, fg)


def kernel(x_prompt, x_sample, cache_k, cache_v, cache_idx_k, state_conv, p_prompt, p_sample, norm_g, w_in, gmlp_ln_g, gmlp_ln_b, gmlp_ws, gmlp_bs, conv_w, conv_b, conv_ln_g, conv_ln_b, w_branch_a, w_branch_b, w_branch_c, w_out, ple_norm_g, w_ple_gate, w_ple, final_norm_g):
    batch, seq, d = x_prompt.shape
    dec_batch, dec_seq, _ = x_sample.shape
    depth = w_in.shape[0]
    past = cache_k.shape[2]
    bw = gmlp_ln_g.shape[1]
    cw = conv_b.shape[1]
    mp = batch * seq
    ms = dec_batch * dec_seq
    assert mp % ROW_TILE == 0 and ms % ROW_TILE == 0 and seq % ROW_TILE == 0
    assert dec_seq <= CHUNK and B_CHUNK % dec_seq == 0 and past % CHUNK == 0

    widths = (A_WIDTH, A_KV_WIDTH, A_KV_WIDTH, IDX_HEADS * IDX_DIM, IDX_DIM, IDX_HEADS, A_WIDTH,
              2 * bw, bw, 2 * cw, cw, N_BRANCH * d)
    cuts = np.concatenate([[0], np.cumsum(widths)])
    out_widths = (QW, KVW, A_WIDTH, 2 * bw, bw, 2 * cw, cw, N_BRANCH * d)

    tail0 = QW + KVW
    placed = [(0, 0), (3, Q_IQ), (5, Q_IW), (1, QW), (2, QW + KV_V), (4, QW + KV_IK),
              (6, tail0)]
    pieces = [(int(cuts[j]), int(widths[j]), dst) for j, dst in placed[:-1]]
    pieces.append((int(cuts[6]), int(cuts[12] - cuts[6]), tail0))
    w1 = _wprep(jnp.transpose(w_in, (2, 0, 1)), tuple(pieces), sum(out_widths))

    n_first = mp // ROW_TILE
    x_pair = (x_prompt.reshape(mp, d), x_sample.reshape(ms, d), n_first, 0)
    rep = B_CHUNK // dec_seq
    eye = jnp.eye(rep, dtype=F32)
    p_all_prompt = p_prompt.reshape(depth * mp, -1)
    p_all_sample = p_sample.reshape(depth * ms, -1)
    ck_t = jnp.transpose(cache_k, (0, 1, 3, 4, 2)).reshape(depth * dec_batch * A_KV_WIDTH, past)
    cv_t = jnp.transpose(cache_v, (0, 1, 3, 4, 2)).reshape(depth * dec_batch * A_KV_WIDTH, past)
    cik_t = jnp.transpose(cache_idx_k, (0, 1, 3, 2)).reshape(depth * dec_batch * IDX_DIM, past)

    kp, vp, ikp, cvp, ks, vs, iks, cvs, gvs = ([] for _ in range(9))
    for l in range(depth):
        pq, pkv, za, uv, zb, glu, zc, gates = _proj(x_pair, mp + ms, norm_g[l][None, :], w1, l, out_widths)

        a_p, k_t, v_t, ik_t = _attn_prompt(pq, pkv, za, batch, seq)
        a_s = _attn_sample(pq, pkv, za, ck_t, cv_t, cik_t, l, mp, dec_batch, dec_seq)

        ln_g, ln_b = gmlp_ln_g[l][None, :], gmlp_ln_b[l][None, :]
        (b_p,) = _gmlp(uv, zb, ln_g, ln_b, gmlp_ws[l], gmlp_bs[l][:, :, None], 0, mp, True, False)
        ws_s = jnp.einsum('ab,gij->gaibj', eye, gmlp_ws[l][:, :dec_seq, :dec_seq]).reshape(
            B_GROUPS, B_CHUNK, B_CHUNK)
        bs_s = jnp.tile(gmlp_bs[l][:, :dec_seq], (1, rep))[:, :, None]
        b_s, gv = _gmlp(uv, zb, ln_g, ln_b, ws_s, bs_s, mp, ms, False, True)

        cargs = (jnp.repeat(conv_w[l], SUBLANES, axis=0), conv_b[l][None, :],
                 conv_ln_g[l][None, :], conv_ln_b[l][None, :])
        c_p, tail_p = _conv_prompt(glu, zc, *cargs, batch, seq)
        c_s, tail_s = _conv_sample(glu, zc, state_conv[l], *cargs, mp, dec_batch, dec_seq)

        p = (p_all_prompt, p_all_sample, l * n_first, l * (ms // ROW_TILE))
        outs = _merge((a_p, a_s), (b_p, b_s), (c_p, c_s), x_pair, p, gates,
                      w_branch_a[l].astype(BF16), w_branch_b[l].astype(BF16), w_branch_c[l].astype(BF16),
                      w_out[l].astype(BF16), ple_norm_g[l][None, :], w_ple_gate[l].astype(BF16),
                      w_ple[l].astype(BF16), final_norm_g[None, :], l == depth - 1)
        x_pair = (outs[0], outs[0], n_first, n_first)

        heads_last = lambda x_t: jnp.transpose(x_t.reshape(batch, A_KV_HEADS, A_HEAD_DIM, seq), (0, 3, 1, 2))
        kp.append(heads_last(k_t))
        vp.append(heads_last(v_t))
        ikp.append(jnp.transpose(ik_t.reshape(batch, IDX_DIM, seq), (0, 2, 1)))
        cvp.append(tail_p)
        ks.append(pkv[mp:, 0:A_KV_WIDTH].reshape(dec_batch, dec_seq, A_KV_HEADS, A_HEAD_DIM))
        vs.append(pkv[mp:, KV_V:KV_V + A_KV_WIDTH].reshape(dec_batch, dec_seq, A_KV_HEADS, A_HEAD_DIM))
        iks.append(pkv[mp:, KV_IK:KV_IK + IDX_DIM].reshape(dec_batch, dec_seq, IDX_DIM))
        cvs.append(tail_s)
        gvs.append(gv.reshape(dec_batch, dec_seq, bw))

    return (outs[0].reshape(batch, seq, d), outs[1].reshape(dec_batch, dec_seq, d),
            jnp.stack(kp), jnp.stack(vp), jnp.stack(ikp), jnp.stack(cvp),
            jnp.stack(ks), jnp.stack(vs), jnp.stack(iks), jnp.stack(cvs), jnp.stack(gvs))
```

```python
import functools

import jax
import jax.numpy as jnp
import numpy as np
from jax import lax
from jax.experimental import pallas as pl
from jax.experimental.pallas import tpu as pltpu

F32 = jnp.float32
BF16 = jnp.bfloat16

CHUNK = 64
EPS = 1e-6
A_HEAD_DIM = 64
A_HEADS = 8
A_KV_HEADS = 2
A_REP = A_HEADS // A_KV_HEADS
A_WIDTH = A_HEADS * A_HEAD_DIM
A_KV_WIDTH = A_KV_HEADS * A_HEAD_DIM
IDX_HEADS = 4
IDX_DIM = 64
IDX_W_SCALE = (IDX_HEADS * IDX_DIM) ** -0.5
TOPK_MAX = 256
B_GROUPS = 4
B_CHUNK = 128
C_CONV = 31
N_BRANCH = 3

QW = 896
KVW = 384
Q_IQ = A_WIDTH
Q_IW = A_WIDTH + IDX_HEADS * IDX_DIM
KV_V = A_KV_WIDTH
KV_IK = 2 * A_KV_WIDTH

VMEM_LIMIT = 56 * 1024 * 1024
ROW_TILE = 256
KEY_TILE = 256
BISECT_STEPS = 16
NEG_INF = float("-inf")


def _cparams(sem):
    return pltpu.CompilerParams(dimension_semantics=sem, vmem_limit_bytes=VMEM_LIMIT)


def _rms(xf, g):
    return xf * lax.rsqrt(jnp.mean(jnp.square(xf), axis=-1, keepdims=True) + EPS) * g


def _ln(xf, g, b):
    xc = xf - jnp.mean(xf, axis=-1, keepdims=True)
    var = jnp.mean(jnp.square(xc), axis=-1, keepdims=True)
    return xc * lax.rsqrt(var + EPS) * g + b


def _silu(x):
    return x * jax.nn.sigmoid(x)


def _dot(a, b):
    return jnp.dot(a, b, preferred_element_type=F32)


def _dot_nt(a, b):
    return lax.dot_general(a, b, (((1,), (1,)), ((), ())), preferred_element_type=F32)


WPREP_ROWS = 128


def _wprep_kernel(src_ref, cnt_ref, w_ref, o_ref):
    i = pl.program_id(0)
    keep = lax.broadcasted_iota(jnp.int32, (WPREP_ROWS, w_ref.shape[2]), 0) < cnt_ref[i]
    for l in range(w_ref.shape[1]):
        o_ref[l] = jnp.where(keep, w_ref[:, l, :], 0.0).astype(BF16)


def _wprep(w_t, pieces, n_out):
    n, depth, d = w_t.shape
    src, cnt = [], []
    for dst0 in range(0, n_out, WPREP_ROWS):
        hit = [(s + dst0 - t, min(WPREP_ROWS, t + wd - dst0)) for s, wd, t in pieces if t <= dst0 < t + wd]
        assert len(hit) == 1 and hit[0][0] + WPREP_ROWS <= n
        src.append(hit[0][0])
        cnt.append(hit[0][1])
    grid_spec = pltpu.PrefetchScalarGridSpec(
        num_scalar_prefetch=2,
        grid=(n_out // WPREP_ROWS,),
        in_specs=[pl.BlockSpec((pl.Element(WPREP_ROWS), pl.Element(depth), pl.Element(d)),
                               lambda i, src_ref, cnt_ref: (src_ref[i], 0, 0))],
        out_specs=pl.BlockSpec((depth, WPREP_ROWS, d), lambda i, src_ref, cnt_ref: (0, i, 0)),
    )
    return pl.pallas_call(
        _wprep_kernel,
        grid_spec=grid_spec,
        out_shape=jax.ShapeDtypeStruct((depth, n_out, d), BF16),
        compiler_params=_cparams(("parallel",)),
        name="wprep",
    )(jnp.asarray(src, jnp.int32), jnp.asarray(cnt, jnp.int32), w_t)


def _split_specs(t, width, n_first, second_off, first_off=0):
    first = pl.BlockSpec((t, width), lambda i: (jnp.minimum(i, n_first - 1) + first_off, 0))
    second = pl.BlockSpec((t, width), lambda i: (jnp.maximum(i - n_first, 0) + second_off, 0))
    return [first, second]


def _pick(n_first, first_ref, second_ref):
    return jnp.where(pl.program_id(0) < n_first, first_ref[...], second_ref[...])


def _proj_kernel(x1_ref, x2_ref, g_ref, w_ref, *out_refs, widths, n_first):
    h = _rms(_pick(n_first, x1_ref, x2_ref), g_ref[...]).astype(BF16)
    off = 0
    for o_ref, wd in zip(out_refs, widths):
        for c in range(0, wd, 512):
            cw = min(512, wd - c)
            o_ref[:, c:c + cw] = _dot_nt(h, w_ref[off + c:off + c + cw, :])
        off += wd


def _proj(x_pair, m, g, w, layer, widths):
    x1, x2, n_first, second_off = x_pair
    d = x1.shape[1]
    n = w.shape[1]
    return pl.pallas_call(
        functools.partial(_proj_kernel, widths=widths, n_first=n_first),
        grid=(m // ROW_TILE,),
        in_specs=_split_specs(ROW_TILE, d, n_first, second_off) + [
            pl.BlockSpec((1, d), lambda i: (0, 0)),
            pl.BlockSpec((None, n, d), lambda i: (layer, 0, 0), pipeline_mode=pl.Buffered(1)),
        ],
        out_specs=[pl.BlockSpec((ROW_TILE, wd), lambda i: (i, 0)) for wd in widths],
        out_shape=[jax.ShapeDtypeStruct((m, wd), F32) for wd in widths],
        compiler_params=_cparams(("parallel",)),
        name="proj",
    )(x1, x2, g, w)


LANES = 128
TQ_PROMPT = 256
TQ_SAMPLE = 128


def _rows8(x):
    return [x[8 * j:8 * j + 8] for j in range(x.shape[0] // 8)]


def _tree(parts, op):
    while len(parts) > 1:
        nxt = [op(parts[2 * j], parts[2 * j + 1]) for j in range(len(parts) // 2)]
        if len(parts) % 2:
            nxt.append(parts[-1])
        parts = nxt
    return parts[0]


def _all8(x, op):
    for shift in (4, 2, 1):
        x = op(x, pltpu.roll(x, shift, 0))
    return x


def _fold(x, op):
    return _all8(_tree(_rows8(x), op), op)


def _per_rows(op, x, v8):
    return jnp.concatenate([op(part, v8) for part in _rows8(x)], axis=0)


def _head_slabs(x, n, dst):
    t = x.shape[0]
    keep = (lax.broadcasted_iota(jnp.int32, (t, LANES), 1) // A_HEAD_DIM) == dst
    parts = []
    for h in range(n):
        slab = x[:, (h // 2) * LANES:(h // 2 + 1) * LANES]
        if h % 2 != dst:
            slab = pltpu.roll(slab, A_HEAD_DIM, 1)
        parts.append(jnp.where(keep, slab, 0.0))
    return parts


def _to_cols(x):
    return jnp.concatenate([x[c:c + LANES, :].T for c in range(0, KEY_TILE, LANES)], axis=1)


def _to_rows(x_t):
    return jnp.concatenate([x_t[:, c:c + LANES].T for c in range(0, KEY_TILE, LANES)], axis=0)


def _fill_keys(kt, j, k, v_t, ik, kb_ref, vt_ref, ikb_ref):
    rows = pl.ds(kt * KEY_TILE, KEY_TILE)
    kb_ref[rows, j * LANES:(j + 1) * LANES] = k.astype(BF16)
    ikb_ref[rows, j * LANES:(j + 1) * LANES] = ik.astype(BF16)
    for g in range(A_KV_HEADS):
        vt_ref[kt, g, j * A_HEAD_DIM:(j + 1) * A_HEAD_DIM, :] = (
            v_t[g * A_HEAD_DIM:(g + 1) * A_HEAD_DIM, :].astype(BF16))


def _attend(qrow, za, kb_ref, vt_ref, ikb_ref, score_ref, lg_ref, p_ref, acc_ref, *, nt, q_pos0, n_keys, topk, nb=1):
    tk = KEY_TILE
    TQ = qrow.shape[0]
    tqb = TQ // nb

    def spread(x):
        if nb == 1:
            return x
        row_set = lax.broadcasted_iota(jnp.int32, x.shape, 0) // tqb
        return jnp.concatenate([jnp.where(row_set == j, x, 0.0) for j in range(nb)], axis=1)

    iq_all = jnp.concatenate([spread(s).astype(BF16) for s in _head_slabs(qrow[:, Q_IQ:Q_IW], IDX_HEADS, 0)],
                             axis=0)
    iw_t = qrow[:, Q_IW:Q_IW + LANES].T * IDX_W_SCALE
    iw_rows = [iw_t[h:h + 1, :] for h in range(IDX_HEADS)]

    k_row = lax.broadcasted_iota(jnp.int32, (tk, TQ), 0)
    q_pos = q_pos0 + lax.broadcasted_iota(jnp.int32, (tk, TQ), 1) % tqb
    n_vis_q = jnp.minimum((q_pos // CHUNK + 1) * CHUNK, n_keys)

    def visible(kt):
        k_pos = kt * tk + k_row
        return k_pos < n_vis_q, k_pos

    def score_body(kt, carry):
        mn, mx = carry
        ik_t = ikb_ref[pl.ds(pl.multiple_of(kt * tk, tk), tk), :]
        r = _dot_nt(ik_t, iq_all)
        s = jnp.zeros((tk, TQ), F32)
        for h in range(IDX_HEADS):
            s = s + jnp.maximum(r[:, h * TQ:(h + 1) * TQ], 0.0) * iw_rows[h]
        vis, _ = visible(kt)
        masked = jnp.where(vis, s, NEG_INF)
        score_ref[kt] = masked
        mn = jnp.minimum(mn, _tree(_rows8(jnp.where(vis, s, jnp.inf)), jnp.minimum))
        mx = jnp.maximum(mx, _tree(_rows8(masked), jnp.maximum))
        return mn, mx

    mn, mx = lax.fori_loop(0, nt, score_body,
                           (jnp.full((8, TQ), jnp.inf, F32), jnp.full((8, TQ), NEG_INF, F32)))
    mx = _all8(mx, jnp.maximum)

    q_pos8 = q_pos0 + lax.broadcasted_iota(jnp.int32, (8, TQ), 1) % tqb
    n_vis = jnp.minimum((q_pos8 // CHUNK + 1) * CHUNK, n_keys)
    k_eff = jnp.minimum(n_vis, topk).astype(F32)

    def count_ge(thr):
        def body(kt, acc):
            return acc + _tree([jnp.where(part >= thr, 1.0, 0.0) for part in _rows8(score_ref[kt])], jnp.add)
        return _all8(lax.fori_loop(0, nt, body, jnp.zeros((8, TQ), F32)), jnp.add)

    def bisect_step(_, state):
        lo, hi, c_lo, c_hi = state
        mid = lo + (hi - lo) * 0.5
        inside = (mid > lo) & (mid < hi)
        c = count_ge(mid)
        up = inside & (c >= k_eff)
        dn = inside & (c < k_eff)
        return (jnp.where(up, mid, lo), jnp.where(dn, mid, hi), jnp.where(up, c, c_lo), jnp.where(dn, c, c_hi))

    def band_max(lo, hi):
        def body(kt, acc):
            parts = [jnp.where((part >= lo) & (part < hi), part, NEG_INF) for part in _rows8(score_ref[kt])]
            return jnp.maximum(acc, _tree(parts, jnp.maximum))
        return _all8(lax.fori_loop(0, nt, body, jnp.full((8, TQ), NEG_INF, F32)), jnp.maximum)

    def peel_step(state):
        it, _, lo, hi, c_hi, done = state
        top = band_max(lo, hi)
        c_top = count_ge(top)
        hit = c_top >= k_eff
        cut = jnp.logical_not(hit) & (done == 0.0)
        lo = jnp.where(hit & (done == 0.0), top, lo)
        hi = jnp.where(cut, top, hi)
        c_hi = jnp.where(cut, c_top, c_hi)
        done = jnp.where(hit, 1.0, done)
        return it + 1, jnp.max(1.0 - done).astype(F32), lo, hi, c_hi, done

    lo0 = _all8(mn, jnp.minimum)
    hi0 = mx + jnp.maximum(jnp.abs(mx), 1e-30) * (2.0 ** -10)
    lo, hi, c_lo, c_hi = lax.fori_loop(
        0, BISECT_STEPS, bisect_step, (lo0, hi0, n_vis.astype(F32), jnp.zeros((8, TQ), F32)))
    done0 = jnp.where(c_lo == k_eff, 1.0, 0.0)
    init = (jnp.int32(0), jnp.max(1.0 - done0).astype(F32), lo, hi, c_hi, done0)
    _, _, lo, hi, c_hi, _ = lax.while_loop(lambda st: (st[0] < n_keys) & (st[1] > 0.0), peel_step, init)
    lo_r, hi_r = lo[0:1], hi[0:1]
    need_r = (k_eff - c_hi)[0:1]

    tril = (lax.broadcasted_iota(jnp.int32, (tk, tk), 1)
            <= lax.broadcasted_iota(jnp.int32, (tk, tk), 0)).astype(BF16)

    def select_body(kt, carry):
        s = score_ref[kt]
        inb = (s >= lo_r) & (s < hi_r)
        band = jnp.where(inb, 1.0, 0.0)
        rank = _dot(tril, band.astype(BF16)) + carry
        sel = (s >= hi_r) | (inb & (rank <= need_r))
        _, k_pos = visible(kt)
        dist = jnp.abs(q_pos - k_pos).astype(F32)
        score_ref[kt] = jnp.where(sel, -dist, NEG_INF)
        return carry + jnp.sum(band, axis=0, keepdims=True)

    lax.fori_loop(0, nt, select_body, jnp.zeros((1, TQ), F32))

    slopes = [2.0 ** (-8.0 * (h + 1) / A_HEADS) for h in range(A_HEADS)]
    gw = A_REP * A_HEAD_DIM
    q_gs = []
    for g in range(A_KV_HEADS):
        q_g = jnp.concatenate([spread(s) for s in _head_slabs(qrow[:, g * gw:(g + 1) * gw], A_REP, g)], axis=0)
        q_gs.append((q_g * (A_HEAD_DIM ** -0.5)).astype(BF16))
    acc_ref[...] = jnp.zeros(acc_ref.shape, F32)
    lane_set = (lax.broadcasted_iota(jnp.int32, (A_HEAD_DIM, A_REP * TQ), 1) % TQ) // tqb

    last = A_KV_HEADS - 1
    p_ref[last] = jnp.zeros(p_ref.shape[1:], BF16)

    def add_values(kt, g, corr_all):
        pv = _dot(vt_ref[kt, g], p_ref[g])
        if nb > 1:
            pv = sum(jnp.where(lane_set == j, pv[j * A_HEAD_DIM:(j + 1) * A_HEAD_DIM], 0.0) for j in range(nb))
        acc_ref[g] = _per_rows(jnp.multiply, acc_ref[g], corr_all) + pv

    def attn_body(kt, carry):
        ms, ls, corr_last = carry
        k_t = kb_ref[pl.ds(pl.multiple_of(kt * tk, tk), tk), :]
        nd = score_ref[kt]
        for g in range(A_KV_HEADS):
            lg_ref[g] = _dot_nt(k_t, q_gs[g])
        add_values(jnp.maximum(kt - 1, 0), last, corr_last)
        ms_new, ls_new = [], []
        for g in range(A_KV_HEADS):
            corrs = []
            for r in range(A_REP):
                h = g * A_REP + r
                m_parts, l_parts, c_parts = [], [], []
                for c in range(0, TQ, LANES):
                    cols = slice(r * TQ + c, r * TQ + c + LANES)
                    m_old, l_old = ms[h][:, c:c + LANES], ls[h][:, c:c + LANES]
                    lgr = lg_ref[g, :, cols] + slopes[h] * score_ref[kt, :, c:c + LANES]
                    m_new = jnp.maximum(m_old, _fold(lgr, jnp.maximum))
                    m_safe = jnp.where(m_new == NEG_INF, 0.0, m_new)
                    p = _per_rows(lambda a, m: jnp.exp(a - m), lgr, m_safe)
                    corr = jnp.exp(m_old - m_safe)
                    l_parts.append(l_old * corr + _fold(p, jnp.add))
                    m_parts.append(m_new)
                    c_parts.append(corr)
                    p_ref[g, :, cols] = p.astype(BF16)
                ls_new.append(jnp.concatenate(l_parts, axis=1))
                ms_new.append(jnp.concatenate(m_parts, axis=1))
                corrs.append(jnp.concatenate(c_parts, axis=1))
            corr_all = jnp.concatenate(corrs, axis=1)
            if g < last:
                add_values(kt, g, corr_all)
        return tuple(ms_new), tuple(ls_new), corr_all

    init = (tuple(jnp.full((8, TQ), NEG_INF, F32) for _ in range(A_HEADS)),
            tuple(jnp.zeros((8, TQ), F32) for _ in range(A_HEADS)),
            jnp.ones((8, A_REP * TQ), F32))
    _, ls, corr_last = lax.fori_loop(0, nt, attn_body, init)
    add_values(nt - 1, last, corr_last)
    pieces = []
    for g in range(A_KV_HEADS):
        o_t = _per_rows(jnp.divide, acc_ref[g], jnp.concatenate(ls[g * A_REP:(g + 1) * A_REP], axis=1))
        for r in range(0, A_REP, 2):
            pair = jnp.concatenate([o_t[:, r * TQ:(r + 1) * TQ], o_t[:, (r + 1) * TQ:(r + 2) * TQ]], axis=0)
            pieces.append(pair.T)
    o_a = jnp.concatenate(pieces, axis=-1)
    return o_a * _silu(za)


def _attn_scratch(n_tiles, TQ, nb):
    return [
        pltpu.VMEM((n_tiles * KEY_TILE, nb * LANES), BF16),
        pltpu.VMEM((n_tiles, A_KV_HEADS, nb * A_HEAD_DIM, KEY_TILE), BF16),
        pltpu.VMEM((n_tiles * KEY_TILE, nb * LANES), BF16),
        pltpu.VMEM((n_tiles, KEY_TILE, TQ), F32),
        pltpu.VMEM((A_KV_HEADS, KEY_TILE, A_REP * TQ), F32),
        pltpu.VMEM((A_KV_HEADS, KEY_TILE, A_REP * TQ), BF16),
        pltpu.VMEM((A_KV_HEADS, A_HEAD_DIM, A_REP * TQ), F32),
    ]


def _attn_prompt_kernel(q_ref, kv_ref, za_ref, o_ref, kt_out, vt_out, ikt_out, kb_ref, vt_ref, ikb_ref, *work,
                        seq, topk):
    i = pl.program_id(1)

    @pl.when(i == 0)
    def _():
        for kt in range(seq // KEY_TILE):
            rows = slice(kt * KEY_TILE, (kt + 1) * KEY_TILE)
            k, ik = kv_ref[rows, 0:LANES], kv_ref[rows, KV_IK:KV_IK + LANES]
            v_t = _to_cols(kv_ref[rows, KV_V:KV_V + LANES])
            _fill_keys(kt, 0, k, v_t, ik, kb_ref, vt_ref, ikb_ref)
            kt_out[:, rows] = _to_cols(k)
            vt_out[:, rows] = v_t
            ikt_out[:, rows] = _to_cols(ik)[0:IDX_DIM, :]

    TQ = TQ_PROMPT
    nt = (i * TQ + TQ + KEY_TILE - 1) // KEY_TILE
    out = _attend(q_ref[...], za_ref[...], kb_ref, vt_ref, ikb_ref, *work,
                  nt=nt, q_pos0=i * TQ, n_keys=seq, topk=topk)
    o_ref[...] = out.astype(BF16)


def _attn_prompt(pq, pkv, za, batch, seq):
    TQ = TQ_PROMPT
    nq = seq // TQ
    topk = min(TOPK_MAX, seq // 4)
    return pl.pallas_call(
        functools.partial(_attn_prompt_kernel, seq=seq, topk=topk),
        grid=(batch, nq),
        in_specs=[
            pl.BlockSpec((TQ, QW), lambda b, i: (b * nq + i, 0)),
            pl.BlockSpec((seq, KVW), lambda b, i: (b, 0)),
            pl.BlockSpec((TQ, A_WIDTH), lambda b, i: (b * nq + i, 0)),
        ],
        out_specs=[
            pl.BlockSpec((TQ, A_WIDTH), lambda b, i: (b * nq + i, 0)),
            pl.BlockSpec((A_KV_WIDTH, seq), lambda b, i: (b, 0)),
            pl.BlockSpec((A_KV_WIDTH, seq), lambda b, i: (b, 0)),
            pl.BlockSpec((IDX_DIM, seq), lambda b, i: (b, 0)),
        ],
        out_shape=[
            jax.ShapeDtypeStruct((batch * seq, A_WIDTH), BF16),
            jax.ShapeDtypeStruct((batch * A_KV_WIDTH, seq), F32),
            jax.ShapeDtypeStruct((batch * A_KV_WIDTH, seq), F32),
            jax.ShapeDtypeStruct((batch * IDX_DIM, seq), F32),
        ],
        scratch_shapes=_attn_scratch(seq // KEY_TILE, TQ, 1),
        compiler_params=_cparams(("parallel", "arbitrary")),
        name="attn_prompt",
    )(pq, pkv, za)


def _attn_sample_kernel(q_ref, kv_ref, za_ref, ck_ref, cv_ref, cik_ref, o_ref,
                        kb_ref, vt_ref, ikb_ref, *work, t, past, topk, nt, nb):
    tk = KEY_TILE
    tail = nt * tk - past
    zeros = lambda n, w: jnp.zeros((n, w), F32)
    pad_rows = lambda x: jnp.concatenate([x, zeros(tail - t, LANES)], axis=0)
    for j in range(nb):
        for kt in range(past // tk):
            cols = slice(kt * tk, (kt + 1) * tk)
            ik_t = jnp.concatenate([cik_ref[j * IDX_DIM:(j + 1) * IDX_DIM, cols], zeros(LANES - IDX_DIM, tk)], axis=0)
            _fill_keys(kt, j, _to_rows(ck_ref[j * LANES:(j + 1) * LANES, cols]),
                       cv_ref[j * LANES:(j + 1) * LANES, cols], _to_rows(ik_t), kb_ref, vt_ref, ikb_ref)
        new = kv_ref[j * t:(j + 1) * t, :]
        _fill_keys(past // tk, j, pad_rows(new[:, 0:LANES]), _to_cols(pad_rows(new[:, KV_V:KV_V + LANES])),
                   pad_rows(new[:, KV_IK:KV_IK + LANES]), kb_ref, vt_ref, ikb_ref)
    out = _attend(q_ref[...], za_ref[...], kb_ref, vt_ref, ikb_ref, *work,
                  nt=nt, q_pos0=past, n_keys=past + t, topk=topk, nb=nb)
    o_ref[...] = out.astype(BF16)


def _attn_sample(pq, pkv, za, ck, cv, cik, layer, row0, dec_batch, t):
    past = ck.shape[1]
    nb = TQ_SAMPLE // t
    cblk0 = layer * (dec_batch // nb)
    assert past % KEY_TILE == 0 and t <= KEY_TILE and TQ_SAMPLE % t == 0 and dec_batch % nb == 0
    topk = min(TOPK_MAX, (past + t) // 4)
    nt = past // KEY_TILE + 1
    blk0 = row0 // TQ_SAMPLE
    return pl.pallas_call(
        functools.partial(_attn_sample_kernel, t=t, past=past, topk=topk, nt=nt, nb=nb),
        grid=(dec_batch // nb,),
        in_specs=[
            pl.BlockSpec((TQ_SAMPLE, QW), lambda b: (blk0 + b, 0)),
            pl.BlockSpec((TQ_SAMPLE, KVW), lambda b: (blk0 + b, 0)),
            pl.BlockSpec((TQ_SAMPLE, A_WIDTH), lambda b: (blk0 + b, 0)),
            pl.BlockSpec((nb * A_KV_WIDTH, past), lambda b: (cblk0 + b, 0)),
            pl.BlockSpec((nb * A_KV_WIDTH, past), lambda b: (cblk0 + b, 0)),
            pl.BlockSpec((nb * IDX_DIM, past), lambda b: (cblk0 + b, 0)),
        ],
        out_specs=pl.BlockSpec((TQ_SAMPLE, A_WIDTH), lambda b: (b, 0)),
        out_shape=jax.ShapeDtypeStruct((dec_batch * t, A_WIDTH), BF16),
        scratch_shapes=_attn_scratch(nt, TQ_SAMPLE, nb),
        compiler_params=_cparams(("parallel",)),
        name="attn_sample",
    )(pq, pkv, za, ck, cv, cik)


GMLP_ROWS = 512


def _gmlp_kernel(uv_ref, zb_ref, g_ref, b_ref, w_ref, bs_ref, o_ref, *v_out, width, mask_chunks):
    gw = width // B_GROUPS
    ws = []
    for g in range(B_GROUPS):
        w = w_ref[g]
        if mask_chunks:
            i = lax.broadcasted_iota(jnp.int32, (B_CHUNK, B_CHUNK), 0)
            j = lax.broadcasted_iota(jnp.int32, (B_CHUNK, B_CHUNK), 1)
            w = jnp.where((j // CHUNK) <= (i // CHUNK), w, 0.0)
        ws.append(w.astype(BF16))
    for c in range(uv_ref.shape[0] // B_CHUNK):
        rows = slice(c * B_CHUNK, (c + 1) * B_CHUNK)
        uv = uv_ref[rows, :]
        act = 0.5 * uv * (1.0 + lax.erf(uv * np.float32(1.0 / np.sqrt(2.0))))
        u = act[:, :width]
        v = _ln(act[:, width:], g_ref[...], b_ref[...])
        if v_out:
            v_out[0][rows, :] = v
        vb = v.astype(BF16)
        zs = _silu(zb_ref[rows, :])
        for g in range(B_GROUPS):
            cols = slice(g * gw, (g + 1) * gw)
            mixed = _dot(ws[g], vb[:, cols]) + bs_ref[g]
            o_ref[rows, cols] = (u[:, cols] * mixed * zs[:, cols]).astype(BF16)


def _gmlp(uv, zb, ln_g, ln_b, w, bs, row0, rows, mask_chunks, want_v):
    width = zb.shape[1]
    t = GMLP_ROWS
    assert row0 % t == 0 and rows % t == 0
    blk0 = row0 // t
    out_specs = [pl.BlockSpec((t, width), lambda i: (i, 0))]
    out_shape = [jax.ShapeDtypeStruct((rows, width), BF16)]
    if want_v:
        out_specs.append(pl.BlockSpec((t, width), lambda i: (i, 0)))
        out_shape.append(jax.ShapeDtypeStruct((rows, width), F32))
    return pl.pallas_call(
        functools.partial(_gmlp_kernel, width=width, mask_chunks=mask_chunks),
        grid=(rows // t,),
        in_specs=[
            pl.BlockSpec((t, 2 * width), lambda i: (blk0 + i, 0)),
            pl.BlockSpec((t, width), lambda i: (blk0 + i, 0)),
            pl.BlockSpec((1, width), lambda i: (0, 0)),
            pl.BlockSpec((1, width), lambda i: (0, 0)),
            pl.BlockSpec((B_GROUPS, B_CHUNK, B_CHUNK), lambda i: (0, 0, 0)),
            pl.BlockSpec((B_GROUPS, B_CHUNK, 1), lambda i: (0, 0, 0)),
        ],
        out_specs=out_specs,
        out_shape=out_shape,
        compiler_params=_cparams(("parallel",)),
        name="gmlp_sample" if want_v else "gmlp_prompt",
    )(uv, zb, ln_g, ln_b, w, bs)


CONV_PAD = 32


CONV_ROWS = 64
SUBLANES = 8


def _conv_tile(glu_ref, zc_ref, o_ref, ext_ref, y_ref, w_ref, b_ref, g_ref, beta_ref, t, width):
    ext_ref[0, CONV_PAD:CONV_PAD + t, :] = glu_ref[:, :width] * jax.nn.sigmoid(glu_ref[:, width:])
    n = t + CONV_PAD - SUBLANES
    for s in range(1, SUBLANES):
        ext_ref[s, 0:n, :] = ext_ref[0, s:s + n, :]
    base = CONV_PAD - (C_CONV - 1)

    rc = min(CONV_ROWS, t)

    def rows_step(ci, carry):
        r0 = pl.multiple_of(ci * rc, rc)
        y = jnp.zeros((rc, width), F32)
        for k in range(C_CONV):
            s, q = (base + k) % SUBLANES, (base + k) // SUBLANES
            tap = w_ref[SUBLANES * k:SUBLANES * (k + 1), :]
            y = y + _per_rows(jnp.multiply, ext_ref[s, pl.ds(r0 + SUBLANES * q, rc), :], tap)
        y_ref[pl.ds(r0, rc), :] = y
        return carry

    lax.fori_loop(0, t // rc, rows_step, 0)
    y = y_ref[...] + b_ref[...]
    o_ref[...] = (_silu(_ln(y, g_ref[...], beta_ref[...])) * _silu(zc_ref[...])).astype(BF16)


def _conv_prompt_kernel(glu_ref, zc_ref, w_ref, b_ref, g_ref, beta_ref, o_ref, tail_ref, ext_ref, y_ref,
                        *, t, width):
    i = pl.program_id(1)

    @pl.when(i == 0)
    def _():
        ext_ref[0, 0:CONV_PAD, :] = jnp.zeros((CONV_PAD, width), F32)

    _conv_tile(glu_ref, zc_ref, o_ref, ext_ref, y_ref, w_ref, b_ref, g_ref, beta_ref, t, width)
    tail = ext_ref[0, t:t + CONV_PAD, :]
    ext_ref[0, 0:CONV_PAD, :] = tail

    @pl.when(i == pl.num_programs(1) - 1)
    def _():
        tail_ref[...] = tail[CONV_PAD - (C_CONV - 1):, :]


def _conv_prompt(glu, zc, w, b, ln_g, ln_b, batch, seq):
    t = ROW_TILE
    width = zc.shape[1]
    nb = seq // t
    return pl.pallas_call(
        functools.partial(_conv_prompt_kernel, t=t, width=width),
        grid=(batch, nb),
        in_specs=[
            pl.BlockSpec((t, 2 * width), lambda b_, i: (b_ * nb + i, 0)),
            pl.BlockSpec((t, width), lambda b_, i: (b_ * nb + i, 0)),
            pl.BlockSpec((SUBLANES * C_CONV, width), lambda b_, i: (0, 0)),
            pl.BlockSpec((1, width), lambda b_, i: (0, 0)),
            pl.BlockSpec((1, width), lambda b_, i: (0, 0)),
            pl.BlockSpec((1, width), lambda b_, i: (0, 0)),
        ],
        out_specs=[
            pl.BlockSpec((t, width), lambda b_, i: (b_ * nb + i, 0)),
            pl.BlockSpec((None, C_CONV - 1, width), lambda b_, i: (b_, 0, 0)),
        ],
        out_shape=[
            jax.ShapeDtypeStruct((batch * seq, width), BF16),
            jax.ShapeDtypeStruct((batch, C_CONV - 1, width), F32),
        ],
        scratch_shapes=[pltpu.VMEM((SUBLANES, CONV_PAD + t, width), F32), pltpu.VMEM((t, width), F32)],
        compiler_params=_cparams(("parallel", "arbitrary")),
        name="conv_prompt",
    )(glu, zc, w, b, ln_g, ln_b)


def _conv_sample_kernel(glu_ref, zc_ref, st_ref, w_ref, b_ref, g_ref, beta_ref, o_ref, tail_ref, ext_ref, y_ref,
                        *, t, width):
    base = CONV_PAD - (C_CONV - 1)
    ext_ref[0, 0:base, :] = jnp.zeros((base, width), F32)
    ext_ref[0, base:CONV_PAD, :] = st_ref[...]
    _conv_tile(glu_ref, zc_ref, o_ref, ext_ref, y_ref, w_ref, b_ref, g_ref, beta_ref, t, width)
    tail_ref[...] = ext_ref[0, t + base:t + CONV_PAD, :]


def _conv_sample(glu, zc, state, w, b, ln_g, ln_b, row0, dec_batch, t):
    width = zc.shape[1]
    blk0 = row0 // t
    return pl.pallas_call(
        functools.partial(_conv_sample_kernel, t=t, width=width),
        grid=(dec_batch,),
        in_specs=[
            pl.BlockSpec((t, 2 * width), lambda b_: (blk0 + b_, 0)),
            pl.BlockSpec((t, width), lambda b_: (blk0 + b_, 0)),
            pl.BlockSpec((None, C_CONV - 1, width), lambda b_: (b_, 0, 0)),
            pl.BlockSpec((SUBLANES * C_CONV, width), lambda b_: (0, 0)),
            pl.BlockSpec((1, width), lambda b_: (0, 0)),
            pl.BlockSpec((1, width), lambda b_: (0, 0)),
            pl.BlockSpec((1, width), lambda b_: (0, 0)),
        ],
        out_specs=[
            pl.BlockSpec((t, width), lambda b_: (b_, 0)),
            pl.BlockSpec((None, C_CONV - 1, width), lambda b_: (b_, 0, 0)),
        ],
        out_shape=[
            jax.ShapeDtypeStruct((dec_batch * t, width), BF16),
            jax.ShapeDtypeStruct((dec_batch, C_CONV - 1, width), F32),
        ],
        scratch_shapes=[pltpu.VMEM((SUBLANES, CONV_PAD + t, width), F32), pltpu.VMEM((t, width), F32)],
        compiler_params=_cparams(("parallel",)),
        name="conv_sample",
    )(glu, zc, state, w, b, ln_g, ln_b)


def _merge_kernel(a1, a2, b1, b2, c1, c2, x1, x2, p1, p2, gates_ref, wa_ref, wb_ref, wc_ref, wo_ref,
                  pg_ref, wpg_ref, wple_ref, fg_ref, *o_refs, d, final, n_first):
    pick = functools.partial(_pick, n_first)
    merged = (jax.nn.sigmoid(gates_ref[:, 0:d]) * _dot(pick(a1, a2), wa_ref[...])
              + jax.nn.sigmoid(gates_ref[:, d:2 * d]) * _dot(pick(b1, b2), wb_ref[...])
              + jax.nn.sigmoid(gates_ref[:, 2 * d:3 * d]) * _dot(pick(c1, c2), wc_ref[...]))
    x = pick(x1, x2) + _dot(merged.astype(BF16), wo_ref[...])
    gate = jax.nn.sigmoid(_dot(_rms(x, pg_ref[...]).astype(BF16), wpg_ref[...]))
    x = x + gate * _dot(pick(p1, p2).astype(BF16), wple_ref[...])
    if not final:
        o_refs[0][...] = x
        return
    y = _rms(x, fg_ref[...])
    on_first = pl.program_id(0) < n_first

    @pl.when(on_first)
    def _():
        o_refs[0][...] = y

    @pl.when(jnp.logical_not(on_first))
    def _():
        o_refs[1][...] = y


def _merge(a, b, c, x_pair, p_quad, gates, wa, wb, wc, wo, pg, wpg, wple, fg, final):
    x1, x2, n_first, x_off = x_pair
    p = p_quad[:2]
    m, d = gates.shape[0], x1.shape[1]
    t = ROW_TILE
    pair = lambda arrs, off=0, off1=0: _split_specs(t, arrs[0].shape[1], n_first, off, off1)
    full = lambda arr: pl.BlockSpec(arr.shape, lambda i: (0, 0), pipeline_mode=pl.Buffered(1))
    if final:
        out_specs = _split_specs(t, d, n_first, 0)
        out_shape = [jax.ShapeDtypeStruct((n_first * t, d), F32), jax.ShapeDtypeStruct((m - n_first * t, d), F32)]
    else:
        out_specs = [pl.BlockSpec((t, d), lambda i: (i, 0))]
        out_shape = [jax.ShapeDtypeStruct((m, d), F32)]
    return pl.pallas_call(
        functools.partial(_merge_kernel, d=d, final=final, n_first=n_first),
        grid=(m // t,),
        in_specs=(pair(a) + pair(b) + pair(c) + pair((x1, x2), x_off) + pair(p, p_quad[3], p_quad[2])
                  + [pl.BlockSpec((t, 3 * d), lambda i: (i, 0))]
                  + [full(wa), full(wb), full(wc), full(wo), full(pg), full(wpg), full(wple), full(fg)]),
        out_specs=out_specs,
        out_shape=out_shape,
        compiler_params=_cparams(("arbitrary",)),
        name="merge",
    )(*a, *b, *c, x1, x2, *p, gates, wa, wb, wc, wo, pg, wpg, wple, fg)


def kernel(x_prompt, x_sample, cache_k, cache_v, cache_idx_k, state_conv, p_prompt, p_sample, norm_g, w_in, gmlp_ln_g, gmlp_ln_b, gmlp_ws, gmlp_bs, conv_w, conv_b, conv_ln_g, conv_ln_b, w_branch_a, w_branch_b, w_branch_c, w_out, ple_norm_g, w_ple_gate, w_ple, final_norm_g):
    batch, seq, d = x_prompt.shape
    dec_batch, dec_seq, _ = x_sample.shape
    depth = w_in.shape[0]
    past = cache_k.shape[2]
    bw = gmlp_ln_g.shape[1]
    cw = conv_b.shape[1]
    mp = batch * seq
    ms = dec_batch * dec_seq
    assert mp % ROW_TILE == 0 and ms % ROW_TILE == 0 and seq % ROW_TILE == 0
    assert dec_seq <= CHUNK and B_CHUNK % dec_seq == 0 and past % CHUNK == 0

    widths = (A_WIDTH, A_KV_WIDTH, A_KV_WIDTH, IDX_HEADS * IDX_DIM, IDX_DIM, IDX_HEADS, A_WIDTH,
              2 * bw, bw, 2 * cw, cw, N_BRANCH * d)
    cuts = np.concatenate([[0], np.cumsum(widths)])
    out_widths = (QW, KVW, A_WIDTH, 2 * bw, bw, 2 * cw, cw, N_BRANCH * d)

    tail0 = QW + KVW
    placed = [(0, 0), (3, Q_IQ), (5, Q_IW), (1, QW), (2, QW + KV_V), (4, QW + KV_IK),
              (6, tail0)]
    pieces = [(int(cuts[j]), int(widths[j]), dst) for j, dst in placed[:-1]]
    pieces.append((int(cuts[6]), int(cuts[12] - cuts[6]), tail0))
    w1 = _wprep(jnp.transpose(w_in, (2, 0, 1)), tuple(pieces), sum(out_widths))

    n_first = mp // ROW_TILE
    x_pair = (x_prompt.reshape(mp, d), x_sample.reshape(ms, d), n_first, 0)
    rep = B_CHUNK // dec_seq
    eye = jnp.eye(rep, dtype=F32)
    p_all_prompt = p_prompt.reshape(depth * mp, -1)
    p_all_sample = p_sample.reshape(depth * ms, -1)
    ck_t = jnp.transpose(cache_k, (0, 1, 3, 4, 2)).reshape(depth * dec_batch * A_KV_WIDTH, past)
    cv_t = jnp.transpose(cache_v, (0, 1, 3, 4, 2)).reshape(depth * dec_batch * A_KV_WIDTH, past)
    cik_t = jnp.transpose(cache_idx_k, (0, 1, 3, 2)).reshape(depth * dec_batch * IDX_DIM, past)

    kp, vp, ikp, cvp, ks, vs, iks, cvs, gvs = ([] for _ in range(9))
    for l in range(depth):
        pq, pkv, za, uv, zb, glu, zc, gates = _proj(x_pair, mp + ms, norm_g[l][None, :], w1, l, out_widths)

        a_p, k_t, v_t, ik_t = _attn_prompt(pq, pkv, za, batch, seq)
        a_s = _attn_sample(pq, pkv, za, ck_t, cv_t, cik_t, l, mp, dec_batch, dec_seq)

        ln_g, ln_b = gmlp_ln_g[l][None, :], gmlp_ln_b[l][None, :]
        (b_p,) = _gmlp(uv, zb, ln_g, ln_b, gmlp_ws[l], gmlp_bs[l][:, :, None], 0, mp, True, False)
        ws_s = jnp.einsum('ab,gij->gaibj', eye, gmlp_ws[l][:, :dec_seq, :dec_seq]).reshape(
            B_GROUPS, B_CHUNK, B_CHUNK)
        bs_s = jnp.tile(gmlp_bs[l][:, :dec_seq], (1, rep))[:, :, None]
        b_s, gv = _gmlp(uv, zb, ln_g, ln_b, ws_s, bs_s, mp, ms, False, True)

        cargs = (jnp.repeat(conv_w[l], SUBLANES, axis=0), conv_b[l][None, :],
                 conv_ln_g[l][None, :], conv_ln_b[l][None, :])
        c_p, tail_p = _conv_prompt(glu, zc, *cargs, batch, seq)
        c_s, tail_s = _conv_sample(glu, zc, state_conv[l], *cargs, mp, dec_batch, dec_seq)

        p = (p_all_prompt, p_all_sample, l * n_first, l * (ms // ROW_TILE))
        outs = _merge((a_p, a_s), (b_p, b_s), (c_p, c_s), x_pair, p, gates,
                      w_branch_a[l].astype(BF16), w_branch_b[l].astype(BF16), w_branch_c[l].astype(BF16),
                      w_out[l].astype(BF16), ple_norm_g[l][None, :], w_ple_gate[l].astype(BF16),
                      w_ple[l].astype(BF16), final_norm_g[None, :], l == depth - 1)
        x_pair = (outs[0], outs[0], n_first, n_first)

        heads_last = lambda x_t: jnp.transpose(x_t.reshape(batch, A_KV_HEADS, A_HEAD_DIM, seq), (0, 3, 1, 2))
        kp.append(heads_last(k_t))
        vp.append(heads_last(v_t))
        ikp.append(jnp.transpose(ik_t.reshape(batch, IDX_DIM, seq), (0, 2, 1)))
        cvp.append(tail_p)
        ks.append(pkv[mp:, 0:A_KV_WIDTH].reshape(dec_batch, dec_seq, A_KV_HEADS, A_HEAD_DIM))
        vs.append(pkv[mp:, KV_V:KV_V + A_KV_WIDTH].reshape(dec_batch, dec_seq, A_KV_HEADS, A_HEAD_DIM))
        iks.append(pkv[mp:, KV_IK:KV_IK + IDX_DIM].reshape(dec_batch, dec_seq, IDX_DIM))
        cvs.append(tail_s)
        gvs.append(gv.reshape(dec_batch, dec_seq, bw))

    return (outs[0].reshape(batch, seq, d), outs[1].reshape(dec_batch, dec_seq, d),
            jnp.stack(kp), jnp.stack(vp), jnp.stack(ikp), jnp.stack(cvp),
            jnp.stack(ks), jnp.stack(vs), jnp.stack(iks), jnp.stack(cvs), jnp.stack(gvs))
```
